```python
import jax, jax.numpy as jnp
from jax import lax
import numpy as np

D_MODEL = 1024
BATCH = 8
SEQ = 2048
DEPTH = 2

CHUNK = 64
N_LEFT_CHUNKS = 8
BAND_CHUNKS = N_LEFT_CHUNKS + 1
HEAD_DIM = 64
RWKV_WIDTH = D_MODEL // 2
ATTN_WIDTH = D_MODEL - RWKV_WIDTH
RWKV_HEADS = RWKV_WIDTH // HEAD_DIM
ATTN_HEADS = ATTN_WIDTH // HEAD_DIM
DECAY_LORA = 64
AAA_LORA = 64
GATE_LORA = 128
REL_CLIP = 128
D_FF = 2816
N_EXPERTS = 8
TOP_K = 2
D_FF_EXPERT = D_FF // 2
N_DENSE = (DEPTH + 1) // 2
N_MOE = DEPTH // 2
RMS_EPS = 1e-6
GN_EPS = 64e-5
MASK_VALUE = -1e30
SHIFT_SIZES = (RWKV_WIDTH, RWKV_WIDTH, RWKV_WIDTH, DECAY_LORA, AAA_LORA, GATE_LORA)
ATTN_SIZES = (ATTN_WIDTH, ATTN_WIDTH, ATTN_WIDTH)
SHIFT_COLS = 3 * RWKV_WIDTH + DECAY_LORA + AAA_LORA + GATE_LORA
IN_COLS = SHIFT_COLS + 3 * ATTN_WIDTH

kernel_name = 'hybrid_rwkv7_chunkattn_moe_encoder'


def _split(t, sizes):
    out, start = [], 0
    for s in sizes:
        out.append(t[..., start:start + s])
        start += s
    return out


def _rmsnorm(x, g):
    xf = x.astype(jnp.float32)
    y = xf * lax.rsqrt(jnp.mean(xf * xf, axis=-1, keepdims=True) + RMS_EPS)
    return (y * g.astype(jnp.float32)).astype(x.dtype)


def _token_shift(h, mu):
    prev = jnp.pad(h, ((0, 0), (1, 0), (0, 0)))[:, :-1]
    return h + mu.astype(h.dtype) * (prev - h)


def _rwkv7(r, k, v, wd, ad, gd, w0, w2, a0, a2, g2, k_k, k_a, r_k, ln_w, ln_b):
    B, S, _ = r.shape
    dt = r.dtype
    f = lambda t: t.astype(jnp.float32)
    r, k, v = f(r), f(k), f(v)
    w = f(w0) + jnp.tanh(f(wd)) @ f(w2)
    w = -jax.nn.softplus(-w) - 0.5
    decay = jnp.exp(-jnp.exp(w))
    a = jax.nn.sigmoid(f(a0) + f(ad) @ f(a2))
    g = jax.nn.sigmoid(f(gd)) @ f(g2)
    heads = lambda t: t.reshape(B, S, RWKV_HEADS, HEAD_DIM)
    kk = heads(k * f(k_k))
    kk = kk / jnp.maximum(jnp.linalg.norm(kk, axis=-1, keepdims=True), 1e-12)
    k = k * (1.0 + (a - 1.0) * f(k_a))
    rh, kh, vh, wh, ah = heads(r), heads(k), heads(v), heads(decay), heads(a)

    def step(state, inp):
        r_t, w_t, k_t, v_t, kk_t, a_t = inp
        sa = jnp.einsum('bhvk,bhk->bhv', state, -kk_t)
        state = (state * w_t[:, :, None, :]
                 + sa[..., None] * (kk_t * a_t)[:, :, None, :]
                 + v_t[..., None] * k_t[:, :, None, :])
        y_t = jnp.einsum('bhvk,bhk->bhv', state, r_t)
        return state, y_t

    xs = tuple(jnp.swapaxes(t, 0, 1) for t in (rh, wh, kh, vh, kk, ah))
    s0 = jnp.zeros((B, RWKV_HEADS, HEAD_DIM, HEAD_DIM), jnp.float32)
    _, y = lax.scan(step, s0, xs)
    y = jnp.swapaxes(y, 0, 1)
    mu = jnp.mean(y, axis=-1, keepdims=True)
    var = jnp.mean(jnp.square(y - mu), axis=-1, keepdims=True)
    y = ((y - mu) * lax.rsqrt(var + GN_EPS)).reshape(B, S, RWKV_WIDTH) * f(ln_w) + f(ln_b)
    r_k_h = f(r_k).reshape(RWKV_HEADS, HEAD_DIM)
    bonus = jnp.sum(rh * kh * r_k_h, axis=-1, keepdims=True) * vh
    y = (y + bonus.reshape(B, S, RWKV_WIDTH)) * g
    return y.astype(dt)


def _chunk_attention(q, k, v, rel_bias, norm_g):
    B, S, _ = q.shape
    NC = S // CHUNK
    shp = lambda t: t.reshape(B, NC, CHUNK, ATTN_HEADS, HEAD_DIM)
    q, k, v = shp(q), shp(k), shp(v)
    pad = ((0, 0), (N_LEFT_CHUNKS, 0), (0, 0), (0, 0), (0, 0))
    kp, vp = jnp.pad(k, pad), jnp.pad(v, pad)
    band = jnp.arange(NC)[:, None] + jnp.arange(BAND_CHUNKS)[None, :]
    kb = kp[:, band].reshape(B, NC, BAND_CHUNKS * CHUNK, ATTN_HEADS, HEAD_DIM)
    vb = vp[:, band].reshape(B, NC, BAND_CHUNKS * CHUNK, ATTN_HEADS, HEAD_DIM)
    scores = jnp.einsum('bnqhd,bnkhd->bnhqk', q, kb,
                        preferred_element_type=jnp.float32) * (HEAD_DIM ** -0.5)
    qi = jnp.arange(CHUNK)[:, None]
    kj = jnp.arange(BAND_CHUNKS * CHUNK)[None, :]
    rel = qi - kj + N_LEFT_CHUNKS * CHUNK
    idx = jnp.clip(rel, -REL_CLIP, REL_CLIP) + REL_CLIP
    bias = jnp.transpose(rel_bias.astype(jnp.float32)[idx], (2, 0, 1))
    valid = jnp.repeat(band >= N_LEFT_CHUNKS, CHUNK, axis=1)
    scores = jnp.where(valid[None, :, None, None, :], scores + bias[None, None], MASK_VALUE)
    p = jax.nn.softmax(scores, axis=-1)
    o = jnp.einsum('bnhqk,bnkhd->bnqhd', p.astype(vb.dtype), vb)
    o = o.reshape(B, S, ATTN_WIDTH)
    return _rmsnorm(o, norm_g)


def _swiglu(h, wg, wu, wd):
    return (jax.nn.silu(h @ wg) * (h @ wu)) @ wd


def _moe(h, router, wg, wu, wd):
    logits = (h @ router).astype(jnp.float32)
    top_val, top_idx = lax.top_k(logits, TOP_K)
    gates = jax.nn.softmax(top_val, axis=-1)
    combine = jnp.sum(jax.nn.one_hot(top_idx, N_EXPERTS, dtype=jnp.float32) * gates[..., None], axis=-2)
    combine = combine.astype(h.dtype)
    y = jnp.zeros_like(h)
    for e in range(N_EXPERTS):
        y = y + combine[..., e:e + 1] * _swiglu(h, wg[e], wu[e], wd[e])
    return y


def setup_inputs(seed: int = 0) -> dict:
    key = jax.random.key(seed)
    keys = jax.random.split(key, 40)
    counter = [0]

    def nxt():
        kk = keys[counter[0]]
        counter[0] += 1
        return kk

    nrm = lambda shape, scale: scale * jax.random.normal(nxt(), shape, jnp.float32)
    L = DEPTH
    x = nrm((BATCH, SEQ, D_MODEL), 1.0)
    norm_mix_g = 1.0 + nrm((L, D_MODEL), 0.02)
    w_in = nrm((L, D_MODEL, IN_COLS), D_MODEL ** -0.5)
    shift_mu = jax.random.uniform(nxt(), (L, SHIFT_COLS), jnp.float32, 0.2, 0.8)
    ramp = jnp.linspace(0.0, 1.0, RWKV_WIDTH) ** 0.85
    rwkv_w0 = (-6.5 + 5.0 * ramp)[None, :] + nrm((L, RWKV_WIDTH), 0.1)
    rwkv_w2 = nrm((L, DECAY_LORA, RWKV_WIDTH), 0.1 * DECAY_LORA ** -0.5)
    rwkv_a0 = nrm((L, RWKV_WIDTH), 0.2)
    rwkv_a2 = nrm((L, AAA_LORA, RWKV_WIDTH), 0.5 * AAA_LORA ** -0.5)
    rwkv_g2 = nrm((L, GATE_LORA, RWKV_WIDTH), GATE_LORA ** -0.5)
    rwkv_k_k = 0.85 + nrm((L, RWKV_WIDTH), 0.05)
    rwkv_k_a = 1.0 + nrm((L, RWKV_WIDTH), 0.05)
    rwkv_r_k = nrm((L, RWKV_WIDTH), 0.1)
    rwkv_ln_w = 1.0 + nrm((L, RWKV_WIDTH), 0.02)
    rwkv_ln_b = nrm((L, RWKV_WIDTH), 0.02)
    attn_rel_bias = nrm((L, 2 * REL_CLIP + 1, ATTN_HEADS), 0.5)
    attn_norm_g = 1.0 + nrm((L, ATTN_WIDTH), 0.02)
    w_out = nrm((L, D_MODEL, D_MODEL), D_MODEL ** -0.5)
    norm_ffn_g = 1.0 + nrm((L, D_MODEL), 0.02)
    ffn_w_gate = nrm((N_DENSE, D_MODEL, D_FF), D_MODEL ** -0.5)
    ffn_w_up = nrm((N_DENSE, D_MODEL, D_FF), D_MODEL ** -0.5)
    ffn_w_down = nrm((N_DENSE, D_FF, D_MODEL), D_FF ** -0.5)
    moe_router = nrm((N_MOE, D_MODEL, N_EXPERTS), D_MODEL ** -0.5)
    moe_w_gate = nrm((N_MOE, N_EXPERTS, D_MODEL, D_FF_EXPERT), D_MODEL ** -0.5)
    moe_w_up = nrm((N_MOE, N_EXPERTS, D_MODEL, D_FF_EXPERT), D_MODEL ** -0.5)
    moe_w_down = nrm((N_MOE, N_EXPERTS, D_FF_EXPERT, D_MODEL), D_FF_EXPERT ** -0.5)
    norm_final_g = 1.0 + nrm((D_MODEL,), 0.02)
    return {'x': x, 'norm_mix_g': norm_mix_g, 'w_in': w_in, 'shift_mu': shift_mu,
            'rwkv_w0': rwkv_w0, 'rwkv_w2': rwkv_w2, 'rwkv_a0': rwkv_a0, 'rwkv_a2': rwkv_a2,
            'rwkv_g2': rwkv_g2, 'rwkv_k_k': rwkv_k_k, 'rwkv_k_a': rwkv_k_a, 'rwkv_r_k': rwkv_r_k,
            'rwkv_ln_w': rwkv_ln_w, 'rwkv_ln_b': rwkv_ln_b, 'attn_rel_bias': attn_rel_bias,
            'attn_norm_g': attn_norm_g, 'w_out': w_out, 'norm_ffn_g': norm_ffn_g,
            'ffn_w_gate': ffn_w_gate, 'ffn_w_up': ffn_w_up, 'ffn_w_down': ffn_w_down,
            'moe_router': moe_router, 'moe_w_gate': moe_w_gate, 'moe_w_up': moe_w_up,
            'moe_w_down': moe_w_down, 'norm_final_g': norm_final_g}


def reference(x, norm_mix_g, w_in, shift_mu, rwkv_w0, rwkv_w2, rwkv_a0, rwkv_a2, rwkv_g2,
              rwkv_k_k, rwkv_k_a, rwkv_r_k, rwkv_ln_w, rwkv_ln_b, attn_rel_bias, attn_norm_g,
              w_out, norm_ffn_g, ffn_w_gate, ffn_w_up, ffn_w_down, moe_router, moe_w_gate,
              moe_w_up, moe_w_down, norm_final_g):
    for l in range(DEPTH):
        h = _rmsnorm(x, norm_mix_g[l])
        proj = h @ w_in[l]
        r, k, v, wd, ad, gd = _split(_token_shift(proj[..., :SHIFT_COLS], shift_mu[l]), SHIFT_SIZES)
        qa, ka, va = _split(proj[..., SHIFT_COLS:], ATTN_SIZES)
        y_rwkv = _rwkv7(r, k, v, wd, ad, gd, rwkv_w0[l], rwkv_w2[l], rwkv_a0[l], rwkv_a2[l],
                        rwkv_g2[l], rwkv_k_k[l], rwkv_k_a[l], rwkv_r_k[l], rwkv_ln_w[l], rwkv_ln_b[l])
        y_attn = _chunk_attention(qa, ka, va, attn_rel_bias[l], attn_norm_g[l])
        x = x + jnp.concatenate([y_rwkv, y_attn], axis=-1) @ w_out[l]
        h = _rmsnorm(x, norm_ffn_g[l])
        li = l // 2
        if l % 2 == 0:
            x = x + _swiglu(h, ffn_w_gate[li], ffn_w_up[li], ffn_w_down[li])
        else:
            x = x + _moe(h, moe_router[li], moe_w_gate[li], moe_w_up[li], moe_w_down[li])
    return _rmsnorm(x, norm_final_g)
```

```python
import functools

import jax
import jax.numpy as jnp
from jax import lax
from jax.experimental import pallas as pl
from jax.experimental.pallas import tpu as pltpu

F32 = jnp.float32
BF16 = jnp.bfloat16

D_MODEL = 1024
CHUNK = 64
N_LEFT_CHUNKS = 8
HEAD_DIM = 64
RWKV_WIDTH = 512
ATTN_WIDTH = 512
DECAY_LORA = 64
AAA_LORA = 64
GATE_LORA = 128
REL_CLIP = 128
N_EXPERTS = 8
RMS_EPS = 1e-6
GN_EPS = 64e-5
MASK_VALUE = -1e30
SHIFT_COLS = 3 * RWKV_WIDTH + DECAY_LORA + AAA_LORA + GATE_LORA

LANES = 128
PAIR = 2 * CHUNK
N_PAIRS = RWKV_WIDTH // LANES
ATTN_WINDOW = (N_LEFT_CHUNKS + 2) * CHUNK
N_BIAS_TABLES = N_LEFT_CHUNKS + 2
BIAS_BASE = 768
REL_ROWS = 384
VMEM_LIMIT = 56 * 1024 * 1024


def _cparams(sem):
    return pltpu.CompilerParams(dimension_semantics=sem, vmem_limit_bytes=VMEM_LIMIT)


def _mm(a, b):
    return jnp.dot(a.astype(BF16), b.astype(BF16), preferred_element_type=F32)


def _mm_nt(a, b):
    return lax.dot_general(a.astype(BF16), b.astype(BF16), (((1,), (1,)), ((), ())),
                           preferred_element_type=F32)


def _mm_tn(a, b):
    return lax.dot_general(a.astype(BF16), b.astype(BF16), (((0,), (0,)), ((), ())),
                           preferred_element_type=F32)


def _split_terms(x, n):
    terms, rem = [], x
    for _ in range(n):
        hi = rem.astype(BF16)
        terms.append(hi)
        rem = rem - hi.astype(F32)
    return terms


def _dot_exact_rhs(x, w_bf16, n):
    acc = None
    for t in _split_terms(x, n):
        d = jnp.dot(t, w_bf16, preferred_element_type=F32)
        acc = d if acc is None else acc + d
    return acc


def _dot_exact_lhs(w_bf16, x, n):
    acc = None
    for t in _split_terms(x, n):
        d = jnp.dot(w_bf16, t, preferred_element_type=F32)
        acc = d if acc is None else acc + d
    return acc


def _dot_f32(a, b):
    a1, a2, a3 = _split_terms(a, 3)
    b1, b2, b3 = _split_terms(b, 3)
    acc = None
    for x, y in ((a1, b1), (a1, b2), (a2, b1), (a2, b2), (a1, b3), (a3, b1)):
        d = jnp.dot(x, y, preferred_element_type=F32)
        acc = d if acc is None else acc + d
    return acc


def _sigmoid(x):
    return 1.0 / (1.0 + jnp.exp(-x))


def _rms(x, g):
    return x * lax.rsqrt(jnp.mean(x * x, axis=-1, keepdims=True) + RMS_EPS) * g


def _inproj_kernel(x_ref, g_ref, ws_ref, wa_ref, ps_ref, qkv_ref):
    hb = _rms(x_ref[...], g_ref[...]).astype(BF16)
    ps_ref[...] = jnp.dot(hb, ws_ref[...], preferred_element_type=F32)
    qkv_ref[...] = jnp.dot(hb, wa_ref[...], preferred_element_type=F32).astype(BF16)


def _inproj(x2, g, w_shift, w_attn, tm):
    T = x2.shape[0]
    ns, na = w_shift.shape[1], w_attn.shape[1]
    return pl.pallas_call(
        _inproj_kernel,
        grid=(T // tm,),
        in_specs=[pl.BlockSpec((tm, D_MODEL), lambda i: (i, 0)),
                  pl.BlockSpec((1, D_MODEL), lambda i: (0, 0)),
                  pl.BlockSpec((D_MODEL, ns), lambda i: (0, 0)),
                  pl.BlockSpec((D_MODEL, na), lambda i: (0, 0))],
        out_specs=[pl.BlockSpec((tm, ns), lambda i: (i, 0)),
                   pl.BlockSpec((tm, na), lambda i: (i, 0))],
        out_shape=[jax.ShapeDtypeStruct((T, ns), F32),
                   jax.ShapeDtypeStruct((T, na), BF16)],
        compiler_params=_cparams(("parallel",)),
        name="inproj",
    )(x2, g, w_shift, w_attn)


def _rwkv_kernel(ps_ref, prev_ref, mu_ref, vec_ref, wa_ref, g2_ref, bd_ref, y_ref, h_ref):
    c = pl.program_id(1)

    @pl.when(c == 0)
    def _init():
        h_ref[...] = jnp.zeros(h_ref.shape, F32)

    p = ps_ref[0]
    last = jnp.where(c > 0, prev_ref[0][7:8, :], 0.0)
    row = lax.broadcasted_iota(jnp.int32, p.shape, 0)
    prev = jnp.where(row == 0, last, pltpu.roll(p, 1, 0))
    xs = p + mu_ref[...] * (prev - p)

    W = RWKV_WIDTH
    r, k, v = xs[:, 0:W], xs[:, W:2 * W], xs[:, 2 * W:3 * W]
    z0 = xs[:, 3 * W:3 * W + LANES]
    gd = xs[:, 3 * W + LANES:3 * W + 2 * LANES]
    lane = lax.broadcasted_iota(jnp.int32, (CHUNK, LANES), 1)
    m1 = lane < HEAD_DIM
    z0 = jnp.where(m1, jnp.tanh(z0), z0)
    lora = _mm(z0, wa_ref[...])
    vec = vec_ref[...]
    w0, a0, k_k, k_a, r_k, ln_w, ln_b = (vec[i:i + 1] for i in range(7))
    w = w0 + lora[:, :W]
    a = _sigmoid(a0 + lora[:, W:])
    g = _mm(_sigmoid(gd), g2_ref[...])
    softplus_neg_w = jnp.maximum(-w, 0.0) + jnp.log(1.0 + jnp.exp(-jnp.abs(w)))
    lw = -jnp.exp(-softplus_neg_w - 0.5)

    bd = bd_ref[...]
    kk = k * k_k
    kk = kk / jnp.maximum(jnp.sqrt(_dot_exact_rhs(kk * kk, bd, 2)), 1e-12)
    k2 = k * (1.0 + (a - 1.0) * k_a)
    kka = kk * a

    ti = lax.broadcasted_iota(jnp.int32, (CHUNK, CHUNK), 0)
    tj = lax.broadcasted_iota(jnp.int32, (CHUNK, CHUNK), 1)
    tri = jnp.where(ti >= tj, 1.0, 0.0).astype(BF16)
    L = _dot_exact_lhs(tri, lw, 3)
    Lc = L[CHUNK - 1:CHUNK]
    inv = jnp.exp(-L)
    to_end = jnp.exp(Lc - L)
    Rt = r * jnp.exp(L)
    At = -kk * jnp.exp(L - lw)
    Bt = kka * inv
    Kt = k2 * inv
    Bh = kka * to_end
    Kh = k2 * to_end
    gC = jnp.exp(Lc)

    ri = lax.broadcasted_iota(jnp.int32, (PAIR, PAIR), 0)
    ci = lax.broadcasted_iota(jnp.int32, (PAIR, PAIR), 1)
    same_head = (ri >> 6) == (ci >> 6)
    strict = same_head & (ri > ci)
    incl = same_head & (ri >= ci)
    eye = ri == ci
    eye_f = jnp.where(eye, 1.0, 0.0)

    ys = []
    for pi in range(N_PAIRS):
        sl = slice(LANES * pi, LANES * (pi + 1))

        def stack(x):
            xp = x[:, sl]
            return jnp.concatenate([jnp.where(m1, xp, 0.0), jnp.where(m1, 0.0, xp)], axis=0)

        sAt, sRt, sV = stack(At).astype(BF16), stack(Rt), stack(v).astype(BF16)
        sBt, sKt = stack(Bt).astype(BF16), stack(Kt).astype(BF16)
        sBh, sKh = stack(Bh).astype(BF16), stack(Kh).astype(BF16)
        big = _mm_nt(jnp.concatenate([sAt, sRt.astype(BF16)], axis=0),
                     jnp.concatenate([sBt, sKt], axis=0))
        AB = jnp.where(strict, big[:PAIR, :PAIR], 0.0)
        AK = jnp.where(strict, big[:PAIR, PAIR:], 0.0)
        RB = jnp.where(incl, big[PAIR:, :PAIR], 0.0)
        RK = jnp.where(incl, big[PAIR:, PAIR:], 0.0)
        X = eye_f + AB
        Pw = _mm(AB, AB)
        for _ in range(4):
            PX = _mm(Pw, jnp.concatenate([Pw, X], axis=1))
            Pw = PX[:, :PAIR]
            X = X + PX[:, PAIR:]
        Tm = X + _mm(Pw, X)
        W1 = _mm(AK, sV)
        PQ = _mm(Tm, jnp.concatenate([sAt, W1.astype(BF16)], axis=1)).astype(BF16)
        Pm, Q = PQ[:, :PAIR], PQ[:, PAIR:]
        RBPQ = _mm(RB, PQ)
        Rp = sRt + RBPQ[:, :PAIR]
        Y0 = RBPQ[:, PAIR:] + _mm(RK, sV)
        Mm = jnp.where(eye, gC[:, sl], 0.0) + _mm_tn(sBh, Pm)
        G = _mm_tn(jnp.concatenate([sBh, sKh], axis=0), jnp.concatenate([Q, sV], axis=0))
        YH = _mm(jnp.concatenate([Rp, Mm], axis=0), h_ref[pi])
        Ysm = YH[:PAIR] + Y0
        h_ref[pi] = YH[PAIR:] + G
        ys.append(Ysm[:CHUNK] + Ysm[CHUNK:])
    y = jnp.concatenate(ys, axis=1)

    inv_n = 1.0 / HEAD_DIM
    mean = _dot_exact_rhs(y, bd, 2) * inv_n
    d = y - mean
    var = _dot_exact_rhs(d * d, bd, 2) * inv_n
    yn = d * lax.rsqrt(var + GN_EPS) * ln_w + ln_b
    bonus = _dot_exact_rhs(r * k2 * r_k, bd, 2) * v
    y_ref[0] = ((yn + bonus) * g).astype(BF16)


def _rwkv(ps3, mu, vec, wa, g2, bd):
    B, S, _ = ps3.shape
    nc = S // CHUNK
    rows8 = CHUNK // 8
    return pl.pallas_call(
        _rwkv_kernel,
        grid=(B, nc),
        in_specs=[pl.BlockSpec((1, CHUNK, SHIFT_COLS), lambda b, c: (b, c, 0)),
                  pl.BlockSpec((1, 8, SHIFT_COLS), lambda b, c: (b, jnp.maximum(c * rows8 - 1, 0), 0)),
                  pl.BlockSpec((1, SHIFT_COLS), lambda b, c: (0, 0)),
                  pl.BlockSpec((8, RWKV_WIDTH), lambda b, c: (0, 0)),
                  pl.BlockSpec((LANES, 2 * RWKV_WIDTH), lambda b, c: (0, 0)),
                  pl.BlockSpec((GATE_LORA, RWKV_WIDTH), lambda b, c: (0, 0)),
                  pl.BlockSpec((RWKV_WIDTH, RWKV_WIDTH), lambda b, c: (0, 0))],
        out_specs=pl.BlockSpec((1, CHUNK, RWKV_WIDTH), lambda b, c: (b, c, 0)),
        out_shape=jax.ShapeDtypeStruct((B, S, RWKV_WIDTH), BF16),
        scratch_shapes=[pltpu.VMEM((N_PAIRS, PAIR, LANES), F32)],
        compiler_params=_cparams(("parallel", "arbitrary")),
        name="rwkv7",
    )(ps3, ps3, mu, vec, wa, g2, bd)


def _bias_kernel(rbt_ref, o_ref):
    e = pl.program_id(1)
    xi = lax.broadcasted_iota(jnp.int32, (REL_ROWS, BIAS_BASE), 1)
    ji = lax.broadcasted_iota(jnp.int32, (REL_ROWS, BIAS_BASE), 0)
    off = jnp.where(xi < BIAS_BASE - CHUNK, xi, xi - BIAS_BASE)
    idx = jnp.clip(e * CHUNK - off, -REL_CLIP, REL_CLIP) + REL_CLIP
    onehot = jnp.where(idx == ji, 1.0, 0.0).astype(BF16)
    base = _dot_exact_rhs(rbt_ref[0], onehot, 3)
    kj = lax.broadcasted_iota(jnp.int32, (CHUNK, ATTN_WINDOW), 1)
    kc = kj >> 6
    valid = (kc <= e) & (kc >= e - N_LEFT_CHUNKS)
    for h in range(8):
        rows = jnp.broadcast_to(base[h:h + 1, :], (CHUNK, BIAS_BASE))
        toep = pltpu.roll(rows, 0, 1, stride=1, stride_axis=0)
        o_ref[0, 0, h * CHUNK:(h + 1) * CHUNK, :] = jnp.where(valid, toep[:, :ATTN_WINDOW], MASK_VALUE)


def _bias_tables(rbt):
    L = rbt.shape[0]
    return pl.pallas_call(
        _bias_kernel,
        grid=(L, N_BIAS_TABLES),
        in_specs=[pl.BlockSpec((1, 8, REL_ROWS), lambda l, e: (l, 0, 0))],
        out_specs=pl.BlockSpec((1, 1, 8 * CHUNK, ATTN_WINDOW), lambda l, e: (l, e, 0, 0)),
        out_shape=jax.ShapeDtypeStruct((L, N_BIAS_TABLES, 8 * CHUNK, ATTN_WINDOW), F32),
        compiler_params=_cparams(("parallel", "parallel")),
        name="bias_tables",
    )(rbt)


def _attn_kernel(q_ref, k_ref, v_ref, bias_ref, g_ref, o_ref):
    n = pl.program_id(1)
    start = pl.multiple_of(jnp.maximum(n - (N_LEFT_CHUNKS + 1), 0) * CHUNK, CHUNK)
    q = q_ref[0] * jnp.asarray(HEAD_DIM ** -0.5, BF16)
    kw = k_ref[0, pl.ds(start, ATTN_WINDOW), :]
    vw = v_ref[0, pl.ds(start, ATTN_WINDOW), :]
    lane = lax.broadcasted_iota(jnp.int32, (CHUNK, LANES), 1)
    m1 = lane < HEAD_DIM
    zero = jnp.zeros((), BF16)
    outs = []
    for pi in range(ATTN_WIDTH // LANES):
        sl = slice(LANES * pi, LANES * (pi + 1))
        qp = q[:, sl]
        qs = jnp.concatenate([jnp.where(m1, qp, zero), jnp.where(m1, zero, qp)], axis=0)
        s = lax.dot_general(qs, kw[:, sl], (((1,), (1,)), ((), ())), preferred_element_type=F32)
        s = s + bias_ref[0, 0, PAIR * pi:PAIR * (pi + 1), :]
        ex = jnp.exp(s - jnp.max(s, axis=1, keepdims=True))
        den = jnp.sum(ex, axis=1, keepdims=True)
        o = jnp.dot(ex.astype(BF16), vw[:, sl], preferred_element_type=F32) / den
        outs.append(jnp.where(m1, o[:CHUNK], o[CHUNK:]))
    o = jnp.concatenate(outs, axis=1)
    o_ref[0] = _rms(o, g_ref[...]).astype(BF16)


def _attn(qkv3, bias_l, l, g):
    B, S, _ = qkv3.shape
    nc = S // CHUNK
    return pl.pallas_call(
        _attn_kernel,
        grid=(B, nc),
        in_specs=[pl.BlockSpec((1, CHUNK, ATTN_WIDTH), lambda b, n: (b, n, 0)),
                  pl.BlockSpec((1, S, ATTN_WIDTH), lambda b, n: (b, 0, 1)),
                  pl.BlockSpec((1, S, ATTN_WIDTH), lambda b, n: (b, 0, 2)),
                  pl.BlockSpec((1, 1, 8 * CHUNK, ATTN_WINDOW),
                               lambda b, n: (l, jnp.minimum(n, N_BIAS_TABLES - 1), 0, 0)),
                  pl.BlockSpec((1, ATTN_WIDTH), lambda b, n: (0, 0))],
        out_specs=pl.BlockSpec((1, CHUNK, ATTN_WIDTH), lambda b, n: (b, n, 0)),
        out_shape=jax.ShapeDtypeStruct((B, S, ATTN_WIDTH), BF16),
        compiler_params=_cparams(("parallel", "arbitrary")),
        name="chunk_attn",
    )(qkv3, qkv3, qkv3, bias_l, g)


def _outproj_kernel(yr_ref, ya_ref, x_ref, w_ref, g_ref, *rest, with_router):
    if with_router:
        router_ref, xo_ref, h_ref, logit_ref = rest
    else:
        xo_ref, h_ref = rest
    y = jnp.concatenate([yr_ref[...], ya_ref[...]], axis=1)
    xn = x_ref[...] + jnp.dot(y, w_ref[...], preferred_element_type=F32)
    xo_ref[...] = xn
    h = _rms(xn, g_ref[...])
    h_ref[...] = h.astype(BF16)
    if with_router:
        logit_ref[...] = _dot_f32(h, router_ref[...])


def _outproj(yr, ya, x2, w, g, router, tm):
    T = x2.shape[0]
    with_router = router is not None
    in_specs = [pl.BlockSpec((tm, RWKV_WIDTH), lambda i: (i, 0)),
                pl.BlockSpec((tm, ATTN_WIDTH), lambda i: (i, 0)),
                pl.BlockSpec((tm, D_MODEL), lambda i: (i, 0)),
                pl.BlockSpec((D_MODEL, D_MODEL), lambda i: (0, 0)),
                pl.BlockSpec((1, D_MODEL), lambda i: (0, 0))]
    out_specs = [pl.BlockSpec((tm, D_MODEL), lambda i: (i, 0)),
                 pl.BlockSpec((tm, D_MODEL), lambda i: (i, 0))]
    out_shape = [jax.ShapeDtypeStruct((T, D_MODEL), F32), jax.ShapeDtypeStruct((T, D_MODEL), BF16)]
    args = [yr, ya, x2, w, g]
    if with_router:
        in_specs.append(pl.BlockSpec((D_MODEL, LANES), lambda i: (0, 0)))
        out_specs.append(pl.BlockSpec((tm, LANES), lambda i: (i, 0)))
        out_shape.append(jax.ShapeDtypeStruct((T, LANES), F32))
        args.append(router)
    return pl.pallas_call(
        functools.partial(_outproj_kernel, with_router=with_router),
        grid=(T // tm,),
        in_specs=in_specs, out_specs=out_specs, out_shape=out_shape,
        compiler_params=_cparams(("parallel",)),
        name="outproj_router" if with_router else "outproj",
    )(*args)


FF_TILE = 256


def _ffn_kernel(h_ref, x_ref, wg_ref, wu_ref, wd_ref, *rest, final):
    if final:
        gf_ref, o_ref = rest
    else:
        (o_ref,) = rest
    h = h_ref[...]
    acc = None
    for f in range(0, wg_ref.shape[1], FF_TILE):
        gate = jnp.dot(h, wg_ref[:, f:f + FF_TILE], preferred_element_type=F32)
        up = jnp.dot(h, wu_ref[:, f:f + FF_TILE], preferred_element_type=F32)
        act = (gate * _sigmoid(gate) * up).astype(BF16)
        d = jnp.dot(act, wd_ref[f:f + FF_TILE, :], preferred_element_type=F32)
        acc = d if acc is None else acc + d
    xo = x_ref[...] + acc
    o_ref[...] = _rms(xo, gf_ref[...]) if final else xo


def _ffn(h, x2, wg, wu, wd, final_g, tm):
    T = x2.shape[0]
    F = wg.shape[1]
    final = final_g is not None
    in_specs = [pl.BlockSpec((tm, D_MODEL), lambda i: (i, 0)),
                pl.BlockSpec((tm, D_MODEL), lambda i: (i, 0)),
                pl.BlockSpec((D_MODEL, F), lambda i: (0, 0)),
                pl.BlockSpec((D_MODEL, F), lambda i: (0, 0)),
                pl.BlockSpec((F, D_MODEL), lambda i: (0, 0))]
    args = [h, x2, wg, wu, wd]
    if final:
        in_specs.append(pl.BlockSpec((1, D_MODEL), lambda i: (0, 0)))
        args.append(final_g)
    return pl.pallas_call(
        functools.partial(_ffn_kernel, final=final),
        grid=(T // tm,),
        in_specs=in_specs,
        out_specs=pl.BlockSpec((tm, D_MODEL), lambda i: (i, 0)),
        out_shape=jax.ShapeDtypeStruct((T, D_MODEL), F32),
        compiler_params=_cparams(("parallel",)),
        name="ffn_dense",
    )(*args)


def _moe_kernel(h_ref, x_ref, logit_ref, wg_ref, wu_ref, wd_ref, *rest, final):
    if final:
        gf_ref, o_ref, acc_ref, comb_ref = rest
    else:
        o_ref, acc_ref, comb_ref = rest
    e = pl.program_id(1)
    lane = lax.broadcasted_iota(jnp.int32, comb_ref.shape, 1)

    @pl.when(e == 0)
    def _route():
        neg = jnp.asarray(-jnp.inf, F32)
        lg = jnp.where(lane < N_EXPERTS, logit_ref[...], neg)
        top1 = jnp.max(lg, axis=1, keepdims=True)
        idx1 = jnp.min(jnp.where(lg == top1, lane, LANES), axis=1, keepdims=True)
        lg2 = jnp.where(lane == idx1, neg, lg)
        top2 = jnp.max(lg2, axis=1, keepdims=True)
        idx2 = jnp.min(jnp.where(lg2 == top2, lane, LANES), axis=1, keepdims=True)
        ex = jnp.exp(top2 - top1)
        g1 = 1.0 / (1.0 + ex)
        g2 = ex / (1.0 + ex)
        comb_ref[...] = jnp.where(lane == idx1, g1, 0.0) + jnp.where(lane == idx2, g2, 0.0)
        acc_ref[...] = jnp.zeros(acc_ref.shape, F32)

    ce = jnp.sum(jnp.where(lane == e, comb_ref[...], 0.0), axis=1, keepdims=True)
    h = h_ref[...]
    gate = jnp.dot(h, wg_ref[0], preferred_element_type=F32)
    up = jnp.dot(h, wu_ref[0], preferred_element_type=F32)
    act = (ce * (gate * _sigmoid(gate) * up)).astype(BF16)
    acc_ref[...] += jnp.dot(act, wd_ref[0], preferred_element_type=F32)

    @pl.when(e == pl.num_programs(1) - 1)
    def _finish():
        xo = x_ref[...] + acc_ref[...]
        o_ref[...] = _rms(xo, gf_ref[...]) if final else xo


def _moe(h, x2, logits, wg, wu, wd, final_g, tm):
    T = x2.shape[0]
    E, _, F = wg.shape
    final = final_g is not None
    in_specs = [pl.BlockSpec((tm, D_MODEL), lambda i, e: (i, 0)),
                pl.BlockSpec((tm, D_MODEL), lambda i, e: (i, 0)),
                pl.BlockSpec((tm, LANES), lambda i, e: (i, 0)),
                pl.BlockSpec((1, D_MODEL, F), lambda i, e: (e, 0, 0)),
                pl.BlockSpec((1, D_MODEL, F), lambda i, e: (e, 0, 0)),
                pl.BlockSpec((1, F, D_MODEL), lambda i, e: (e, 0, 0))]
    args = [h, x2, logits, wg, wu, wd]
    if final:
        in_specs.append(pl.BlockSpec((1, D_MODEL), lambda i, e: (0, 0)))
        args.append(final_g)
    return pl.pallas_call(
        functools.partial(_moe_kernel, final=final),
        grid=(T // tm, E),
        in_specs=in_specs,
        out_specs=pl.BlockSpec((tm, D_MODEL), lambda i, e: (i, 0)),
        out_shape=jax.ShapeDtypeStruct((T, D_MODEL), F32),
        scratch_shapes=[pltpu.VMEM((tm, D_MODEL), F32), pltpu.VMEM((tm, LANES), F32)],
        compiler_params=_cparams(("parallel", "arbitrary")),
        name="moe_dense",
    )(*args)


def kernel(x, norm_mix_g, w_in, shift_mu, rwkv_w0, rwkv_w2, rwkv_a0, rwkv_a2, rwkv_g2, rwkv_k_k, rwkv_k_a, rwkv_r_k, rwkv_ln_w, rwkv_ln_b, attn_rel_bias, attn_norm_g, w_out, norm_ffn_g, ffn_w_gate, ffn_w_up, ffn_w_down, moe_router, moe_w_gate, moe_w_up, moe_w_down, norm_final_g):
    B, S, D = x.shape
    depth = w_in.shape[0]
    T = B * S
    tm = min(512, T)
    row = lambda t: t.reshape(1, -1).astype(F32)

    hi = jnp.arange(RWKV_WIDTH) // HEAD_DIM
    bd = (hi[:, None] == hi[None, :]).astype(BF16)
    rbt = jnp.pad(jnp.swapaxes(attn_rel_bias, 1, 2).astype(F32),
                  ((0, 0), (0, 0), (0, REL_ROWS - attn_rel_bias.shape[1])))
    bias_tabs = _bias_tables(rbt)

    x2 = x.reshape(T, D).astype(F32)
    for l in range(depth):
        w_shift = w_in[l, :, :SHIFT_COLS].astype(BF16)
        w_attn = w_in[l, :, SHIFT_COLS:].astype(BF16)
        ps, qkv = _inproj(x2, row(norm_mix_g[l]), w_shift, w_attn, tm)

        zeros = jnp.zeros((DECAY_LORA, RWKV_WIDTH), F32)
        wa = jnp.concatenate([jnp.concatenate([rwkv_w2[l], zeros], axis=1),
                              jnp.concatenate([zeros, rwkv_a2[l]], axis=1)], axis=0).astype(BF16)
        vec = jnp.stack([rwkv_w0[l], rwkv_a0[l], rwkv_k_k[l], rwkv_k_a[l], rwkv_r_k[l],
                         rwkv_ln_w[l], rwkv_ln_b[l], jnp.zeros_like(rwkv_w0[l])]).astype(F32)
        y_rwkv = _rwkv(ps.reshape(B, S, SHIFT_COLS), row(shift_mu[l]), vec, wa,
                       rwkv_g2[l].astype(BF16), bd)
        y_attn = _attn(qkv.reshape(B, S, 3 * ATTN_WIDTH), bias_tabs, l, row(attn_norm_g[l]))

        is_moe = l % 2 == 1
        li = l // 2
        router = None
        if is_moe:
            router = jnp.pad(moe_router[li].astype(F32), ((0, 0), (0, LANES - N_EXPERTS)))
        outs = _outproj(y_rwkv.reshape(T, RWKV_WIDTH), y_attn.reshape(T, ATTN_WIDTH), x2,
                        w_out[l].astype(BF16), row(norm_ffn_g[l]), router, tm)
        final_g = row(norm_final_g) if l == depth - 1 else None
        if is_moe:
            x_mid, h, logits = outs
            x2 = _moe(h, x_mid, logits, moe_w_gate[li].astype(BF16), moe_w_up[li].astype(BF16),
                      moe_w_down[li].astype(BF16), final_g, tm)
        else:
            x_mid, h = outs
            x2 = _ffn(h, x_mid, ffn_w_gate[li].astype(BF16), ffn_w_up[li].astype(BF16),
                      ffn_w_down[li].astype(BF16), final_g, tm)
    return x2.reshape(B, S, D).astype(x.dtype)
```

```python
import functools

import jax
import jax.numpy as jnp
from jax import lax
from jax.experimental import pallas as pl
from jax.experimental.pallas import tpu as pltpu

F32 = jnp.float32
BF16 = jnp.bfloat16

D_MODEL = 1024
CHUNK = 64
N_LEFT_CHUNKS = 8
HEAD_DIM = 64
RWKV_WIDTH = 512
ATTN_WIDTH = 512
DECAY_LORA = 64
AAA_LORA = 64
GATE_LORA = 128
REL_CLIP = 128
N_EXPERTS = 8
RMS_EPS = 1e-6
GN_EPS = 64e-5
MASK_VALUE = -1e30
SHIFT_COLS = 3 * RWKV_WIDTH + DECAY_LORA + AAA_LORA + GATE_LORA

LANES = 128
PAIR = 2 * CHUNK
N_PAIRS = RWKV_WIDTH // LANES
ATTN_WINDOW = (N_LEFT_CHUNKS + 2) * CHUNK
N_BIAS_TABLES = N_LEFT_CHUNKS + 2
BIAS_BASE = 768
REL_ROWS = 384
VMEM_LIMIT = 56 * 1024 * 1024


def _cparams(sem):
    return pltpu.CompilerParams(dimension_semantics=sem, vmem_limit_bytes=VMEM_LIMIT)


def _mm(a, b):
    return jnp.dot(a.astype(BF16), b.astype(BF16), preferred_element_type=F32)


def _mm_nt(a, b):
    return lax.dot_general(a.astype(BF16), b.astype(BF16), (((1,), (1,)), ((), ())),
                           preferred_element_type=F32)


def _mm_tn(a, b):
    return lax.dot_general(a.astype(BF16), b.astype(BF16), (((0,), (0,)), ((), ())),
                           preferred_element_type=F32)


def _split_terms(x, n):
    terms, rem = [], x
    for _ in range(n):
        hi = rem.astype(BF16)
        terms.append(hi)
        rem = rem - hi.astype(F32)
    return terms


def _dot_exact_rhs(x, w_bf16, n):
    acc = None
    for t in _split_terms(x, n):
        d = jnp.dot(t, w_bf16, preferred_element_type=F32)
        acc = d if acc is None else acc + d
    return acc


def _dot_exact_lhs(w_bf16, x, n):
    acc = None
    for t in _split_terms(x, n):
        d = jnp.dot(w_bf16, t, preferred_element_type=F32)
        acc = d if acc is None else acc + d
    return acc


def _dot_f32(a, b):
    a1, a2, a3 = _split_terms(a, 3)
    b1, b2, b3 = _split_terms(b, 3)
    acc = None
    for x, y in ((a1, b1), (a1, b2), (a2, b1), (a2, b2), (a1, b3), (a3, b1)):
        d = jnp.dot(x, y, preferred_element_type=F32)
        acc = d if acc is None else acc + d
    return acc


def _sigmoid(x):
    return 1.0 / (1.0 + jnp.exp(-x))


def _rms(x, g):
    return x * lax.rsqrt(jnp.mean(x * x, axis=-1, keepdims=True) + RMS_EPS) * g


def _inproj_kernel(x_ref, g_ref, ws_ref, wa_ref, ps_ref, qkv_ref):
    hb = _rms(x_ref[...], g_ref[...]).astype(BF16)
    ps_ref[...] = jnp.dot(hb, ws_ref[...], preferred_element_type=F32)
    qkv_ref[...] = jnp.dot(hb, wa_ref[...], preferred_element_type=F32).astype(BF16)


def _inproj(x2, g, w_shift, w_attn, tm):
    T = x2.shape[0]
    ns, na = w_shift.shape[1], w_attn.shape[1]
    return pl.pallas_call(
        _inproj_kernel,
        grid=(T // tm,),
        in_specs=[pl.BlockSpec((tm, D_MODEL), lambda i: (i, 0)),
                  pl.BlockSpec((1, D_MODEL), lambda i: (0, 0)),
                  pl.BlockSpec((D_MODEL, ns), lambda i: (0, 0)),
                  pl.BlockSpec((D_MODEL, na), lambda i: (0, 0))],
        out_specs=[pl.BlockSpec((tm, ns), lambda i: (i, 0)),
                   pl.BlockSpec((tm, na), lambda i: (i, 0))],
        out_shape=[jax.ShapeDtypeStruct((T, ns), F32),
                   jax.ShapeDtypeStruct((T, na), BF16)],
        compiler_params=_cparams(("parallel",)),
        name="inproj",
    )(x2, g, w_shift, w_attn)


def _rwkv_kernel(ps_ref, prev_ref, mu_ref, vec_ref, wa_ref, g2_ref, bd_ref, y_ref, h_ref):
    c = pl.program_id(1)

    @pl.when(c == 0)
    def _init():
        h_ref[...] = jnp.zeros(h_ref.shape, F32)

    p = ps_ref[0]
    last = jnp.where(c > 0, prev_ref[0][7:8, :], 0.0)
    row = lax.broadcasted_iota(jnp.int32, p.shape, 0)
    prev = jnp.where(row == 0, last, pltpu.roll(p, 1, 0))
    xs = p + mu_ref[...] * (prev - p)

    W = RWKV_WIDTH
    r, k, v = xs[:, 0:W], xs[:, W:2 * W], xs[:, 2 * W:3 * W]
    z0 = xs[:, 3 * W:3 * W + LANES]
    gd = xs[:, 3 * W + LANES:3 * W + 2 * LANES]
    m1 = lax.broadcasted_iota(jnp.int32, (CHUNK, LANES), 1) < HEAD_DIM
    z0 = jnp.where(lax.broadcasted_iota(jnp.int32, z0.shape, 1) < DECAY_LORA, jnp.tanh(z0), z0)
    lora = _mm(z0, wa_ref[...])
    vec = vec_ref[...]
    w0, a0, k_k, k_a, r_k, ln_w, ln_b = (vec[i:i + 1] for i in range(7))
    w = w0 + lora[:, :W]
    a = _sigmoid(a0 + lora[:, W:])
    g = _mm(_sigmoid(gd), g2_ref[...])
    softplus_neg_w = jnp.maximum(-w, 0.0) + jnp.log(1.0 + jnp.exp(-jnp.abs(w)))
    lw = -jnp.exp(-softplus_neg_w - 0.5)

    bd = bd_ref[...]
    kk = k * k_k
    kk = kk / jnp.maximum(jnp.sqrt(_dot_exact_rhs(kk * kk, bd, 2)), 1e-12)
    k2 = k * (1.0 + (a - 1.0) * k_a)
    kka = kk * a

    rows = p.shape[0]
    n_chunks = rows // CHUNK
    ti = lax.broadcasted_iota(jnp.int32, (rows, rows), 0)
    tj = lax.broadcasted_iota(jnp.int32, (rows, rows), 1)
    tri = jnp.where((ti >= tj) & ((ti >> 6) == (tj >> 6)), 1.0, 0.0).astype(BF16)
    L = _dot_exact_lhs(tri, lw, 3)

    ri = lax.broadcasted_iota(jnp.int32, (PAIR, PAIR), 0)
    ci = lax.broadcasted_iota(jnp.int32, (PAIR, PAIR), 1)
    same_head = (ri >> 6) == (ci >> 6)
    strict = same_head & (ri > ci)
    incl = same_head & (ri >= ci)
    eye = ri == ci
    eye_f = jnp.where(eye, 1.0, 0.0)

    units = []
    for j in range(n_chunks):
        rs = slice(CHUNK * j, CHUNK * (j + 1))
        Lj, lwj = L[rs], lw[rs]
        Lc = Lj[CHUNK - 1:CHUNK]
        inv = jnp.exp(-Lj)
        to_end = jnp.exp(Lc - Lj)
        gC = jnp.exp(Lc)
        Rt = r[rs] * jnp.exp(Lj)
        At = -kk[rs] * jnp.exp(Lj - lwj)
        Bt, Kt = kka[rs] * inv, k2[rs] * inv
        Bh, Kh = kka[rs] * to_end, k2[rs] * to_end
        vj = v[rs]
        for pi in range(N_PAIRS):
            sl = slice(LANES * pi, LANES * (pi + 1))

            def stack(x):
                xp = x[:, sl]
                return jnp.concatenate([jnp.where(m1, xp, 0.0), jnp.where(m1, 0.0, xp)], axis=0)

            units.append(dict(
                j=j, pi=pi, gC=gC[:, sl], sRt=stack(Rt),
                sAt=stack(At).astype(BF16), sV=stack(vj).astype(BF16),
                sBt=stack(Bt).astype(BF16), sKt=stack(Kt).astype(BF16),
                sBh=stack(Bh).astype(BF16), sKh=stack(Kh).astype(BF16)))

    for u in units:
        big = _mm_nt(jnp.concatenate([u["sAt"], u["sRt"].astype(BF16)], axis=0),
                     jnp.concatenate([u["sBt"], u["sKt"]], axis=0))
        u["AB"] = jnp.where(strict, big[:PAIR, :PAIR], 0.0)
        u["AK"] = jnp.where(strict, big[:PAIR, PAIR:], 0.0)
        u["RB"] = jnp.where(incl, big[PAIR:, :PAIR], 0.0)
        u["RK"] = jnp.where(incl, big[PAIR:, PAIR:], 0.0)
    for u in units:
        u["X"] = eye_f + u["AB"]
        u["Pw"] = _mm(u["AB"], u["AB"])
        u["W1"] = _mm(u["AK"], u["sV"])
    for _ in range(4):
        for u in units:
            PX = _mm(u["Pw"], jnp.concatenate([u["Pw"], u["X"]], axis=1))
            u["Pw"] = PX[:, :PAIR]
            u["X"] = u["X"] + PX[:, PAIR:]
    for u in units:
        u["Tm"] = u["X"] + _mm(u["Pw"], u["X"])
    for u in units:
        u["PQ"] = _mm(u["Tm"], jnp.concatenate([u["sAt"], u["W1"].astype(BF16)], axis=1)).astype(BF16)
    for u in units:
        PQ = u["PQ"]
        Pm, Q = PQ[:, :PAIR], PQ[:, PAIR:]
        RBPQ = _mm(u["RB"], PQ)
        u["Rp"] = u["sRt"] + RBPQ[:, :PAIR]
        u["Y0"] = RBPQ[:, PAIR:] + _mm(u["RK"], u["sV"])
        u["Mm"] = jnp.where(eye, u["gC"], 0.0) + _mm_tn(u["sBh"], Pm)
        u["G"] = _mm_tn(jnp.concatenate([u["sBh"], u["sKh"]], axis=0),
                        jnp.concatenate([Q, u["sV"]], axis=0))
    H = [h_ref[pi] for pi in range(N_PAIRS)]
    y_rows = []
    for j in range(n_chunks):
        ys = []
        for u in units[j * N_PAIRS:(j + 1) * N_PAIRS]:
            pi = u["pi"]
            YH = _mm(jnp.concatenate([u["Rp"], u["Mm"]], axis=0), H[pi])
            Ysm = YH[:PAIR] + u["Y0"]
            H[pi] = YH[PAIR:] + u["G"]
            ys.append(Ysm[:CHUNK] + Ysm[CHUNK:])
        y_rows.append(jnp.concatenate(ys, axis=1))
    for pi in range(N_PAIRS):
        h_ref[pi] = H[pi]
    y = y_rows[0] if n_chunks == 1 else jnp.concatenate(y_rows, axis=0)

    inv_n = 1.0 / HEAD_DIM
    mean = _dot_exact_rhs(y, bd, 2) * inv_n
    d = y - mean
    var = _dot_exact_rhs(d * d, bd, 2) * inv_n
    yn = d * lax.rsqrt(var + GN_EPS) * ln_w + ln_b
    bonus = _dot_exact_rhs(r * k2 * r_k, bd, 2) * v
    y_ref[0] = ((yn + bonus) * g).astype(BF16)


RWKV_BLOCK_CHUNKS = 4


def _rwkv(ps3, mu, vec, wa, g2, bd):
    B, S, _ = ps3.shape
    rows = RWKV_BLOCK_CHUNKS * CHUNK
    nc = S // rows
    rows8 = rows // 8
    return pl.pallas_call(
        _rwkv_kernel,
        grid=(B, nc),
        in_specs=[pl.BlockSpec((1, rows, SHIFT_COLS), lambda b, c: (b, c, 0)),
                  pl.BlockSpec((1, 8, SHIFT_COLS), lambda b, c: (b, jnp.maximum(c * rows8 - 1, 0), 0)),
                  pl.BlockSpec((1, SHIFT_COLS), lambda b, c: (0, 0)),
                  pl.BlockSpec((8, RWKV_WIDTH), lambda b, c: (0, 0)),
                  pl.BlockSpec((LANES, 2 * RWKV_WIDTH), lambda b, c: (0, 0)),
                  pl.BlockSpec((GATE_LORA, RWKV_WIDTH), lambda b, c: (0, 0)),
                  pl.BlockSpec((RWKV_WIDTH, RWKV_WIDTH), lambda b, c: (0, 0))],
        out_specs=pl.BlockSpec((1, rows, RWKV_WIDTH), lambda b, c: (b, c, 0)),
        out_shape=jax.ShapeDtypeStruct((B, S, RWKV_WIDTH), BF16),
        scratch_shapes=[pltpu.VMEM((N_PAIRS, PAIR, LANES), F32)],
        compiler_params=_cparams(("parallel", "arbitrary")),
        name="rwkv7",
    )(ps3, ps3, mu, vec, wa, g2, bd)


def _bias_kernel(rbt_ref, o_ref):
    e = pl.program_id(1)
    xi = lax.broadcasted_iota(jnp.int32, (REL_ROWS, BIAS_BASE), 1)
    ji = lax.broadcasted_iota(jnp.int32, (REL_ROWS, BIAS_BASE), 0)
    off = jnp.where(xi < BIAS_BASE - CHUNK, xi, xi - BIAS_BASE)
    idx = jnp.clip(e * CHUNK - off, -REL_CLIP, REL_CLIP) + REL_CLIP
    onehot = jnp.where(idx == ji, 1.0, 0.0).astype(BF16)
    base = _dot_exact_rhs(rbt_ref[0], onehot, 3)
    kj = lax.broadcasted_iota(jnp.int32, (CHUNK, ATTN_WINDOW), 1)
    kc = kj >> 6
    valid = (kc <= e) & (kc >= e - N_LEFT_CHUNKS)
    for h in range(8):
        rows = jnp.broadcast_to(base[h:h + 1, :], (CHUNK, BIAS_BASE))
        toep = pltpu.roll(rows, 0, 1, stride=1, stride_axis=0)
        o_ref[0, 0, h * CHUNK:(h + 1) * CHUNK, :] = jnp.where(valid, toep[:, :ATTN_WINDOW], MASK_VALUE)


def _bias_tables(rbt):
    L = rbt.shape[0]
    return pl.pallas_call(
        _bias_kernel,
        grid=(L, N_BIAS_TABLES),
        in_specs=[pl.BlockSpec((1, 8, REL_ROWS), lambda l, e: (l, 0, 0))],
        out_specs=pl.BlockSpec((1, 1, 8 * CHUNK, ATTN_WINDOW), lambda l, e: (l, e, 0, 0)),
        out_shape=jax.ShapeDtypeStruct((L, N_BIAS_TABLES, 8 * CHUNK, ATTN_WINDOW), F32),
        compiler_params=_cparams(("parallel", "parallel")),
        name="bias_tables",
    )(rbt)


ATTN_BLOCK_CHUNKS = 4


def _attn_kernel(q_ref, k_ref, v_ref, *rest):
    bias_refs, g_ref, o_ref = rest[:ATTN_BLOCK_CHUNKS], rest[-2], rest[-1]
    n0 = pl.program_id(1) * ATTN_BLOCK_CHUNKS
    q = q_ref[0] * jnp.asarray(HEAD_DIM ** -0.5, BF16)
    m1 = lax.broadcasted_iota(jnp.int32, (CHUNK, LANES), 1) < HEAD_DIM
    zero = jnp.zeros((), BF16)
    n_pairs = ATTN_WIDTH // LANES
    units = []
    for j in range(ATTN_BLOCK_CHUNKS):
        start = pl.multiple_of(jnp.maximum(n0 + j - (N_LEFT_CHUNKS + 1), 0) * CHUNK, CHUNK)
        kw = k_ref[0, pl.ds(start, ATTN_WINDOW), :]
        vw = v_ref[0, pl.ds(start, ATTN_WINDOW), :]
        qj = q[CHUNK * j:CHUNK * (j + 1)]
        for pi in range(n_pairs):
            sl = slice(LANES * pi, LANES * (pi + 1))
            qp = qj[:, sl]
            qs = jnp.concatenate([jnp.where(m1, qp, zero), jnp.where(m1, zero, qp)], axis=0)
            units.append(dict(qs=qs, k=kw[:, sl], v=vw[:, sl],
                              bias=bias_refs[j][0, 0, PAIR * pi:PAIR * (pi + 1), :]))
    for u in units:
        u["s"] = lax.dot_general(u["qs"], u["k"], (((1,), (1,)), ((), ())),
                                 preferred_element_type=F32) + u["bias"]
    for u in units:
        s = u["s"]
        ex = jnp.exp(s - jnp.max(s, axis=1, keepdims=True))
        u["den"] = jnp.sum(ex, axis=1, keepdims=True)
        u["ex"] = ex.astype(BF16)
    for u in units:
        o = jnp.dot(u["ex"], u["v"], preferred_element_type=F32) / u["den"]
        u["o"] = jnp.where(m1, o[:CHUNK], o[CHUNK:])
    rows = [jnp.concatenate([u["o"] for u in units[j * n_pairs:(j + 1) * n_pairs]], axis=1)
            for j in range(ATTN_BLOCK_CHUNKS)]
    o = jnp.concatenate(rows, axis=0)
    o_ref[0] = _rms(o, g_ref[...]).astype(BF16)


def _attn(qkv3, bias_l, l, g):
    B, S, _ = qkv3.shape
    rows = ATTN_BLOCK_CHUNKS * CHUNK
    nc = S // rows

    def bias_spec(j):
        return pl.BlockSpec(
            (1, 1, 8 * CHUNK, ATTN_WINDOW),
            lambda b, n: (l, jnp.minimum(n * ATTN_BLOCK_CHUNKS + j, N_BIAS_TABLES - 1), 0, 0))

    return pl.pallas_call(
        _attn_kernel,
        grid=(B, nc),
        in_specs=[pl.BlockSpec((1, rows, ATTN_WIDTH), lambda b, n: (b, n, 0)),
                  pl.BlockSpec((1, S, ATTN_WIDTH), lambda b, n: (b, 0, 1)),
                  pl.BlockSpec((1, S, ATTN_WIDTH), lambda b, n: (b, 0, 2))]
                 + [bias_spec(j) for j in range(ATTN_BLOCK_CHUNKS)]
                 + [pl.BlockSpec((1, ATTN_WIDTH), lambda b, n: (0, 0))],
        out_specs=pl.BlockSpec((1, rows, ATTN_WIDTH), lambda b, n: (b, n, 0)),
        out_shape=jax.ShapeDtypeStruct((B, S, ATTN_WIDTH), BF16),
        compiler_params=_cparams(("parallel", "arbitrary")),
        name="chunk_attn",
    )(qkv3, qkv3, qkv3, *([bias_l] * ATTN_BLOCK_CHUNKS), g)


def _outproj_kernel(yr_ref, ya_ref, x_ref, w_ref, g_ref, *rest, with_router):
    if with_router:
        router_ref, xo_ref, h_ref, logit_ref = rest
    else:
        xo_ref, h_ref = rest
    y = jnp.concatenate([yr_ref[...], ya_ref[...]], axis=1)
    xn = x_ref[...] + jnp.dot(y, w_ref[...], preferred_element_type=F32)
    xo_ref[...] = xn
    h = _rms(xn, g_ref[...])
    h_ref[...] = h.astype(BF16)
    if with_router:
        logit_ref[...] = _dot_f32(h, router_ref[...])


def _outproj(yr, ya, x2, w, g, router, tm):
    T = x2.shape[0]
    with_router = router is not None
    in_specs = [pl.BlockSpec((tm, RWKV_WIDTH), lambda i: (i, 0)),
                pl.BlockSpec((tm, ATTN_WIDTH), lambda i: (i, 0)),
                pl.BlockSpec((tm, D_MODEL), lambda i: (i, 0)),
                pl.BlockSpec((D_MODEL, D_MODEL), lambda i: (0, 0)),
                pl.BlockSpec((1, D_MODEL), lambda i: (0, 0))]
    out_specs = [pl.BlockSpec((tm, D_MODEL), lambda i: (i, 0)),
                 pl.BlockSpec((tm, D_MODEL), lambda i: (i, 0))]
    out_shape = [jax.ShapeDtypeStruct((T, D_MODEL), F32), jax.ShapeDtypeStruct((T, D_MODEL), BF16)]
    args = [yr, ya, x2, w, g]
    if with_router:
        in_specs.append(pl.BlockSpec((D_MODEL, LANES), lambda i: (0, 0)))
        out_specs.append(pl.BlockSpec((tm, LANES), lambda i: (i, 0)))
        out_shape.append(jax.ShapeDtypeStruct((T, LANES), F32))
        args.append(router)
    return pl.pallas_call(
        functools.partial(_outproj_kernel, with_router=with_router),
        grid=(T // tm,),
        in_specs=in_specs, out_specs=out_specs, out_shape=out_shape,
        compiler_params=_cparams(("parallel",)),
        name="outproj_router" if with_router else "outproj",
    )(*args)


FF_TILE = 256


def _ffn_kernel(h_ref, x_ref, wg_ref, wu_ref, wd_ref, *rest, final):
    if final:
        gf_ref, o_ref = rest
    else:
        (o_ref,) = rest
    h = h_ref[...]
    acc = None
    for f in range(0, wg_ref.shape[1], FF_TILE):
        gate = jnp.dot(h, wg_ref[:, f:f + FF_TILE], preferred_element_type=F32)
        up = jnp.dot(h, wu_ref[:, f:f + FF_TILE], preferred_element_type=F32)
        act = (gate * _sigmoid(gate) * up).astype(BF16)
        d = jnp.dot(act, wd_ref[f:f + FF_TILE, :], preferred_element_type=F32)
        acc = d if acc is None else acc + d
    xo = x_ref[...] + acc
    o_ref[...] = _rms(xo, gf_ref[...]) if final else xo


def _ffn(h, x2, wg, wu, wd, final_g, tm):
    T = x2.shape[0]
    F = wg.shape[1]
    final = final_g is not None
    in_specs = [pl.BlockSpec((tm, D_MODEL), lambda i: (i, 0)),
                pl.BlockSpec((tm, D_MODEL), lambda i: (i, 0)),
                pl.BlockSpec((D_MODEL, F), lambda i: (0, 0)),
                pl.BlockSpec((D_MODEL, F), lambda i: (0, 0)),
                pl.BlockSpec((F, D_MODEL), lambda i: (0, 0))]
    args = [h, x2, wg, wu, wd]
    if final:
        in_specs.append(pl.BlockSpec((1, D_MODEL), lambda i: (0, 0)))
        args.append(final_g)
    return pl.pallas_call(
        functools.partial(_ffn_kernel, final=final),
        grid=(T // tm,),
        in_specs=in_specs,
        out_specs=pl.BlockSpec((tm, D_MODEL), lambda i: (i, 0)),
        out_shape=jax.ShapeDtypeStruct((T, D_MODEL), F32),
        compiler_params=_cparams(("parallel",)),
        name="ffn_dense",
    )(*args)


def _moe_kernel(h_ref, x_ref, logit_ref, wg_ref, wu_ref, wd_ref, *rest, final):
    if final:
        gf_ref, o_ref, acc_ref, comb_ref = rest
    else:
        o_ref, acc_ref, comb_ref = rest
    e = pl.program_id(1)
    lane = lax.broadcasted_iota(jnp.int32, comb_ref.shape, 1)

    @pl.when(e == 0)
    def _route():
        neg = jnp.asarray(-jnp.inf, F32)
        lg = jnp.where(lane < N_EXPERTS, logit_ref[...], neg)
        top1 = jnp.max(lg, axis=1, keepdims=True)
        idx1 = jnp.min(jnp.where(lg == top1, lane, LANES), axis=1, keepdims=True)
        lg2 = jnp.where(lane == idx1, neg, lg)
        top2 = jnp.max(lg2, axis=1, keepdims=True)
        idx2 = jnp.min(jnp.where(lg2 == top2, lane, LANES), axis=1, keepdims=True)
        ex = jnp.exp(top2 - top1)
        g1 = 1.0 / (1.0 + ex)
        g2 = ex / (1.0 + ex)
        comb_ref[...] = jnp.where(lane == idx1, g1, 0.0) + jnp.where(lane == idx2, g2, 0.0)
        acc_ref[...] = jnp.zeros(acc_ref.shape, F32)

    ce = jnp.sum(jnp.where(lane == e, comb_ref[...], 0.0), axis=1, keepdims=True)
    h = h_ref[...]
    gate = jnp.dot(h, wg_ref[0], preferred_element_type=F32)
    up = jnp.dot(h, wu_ref[0], preferred_element_type=F32)
    act = (ce * (gate * _sigmoid(gate) * up)).astype(BF16)
    acc_ref[...] += jnp.dot(act, wd_ref[0], preferred_element_type=F32)

    @pl.when(e == pl.num_programs(1) - 1)
    def _finish():
        xo = x_ref[...] + acc_ref[...]
        o_ref[...] = _rms(xo, gf_ref[...]) if final else xo


def _moe(h, x2, logits, wg, wu, wd, final_g, tm):
    T = x2.shape[0]
    E, _, F = wg.shape
    final = final_g is not None
    in_specs = [pl.BlockSpec((tm, D_MODEL), lambda i, e: (i, 0)),
                pl.BlockSpec((tm, D_MODEL), lambda i, e: (i, 0)),
                pl.BlockSpec((tm, LANES), lambda i, e: (i, 0)),
                pl.BlockSpec((1, D_MODEL, F), lambda i, e: (e, 0, 0)),
                pl.BlockSpec((1, D_MODEL, F), lambda i, e: (e, 0, 0)),
                pl.BlockSpec((1, F, D_MODEL), lambda i, e: (e, 0, 0))]
    args = [h, x2, logits, wg, wu, wd]
    if final:
        in_specs.append(pl.BlockSpec((1, D_MODEL), lambda i, e: (0, 0)))
        args.append(final_g)
    return pl.pallas_call(
        functools.partial(_moe_kernel, final=final),
        grid=(T // tm, E),
        in_specs=in_specs,
        out_specs=pl.BlockSpec((tm, D_MODEL), lambda i, e: (i, 0)),
        out_shape=jax.ShapeDtypeStruct((T, D_MODEL), F32),
        scratch_shapes=[pltpu.VMEM((tm, D_MODEL), F32), pltpu.VMEM((tm, LANES), F32)],
        compiler_params=_cparams(("parallel", "arbitrary")),
        name="moe_dense",
    )(*args)


def kernel(x, norm_mix_g, w_in, shift_mu, rwkv_w0, rwkv_w2, rwkv_a0, rwkv_a2, rwkv_g2, rwkv_k_k, rwkv_k_a, rwkv_r_k, rwkv_ln_w, rwkv_ln_b, attn_rel_bias, attn_norm_g, w_out, norm_ffn_g, ffn_w_gate, ffn_w_up, ffn_w_down, moe_router, moe_w_gate, moe_w_up, moe_w_down, norm_final_g):
    B, S, D = x.shape
    depth = w_in.shape[0]
    T = B * S
    tm = min(512, T)
    row = lambda t: t.reshape(1, -1).astype(F32)

    hi = jnp.arange(RWKV_WIDTH) // HEAD_DIM
    bd = (hi[:, None] == hi[None, :]).astype(BF16)
    rbt = jnp.pad(jnp.swapaxes(attn_rel_bias, 1, 2).astype(F32),
                  ((0, 0), (0, 0), (0, REL_ROWS - attn_rel_bias.shape[1])))
    bias_tabs = _bias_tables(rbt)

    x2 = x.reshape(T, D).astype(F32)
    for l in range(depth):
        w_shift = w_in[l, :, :SHIFT_COLS].astype(BF16)
        w_attn = w_in[l, :, SHIFT_COLS:].astype(BF16)
        ps, qkv = _inproj(x2, row(norm_mix_g[l]), w_shift, w_attn, tm)

        zeros = jnp.zeros((DECAY_LORA, RWKV_WIDTH), F32)
        wa = jnp.concatenate([jnp.concatenate([rwkv_w2[l], zeros], axis=1),
                              jnp.concatenate([zeros, rwkv_a2[l]], axis=1)], axis=0).astype(BF16)
        vec = jnp.stack([rwkv_w0[l], rwkv_a0[l], rwkv_k_k[l], rwkv_k_a[l], rwkv_r_k[l],
                         rwkv_ln_w[l], rwkv_ln_b[l], jnp.zeros_like(rwkv_w0[l])]).astype(F32)
        y_rwkv = _rwkv(ps.reshape(B, S, SHIFT_COLS), row(shift_mu[l]), vec, wa,
                       rwkv_g2[l].astype(BF16), bd)
        y_attn = _attn(qkv.reshape(B, S, 3 * ATTN_WIDTH), bias_tabs, l, row(attn_norm_g[l]))

        is_moe = l % 2 == 1
        li = l // 2
        router = None
        if is_moe:
            router = jnp.pad(moe_router[li].astype(F32), ((0, 0), (0, LANES - N_EXPERTS)))
        outs = _outproj(y_rwkv.reshape(T, RWKV_WIDTH), y_attn.reshape(T, ATTN_WIDTH), x2,
                        w_out[l].astype(BF16), row(norm_ffn_g[l]), router, tm)
        final_g = row(norm_final_g) if l == depth - 1 else None
        if is_moe:
            x_mid, h, logits = outs
            x2 = _moe(h, x_mid, logits, moe_w_gate[li].astype(BF16), moe_w_up[li].astype(BF16),
                      moe_w_down[li].astype(BF16), final_g, tm)
        else:
            x_mid, h = outs
            x2 = _ffn(h, x_mid, ffn_w_gate[li].astype(BF16), ffn_w_up[li].astype(BF16),
                      ffn_w_down[li].astype(BF16), final_g, tm)
    return x2.reshape(B, S, D).astype(x.dtype)
```

```python
import functools

import jax
import jax.numpy as jnp
from jax import lax
from jax.experimental import pallas as pl
from jax.experimental.pallas import tpu as pltpu

F32 = jnp.float32
BF16 = jnp.bfloat16

D_MODEL = 1024
CHUNK = 64
N_LEFT_CHUNKS = 8
HEAD_DIM = 64
RWKV_WIDTH = 512
ATTN_WIDTH = 512
DECAY_LORA = 64
AAA_LORA = 64
GATE_LORA = 128
REL_CLIP = 128
N_EXPERTS = 8
RMS_EPS = 1e-6
GN_EPS = 64e-5
MASK_VALUE = -1e30
SHIFT_COLS = 3 * RWKV_WIDTH + DECAY_LORA + AAA_LORA + GATE_LORA

LANES = 128
PAIR = 2 * CHUNK
N_PAIRS = RWKV_WIDTH // LANES
ATTN_WINDOW = (N_LEFT_CHUNKS + 2) * CHUNK
N_BIAS_TABLES = N_LEFT_CHUNKS + 2
BIAS_BASE = 768
REL_ROWS = 384
VMEM_LIMIT = 56 * 1024 * 1024


def _cparams(sem):
    return pltpu.CompilerParams(dimension_semantics=sem, vmem_limit_bytes=VMEM_LIMIT)


def _mm(a, b):
    return jnp.dot(a.astype(BF16), b.astype(BF16), preferred_element_type=F32)


def _mm_nt(a, b):
    return lax.dot_general(a.astype(BF16), b.astype(BF16), (((1,), (1,)), ((), ())),
                           preferred_element_type=F32)


def _mm_tn(a, b):
    return lax.dot_general(a.astype(BF16), b.astype(BF16), (((0,), (0,)), ((), ())),
                           preferred_element_type=F32)


def _split_terms(x, n):
    terms, rem = [], x
    for _ in range(n):
        hi = rem.astype(BF16)
        terms.append(hi)
        rem = rem - hi.astype(F32)
    return terms


def _dot_exact_rhs(x, w_bf16, n):
    acc = None
    for t in _split_terms(x, n):
        d = jnp.dot(t, w_bf16, preferred_element_type=F32)
        acc = d if acc is None else acc + d
    return acc


def _dot_exact_lhs(w_bf16, x, n):
    acc = None
    for t in _split_terms(x, n):
        d = jnp.dot(w_bf16, t, preferred_element_type=F32)
        acc = d if acc is None else acc + d
    return acc


def _dot_f32(a, b):
    a1, a2, a3 = _split_terms(a, 3)
    b1, b2, b3 = _split_terms(b, 3)
    acc = None
    for x, y in ((a1, b1), (a1, b2), (a2, b1), (a2, b2), (a1, b3), (a3, b1)):
        d = jnp.dot(x, y, preferred_element_type=F32)
        acc = d if acc is None else acc + d
    return acc


def _sigmoid(x):
    return 1.0 / (1.0 + jnp.exp(-x))


def _rms(x, g):
    return x * lax.rsqrt(jnp.mean(x * x, axis=-1, keepdims=True) + RMS_EPS) * g


def _inproj_kernel(x_ref, g_ref, ws_ref, wa_ref, ps_ref, qkv_ref):
    hb = _rms(x_ref[...], g_ref[...]).astype(BF16)
    ps_ref[...] = jnp.dot(hb, ws_ref[...], preferred_element_type=F32)
    qkv_ref[...] = jnp.dot(hb, wa_ref[...], preferred_element_type=F32).astype(BF16)


def _inproj(x2, g, w_shift, w_attn, tm):
    T = x2.shape[0]
    ns, na = w_shift.shape[1], w_attn.shape[1]
    return pl.pallas_call(
        _inproj_kernel,
        grid=(T // tm,),
        in_specs=[pl.BlockSpec((tm, D_MODEL), lambda i: (i, 0)),
                  pl.BlockSpec((1, D_MODEL), lambda i: (0, 0)),
                  pl.BlockSpec((D_MODEL, ns), lambda i: (0, 0)),
                  pl.BlockSpec((D_MODEL, na), lambda i: (0, 0))],
        out_specs=[pl.BlockSpec((tm, ns), lambda i: (i, 0)),
                   pl.BlockSpec((tm, na), lambda i: (i, 0))],
        out_shape=[jax.ShapeDtypeStruct((T, ns), F32),
                   jax.ShapeDtypeStruct((T, na), BF16)],
        compiler_params=_cparams(("parallel",)),
        name="inproj",
    )(x2, g, w_shift, w_attn)


def _rwkv_kernel(ps_ref, prev_ref, mu_ref, vec_ref, wa_ref, g2_ref, bd_ref, y_ref, h_ref):
    c = pl.program_id(1)

    @pl.when(c == 0)
    def _init():
        h_ref[...] = jnp.zeros(h_ref.shape, F32)

    p = ps_ref[0]
    last = jnp.where(c > 0, prev_ref[0][7:8, :], 0.0)
    row = lax.broadcasted_iota(jnp.int32, p.shape, 0)
    prev = jnp.where(row == 0, last, pltpu.roll(p, 1, 0))
    xs = p + mu_ref[...] * (prev - p)

    W = RWKV_WIDTH
    r, k, v = xs[:, 0:W], xs[:, W:2 * W], xs[:, 2 * W:3 * W]
    z0 = xs[:, 3 * W:3 * W + LANES]
    gd = xs[:, 3 * W + LANES:3 * W + 2 * LANES]
    m1 = lax.broadcasted_iota(jnp.int32, (CHUNK, LANES), 1) < HEAD_DIM
    z0 = jnp.where(lax.broadcasted_iota(jnp.int32, z0.shape, 1) < DECAY_LORA, jnp.tanh(z0), z0)
    lora = _mm(z0, wa_ref[...])
    vec = vec_ref[...]
    w0, a0, k_k, k_a, r_k, ln_w, ln_b = (vec[i:i + 1] for i in range(7))
    w = w0 + lora[:, :W]
    a = _sigmoid(a0 + lora[:, W:])
    g = _mm(_sigmoid(gd), g2_ref[...])
    softplus_neg_w = jnp.maximum(-w, 0.0) + jnp.log(1.0 + jnp.exp(-jnp.abs(w)))
    lw = -jnp.exp(-softplus_neg_w - 0.5)

    bd = bd_ref[...]
    kk = k * k_k
    kk = kk / jnp.maximum(jnp.sqrt(_dot_exact_rhs(kk * kk, bd, 2)), 1e-12)
    k2 = k * (1.0 + (a - 1.0) * k_a)
    kka = kk * a

    rows = p.shape[0]
    n_chunks = rows // CHUNK
    ti = lax.broadcasted_iota(jnp.int32, (rows, rows), 0)
    tj = lax.broadcasted_iota(jnp.int32, (rows, rows), 1)
    tri = jnp.where((ti >= tj) & ((ti >> 6) == (tj >> 6)), 1.0, 0.0).astype(BF16)
    L = _dot_exact_lhs(tri, lw, 3)

    ri = lax.broadcasted_iota(jnp.int32, (PAIR, PAIR), 0)
    ci = lax.broadcasted_iota(jnp.int32, (PAIR, PAIR), 1)
    same_head = (ri >> 6) == (ci >> 6)
    strict = same_head & (ri > ci)
    incl = same_head & (ri >= ci)
    eye = ri == ci
    eye_f = jnp.where(eye, 1.0, 0.0)

    units = []
    for j in range(n_chunks):
        rs = slice(CHUNK * j, CHUNK * (j + 1))
        Lj, lwj = L[rs], lw[rs]
        Lc = Lj[CHUNK - 1:CHUNK]
        inv = jnp.exp(-Lj)
        to_end = jnp.exp(Lc - Lj)
        gC = jnp.exp(Lc)
        Rt = r[rs] * jnp.exp(Lj)
        At = -kk[rs] * jnp.exp(Lj - lwj)
        Bt, Kt = kka[rs] * inv, k2[rs] * inv
        Bh, Kh = kka[rs] * to_end, k2[rs] * to_end
        vj = v[rs]
        for pi in range(N_PAIRS):
            sl = slice(LANES * pi, LANES * (pi + 1))

            def stack(x):
                xp = x[:, sl]
                return jnp.concatenate([jnp.where(m1, xp, 0.0), jnp.where(m1, 0.0, xp)], axis=0)

            units.append(dict(
                j=j, pi=pi, gC=gC[:, sl], sRt=stack(Rt),
                sAt=stack(At).astype(BF16), sV=stack(vj).astype(BF16),
                sBt=stack(Bt).astype(BF16), sKt=stack(Kt).astype(BF16),
                sBh=stack(Bh).astype(BF16), sKh=stack(Kh).astype(BF16)))

    for u in units:
        big = _mm_nt(jnp.concatenate([u["sAt"], u["sRt"].astype(BF16)], axis=0),
                     jnp.concatenate([u["sBt"], u["sKt"]], axis=0))
        u["AB"] = jnp.where(strict, big[:PAIR, :PAIR], 0.0)
        u["AK"] = jnp.where(strict, big[:PAIR, PAIR:], 0.0)
        u["RB"] = jnp.where(incl, big[PAIR:, :PAIR], 0.0)
        u["RK"] = jnp.where(incl, big[PAIR:, PAIR:], 0.0)
    for u in units:
        u["X"] = eye_f + u["AB"]
        u["Pw"] = _mm(u["AB"], u["AB"])
        u["W1"] = _mm(u["AK"], u["sV"])
    for _ in range(4):
        for u in units:
            PX = _mm(u["Pw"], jnp.concatenate([u["Pw"], u["X"]], axis=1))
            u["Pw"] = PX[:, :PAIR]
            u["X"] = u["X"] + PX[:, PAIR:]
    for u in units:
        u["Tm"] = u["X"] + _mm(u["Pw"], u["X"])
    for u in units:
        u["PQ"] = _mm(u["Tm"], jnp.concatenate([u["sAt"], u["W1"].astype(BF16)], axis=1)).astype(BF16)
    for u in units:
        PQ = u["PQ"]
        Pm, Q = PQ[:, :PAIR], PQ[:, PAIR:]
        RBPQ = _mm(u["RB"], PQ)
        u["Rp"] = u["sRt"] + RBPQ[:, :PAIR]
        u["Y0"] = RBPQ[:, PAIR:] + _mm(u["RK"], u["sV"])
        u["Mm"] = jnp.where(eye, u["gC"], 0.0) + _mm_tn(u["sBh"], Pm)
        u["G"] = _mm_tn(jnp.concatenate([u["sBh"], u["sKh"]], axis=0),
                        jnp.concatenate([Q, u["sV"]], axis=0))
    H = [h_ref[pi] for pi in range(N_PAIRS)]
    y_rows = []
    for j in range(n_chunks):
        ys = []
        for u in units[j * N_PAIRS:(j + 1) * N_PAIRS]:
            pi = u["pi"]
            YH = _mm(jnp.concatenate([u["Rp"], u["Mm"]], axis=0), H[pi])
            Ysm = YH[:PAIR] + u["Y0"]
            H[pi] = YH[PAIR:] + u["G"]
            ys.append(Ysm[:CHUNK] + Ysm[CHUNK:])
        y_rows.append(jnp.concatenate(ys, axis=1))
    for pi in range(N_PAIRS):
        h_ref[pi] = H[pi]
    y = y_rows[0] if n_chunks == 1 else jnp.concatenate(y_rows, axis=0)

    inv_n = 1.0 / HEAD_DIM
    mean = _dot_exact_rhs(y, bd, 2) * inv_n
    d = y - mean
    var = _dot_exact_rhs(d * d, bd, 2) * inv_n
    yn = d * lax.rsqrt(var + GN_EPS) * ln_w + ln_b
    bonus = _dot_exact_rhs(r * k2 * r_k, bd, 2) * v
    y_ref[0] = ((yn + bonus) * g).astype(BF16)


RWKV_BLOCK_CHUNKS = 4


def _rwkv(ps3, mu, vec, wa, g2, bd):
    B, S, _ = ps3.shape
    rows = RWKV_BLOCK_CHUNKS * CHUNK
    nc = S // rows
    rows8 = rows // 8
    return pl.pallas_call(
        _rwkv_kernel,
        grid=(B, nc),
        in_specs=[pl.BlockSpec((1, rows, SHIFT_COLS), lambda b, c: (b, c, 0)),
                  pl.BlockSpec((1, 8, SHIFT_COLS), lambda b, c: (b, jnp.maximum(c * rows8 - 1, 0), 0)),
                  pl.BlockSpec((1, SHIFT_COLS), lambda b, c: (0, 0)),
                  pl.BlockSpec((8, RWKV_WIDTH), lambda b, c: (0, 0)),
                  pl.BlockSpec((LANES, 2 * RWKV_WIDTH), lambda b, c: (0, 0)),
                  pl.BlockSpec((GATE_LORA, RWKV_WIDTH), lambda b, c: (0, 0)),
                  pl.BlockSpec((RWKV_WIDTH, RWKV_WIDTH), lambda b, c: (0, 0))],
        out_specs=pl.BlockSpec((1, rows, RWKV_WIDTH), lambda b, c: (b, c, 0)),
        out_shape=jax.ShapeDtypeStruct((B, S, RWKV_WIDTH), BF16),
        scratch_shapes=[pltpu.VMEM((N_PAIRS, PAIR, LANES), F32)],
        compiler_params=_cparams(("parallel", "arbitrary")),
        name="rwkv7",
    )(ps3, ps3, mu, vec, wa, g2, bd)


def _bias_kernel(rbt_ref, o_ref):
    e = pl.program_id(1)
    xi = lax.broadcasted_iota(jnp.int32, (REL_ROWS, BIAS_BASE), 1)
    ji = lax.broadcasted_iota(jnp.int32, (REL_ROWS, BIAS_BASE), 0)
    off = jnp.where(xi < BIAS_BASE - CHUNK, xi, xi - BIAS_BASE)
    idx = jnp.clip(e * CHUNK - off, -REL_CLIP, REL_CLIP) + REL_CLIP
    onehot = jnp.where(idx == ji, 1.0, 0.0).astype(BF16)
    base = _dot_exact_rhs(rbt_ref[0], onehot, 3)
    kj = lax.broadcasted_iota(jnp.int32, (CHUNK, ATTN_WINDOW), 1)
    kc = kj >> 6
    valid = (kc <= e) & (kc >= e - N_LEFT_CHUNKS)
    for h in range(8):
        rows = jnp.broadcast_to(base[h:h + 1, :], (CHUNK, BIAS_BASE))
        toep = pltpu.roll(rows, 0, 1, stride=1, stride_axis=0)
        o_ref[0, 0, h * CHUNK:(h + 1) * CHUNK, :] = jnp.where(valid, toep[:, :ATTN_WINDOW], MASK_VALUE)


def _bias_tables(rbt):
    L = rbt.shape[0]
    return pl.pallas_call(
        _bias_kernel,
        grid=(L, N_BIAS_TABLES),
        in_specs=[pl.BlockSpec((1, 8, REL_ROWS), lambda l, e: (l, 0, 0))],
        out_specs=pl.BlockSpec((1, 1, 8 * CHUNK, ATTN_WINDOW), lambda l, e: (l, e, 0, 0)),
        out_shape=jax.ShapeDtypeStruct((L, N_BIAS_TABLES, 8 * CHUNK, ATTN_WINDOW), F32),
        compiler_params=_cparams(("parallel", "parallel")),
        name="bias_tables",
    )(rbt)


ATTN_BLOCK_CHUNKS = 4


def _attn_kernel(q_ref, k_ref, v_ref, *rest):
    bias_refs, g_ref, o_ref = rest[:ATTN_BLOCK_CHUNKS], rest[-2], rest[-1]
    n0 = pl.program_id(1) * ATTN_BLOCK_CHUNKS
    q = q_ref[0] * jnp.asarray(HEAD_DIM ** -0.5, BF16)
    m1 = lax.broadcasted_iota(jnp.int32, (CHUNK, LANES), 1) < HEAD_DIM
    zero = jnp.zeros((), BF16)
    n_pairs = ATTN_WIDTH // LANES
    units = []
    for j in range(ATTN_BLOCK_CHUNKS):
        start = pl.multiple_of(jnp.maximum(n0 + j - (N_LEFT_CHUNKS + 1), 0) * CHUNK, CHUNK)
        kw = k_ref[0, pl.ds(start, ATTN_WINDOW), :]
        vw = v_ref[0, pl.ds(start, ATTN_WINDOW), :]
        qj = q[CHUNK * j:CHUNK * (j + 1)]
        for pi in range(n_pairs):
            sl = slice(LANES * pi, LANES * (pi + 1))
            qp = qj[:, sl]
            qs = jnp.concatenate([jnp.where(m1, qp, zero), jnp.where(m1, zero, qp)], axis=0)
            units.append(dict(qs=qs, k=kw[:, sl], v=vw[:, sl],
                              bias=bias_refs[j][0, 0, PAIR * pi:PAIR * (pi + 1), :]))
    for u in units:
        u["s"] = lax.dot_general(u["qs"], u["k"], (((1,), (1,)), ((), ())),
                                 preferred_element_type=F32) + u["bias"]
    for u in units:
        s = u["s"]
        ex = jnp.exp(s - jnp.max(s, axis=1, keepdims=True))
        u["den"] = jnp.sum(ex, axis=1, keepdims=True)
        u["ex"] = ex.astype(BF16)
    for u in units:
        o = jnp.dot(u["ex"], u["v"], preferred_element_type=F32) / u["den"]
        u["o"] = jnp.where(m1, o[:CHUNK], o[CHUNK:])
    rows = [jnp.concatenate([u["o"] for u in units[j * n_pairs:(j + 1) * n_pairs]], axis=1)
            for j in range(ATTN_BLOCK_CHUNKS)]
    o = jnp.concatenate(rows, axis=0)
    o_ref[0] = _rms(o, g_ref[...]).astype(BF16)


def _attn(qkv3, bias_l, l, g):
    B, S, _ = qkv3.shape
    rows = ATTN_BLOCK_CHUNKS * CHUNK
    nc = S // rows

    def bias_spec(j):
        return pl.BlockSpec(
            (1, 1, 8 * CHUNK, ATTN_WINDOW),
            lambda b, n: (l, jnp.minimum(n * ATTN_BLOCK_CHUNKS + j, N_BIAS_TABLES - 1), 0, 0))

    return pl.pallas_call(
        _attn_kernel,
        grid=(B, nc),
        in_specs=[pl.BlockSpec((1, rows, ATTN_WIDTH), lambda b, n: (b, n, 0)),
                  pl.BlockSpec((1, S, ATTN_WIDTH), lambda b, n: (b, 0, 1)),
                  pl.BlockSpec((1, S, ATTN_WIDTH), lambda b, n: (b, 0, 2))]
                 + [bias_spec(j) for j in range(ATTN_BLOCK_CHUNKS)]
                 + [pl.BlockSpec((1, ATTN_WIDTH), lambda b, n: (0, 0))],
        out_specs=pl.BlockSpec((1, rows, ATTN_WIDTH), lambda b, n: (b, n, 0)),
        out_shape=jax.ShapeDtypeStruct((B, S, ATTN_WIDTH), BF16),
        compiler_params=_cparams(("parallel", "arbitrary")),
        name="chunk_attn",
    )(qkv3, qkv3, qkv3, *([bias_l] * ATTN_BLOCK_CHUNKS), g)


def _outproj_kernel(yr_ref, ya_ref, x_ref, w_ref, g_ref, *rest, with_router):
    if with_router:
        router_ref, xo_ref, h_ref, ridx_ref, rgate_ref = rest
    else:
        xo_ref, h_ref = rest
    y = jnp.concatenate([yr_ref[...], ya_ref[...]], axis=1)
    xn = x_ref[...] + jnp.dot(y, w_ref[...], preferred_element_type=F32)
    xo_ref[...] = xn
    h = _rms(xn, g_ref[...])
    if not with_router:
        h_ref[...] = h.astype(BF16)
        return
    h_ref[...] = h.astype(BF16).astype(F32)
    lane = lax.broadcasted_iota(jnp.int32, ridx_ref.shape, 1)
    neg = jnp.asarray(-jnp.inf, F32)
    lg = jnp.where(lane < N_EXPERTS, _dot_f32(h, router_ref[...]), neg)
    top1 = jnp.max(lg, axis=1, keepdims=True)
    idx1 = jnp.min(jnp.where(lg == top1, lane, LANES), axis=1, keepdims=True)
    lg2 = jnp.where(lane == idx1, neg, lg)
    top2 = jnp.max(lg2, axis=1, keepdims=True)
    idx2 = jnp.min(jnp.where(lg2 == top2, lane, LANES), axis=1, keepdims=True)
    ex = jnp.exp(top2 - top1)
    ridx_ref[...] = jnp.where(lane == 0, idx1, jnp.where(lane == 1, idx2, 0))
    rgate_ref[...] = jnp.where(lane == 0, 1.0 / (1.0 + ex), jnp.where(lane == 1, ex / (1.0 + ex), 0.0))


def _outproj(yr, ya, x2, w, g, router, tm):
    T = x2.shape[0]
    with_router = router is not None
    in_specs = [pl.BlockSpec((tm, RWKV_WIDTH), lambda i: (i, 0)),
                pl.BlockSpec((tm, ATTN_WIDTH), lambda i: (i, 0)),
                pl.BlockSpec((tm, D_MODEL), lambda i: (i, 0)),
                pl.BlockSpec((D_MODEL, D_MODEL), lambda i: (0, 0)),
                pl.BlockSpec((1, D_MODEL), lambda i: (0, 0))]
    out_specs = [pl.BlockSpec((tm, D_MODEL), lambda i: (i, 0)),
                 pl.BlockSpec((tm, D_MODEL), lambda i: (i, 0))]
    out_shape = [jax.ShapeDtypeStruct((T, D_MODEL), F32),
                 jax.ShapeDtypeStruct((T, D_MODEL), F32 if with_router else BF16)]
    args = [yr, ya, x2, w, g]
    if with_router:
        in_specs.append(pl.BlockSpec((D_MODEL, LANES), lambda i: (0, 0)))
        out_specs += [pl.BlockSpec((tm, LANES), lambda i: (i, 0))] * 2
        out_shape += [jax.ShapeDtypeStruct((T, LANES), jnp.int32), jax.ShapeDtypeStruct((T, LANES), F32)]
        args.append(router)
    return pl.pallas_call(
        functools.partial(_outproj_kernel, with_router=with_router),
        grid=(T // tm,),
        in_specs=in_specs, out_specs=out_specs, out_shape=out_shape,
        compiler_params=_cparams(("parallel",)),
        name="outproj_router" if with_router else "outproj",
    )(*args)


FF_TILE = 256


def _ffn_kernel(h_ref, x_ref, wg_ref, wu_ref, wd_ref, *rest, final):
    if final:
        gf_ref, o_ref = rest
    else:
        (o_ref,) = rest
    h = h_ref[...]
    acc = None
    for f in range(0, wg_ref.shape[1], FF_TILE):
        gate = jnp.dot(h, wg_ref[:, f:f + FF_TILE], preferred_element_type=F32)
        up = jnp.dot(h, wu_ref[:, f:f + FF_TILE], preferred_element_type=F32)
        act = (gate * _sigmoid(gate) * up).astype(BF16)
        d = jnp.dot(act, wd_ref[f:f + FF_TILE, :], preferred_element_type=F32)
        acc = d if acc is None else acc + d
    xo = x_ref[...] + acc
    o_ref[...] = _rms(xo, gf_ref[...]) if final else xo


def _ffn(h, x2, wg, wu, wd, final_g, tm):
    T = x2.shape[0]
    F = wg.shape[1]
    final = final_g is not None
    in_specs = [pl.BlockSpec((tm, D_MODEL), lambda i: (i, 0)),
                pl.BlockSpec((tm, D_MODEL), lambda i: (i, 0)),
                pl.BlockSpec((D_MODEL, F), lambda i: (0, 0)),
                pl.BlockSpec((D_MODEL, F), lambda i: (0, 0)),
                pl.BlockSpec((F, D_MODEL), lambda i: (0, 0))]
    args = [h, x2, wg, wu, wd]
    if final:
        in_specs.append(pl.BlockSpec((1, D_MODEL), lambda i: (0, 0)))
        args.append(final_g)
    return pl.pallas_call(
        functools.partial(_ffn_kernel, final=final),
        grid=(T // tm,),
        in_specs=in_specs,
        out_specs=pl.BlockSpec((tm, D_MODEL), lambda i: (i, 0)),
        out_shape=jax.ShapeDtypeStruct((T, D_MODEL), F32),
        compiler_params=_cparams(("parallel",)),
        name="ffn_dense",
    )(*args)


MOE_TILE = 256
TOP_K = 2


def _route_tables(ridx, rgate, T):
    n_assign = TOP_K * T
    n_tiles = n_assign // MOE_TILE + N_EXPERTS
    n_rows = n_tiles * MOE_TILE
    e_flat = ridx[:, :TOP_K].T.reshape(-1)
    g_flat = rgate[:, :TOP_K].T.reshape(-1)
    onehot = (e_flat[:, None] == jnp.arange(N_EXPERTS, dtype=jnp.int32)[None, :]).astype(jnp.int32)
    csum = jnp.cumsum(onehot, axis=0)
    rank = jnp.sum((csum - onehot) * onehot, axis=1)
    counts = csum[-1]
    padded = ((counts + MOE_TILE - 1) // MOE_TILE) * MOE_TILE
    gend = jnp.cumsum(padded)
    dest = (gend - padded)[e_flat] + rank
    a_of_row = jnp.full((n_rows,), -1, jnp.int32).at[dest].set(jnp.arange(n_assign, dtype=jnp.int32))
    valid = a_of_row >= 0
    a_safe = jnp.maximum(a_of_row, 0)
    row_tok = jnp.where(valid, a_safe % T, 0)
    pad_rank = jnp.cumsum(jnp.logical_not(valid).astype(jnp.int32)) - 1
    row_dst = jnp.where(valid, a_of_row, n_assign + pad_rank)
    row_gate = jnp.where(valid, g_flat[a_safe], 0.0)
    tile_start = jnp.arange(n_tiles, dtype=jnp.int32) * MOE_TILE
    tile_exp = jnp.sum((tile_start[:, None] >= gend[None, :]).astype(jnp.int32), axis=1)
    tile_exp = jnp.minimum(tile_exp, N_EXPERTS - 1)
    return (row_tok.reshape(n_tiles, 1, MOE_TILE), row_dst.reshape(n_tiles, 1, MOE_TILE),
            row_gate.reshape(n_rows, 1), tile_exp)


def _moe_expert_kernel(texp_ref, tok_ref, tok_next_ref, dst_ref, gate_ref, h_hbm, wg_ref, wu_ref, wd_ref,
                       ys_hbm, xbuf, ybuf, gsem, ssem):
    del texp_ref
    i = pl.program_id(0)
    n = pl.num_programs(0)
    slot = i % 2

    def gather_copy(tok_smem, r, s):
        return pltpu.make_async_copy(h_hbm.at[pl.ds(tok_smem[0, 0, r], 1), :],
                                     xbuf.at[s, pl.ds(r, 1), :], gsem.at[s])

    def gather_all(s):
        return pltpu.make_async_copy(h_hbm.at[pl.ds(0, MOE_TILE), :], xbuf.at[s], gsem.at[s])

    def scatter_copy(r, s):
        return pltpu.make_async_copy(ybuf.at[s, pl.ds(r, 1), :],
                                     ys_hbm.at[pl.ds(dst_ref[0, 0, r], 1), :], ssem.at[s])

    def scatter_all(s):
        return pltpu.make_async_copy(ybuf.at[s], ys_hbm.at[pl.ds(0, MOE_TILE), :], ssem.at[s])

    @pl.when(i == 0)
    def _prologue():
        for r in range(MOE_TILE):
            gather_copy(tok_ref, r, 0).start()

    gather_all(slot).wait()
    for r in range(MOE_TILE):
        gather_copy(tok_next_ref, r, 1 - slot).start()

    x = xbuf[slot].astype(BF16)
    gate = jnp.dot(x, wg_ref[0], preferred_element_type=F32)
    up = jnp.dot(x, wu_ref[0], preferred_element_type=F32)
    act = (gate_ref[...] * (gate * _sigmoid(gate) * up)).astype(BF16)
    y = jnp.dot(act, wd_ref[0], preferred_element_type=F32)

    @pl.when(i >= 2)
    def _reuse():
        scatter_all(slot).wait()

    ybuf[slot] = y
    for r in range(MOE_TILE):
        scatter_copy(r, slot).start()

    @pl.when(i == n - 1)
    def _epilogue():
        gather_all(1 - slot).wait()
        scatter_all(slot).wait()

        @pl.when(n >= 2)
        def _():
            scatter_all(1 - slot).wait()


def _moe_experts(h32, tables, wg, wu, wd):
    row_tok, row_dst, row_gate, tile_exp = tables
    n_tiles = row_tok.shape[0]
    _, _, F = wg.shape
    smem_tile = lambda imap: pl.BlockSpec((1, 1, MOE_TILE), imap, memory_space=pltpu.SMEM)
    grid_spec = pltpu.PrefetchScalarGridSpec(
        num_scalar_prefetch=1,
        grid=(n_tiles,),
        in_specs=[smem_tile(lambda i, te: (i, 0, 0)),
                  smem_tile(lambda i, te: (jnp.minimum(i + 1, n_tiles - 1), 0, 0)),
                  smem_tile(lambda i, te: (i, 0, 0)),
                  pl.BlockSpec((MOE_TILE, 1), lambda i, te: (i, 0)),
                  pl.BlockSpec(memory_space=pl.ANY),
                  pl.BlockSpec((1, D_MODEL, F), lambda i, te: (te[i], 0, 0)),
                  pl.BlockSpec((1, D_MODEL, F), lambda i, te: (te[i], 0, 0)),
                  pl.BlockSpec((1, F, D_MODEL), lambda i, te: (te[i], 0, 0))],
        out_specs=pl.BlockSpec(memory_space=pl.ANY),
        scratch_shapes=[pltpu.VMEM((2, MOE_TILE, D_MODEL), F32),
                        pltpu.VMEM((2, MOE_TILE, D_MODEL), F32),
                        pltpu.SemaphoreType.DMA((2,)),
                        pltpu.SemaphoreType.DMA((2,))])
    return pl.pallas_call(
        _moe_expert_kernel,
        grid_spec=grid_spec,
        out_shape=jax.ShapeDtypeStruct((n_tiles * MOE_TILE, D_MODEL), F32),
        compiler_params=_cparams(("arbitrary",)),
        name="moe_experts",
    )(tile_exp, row_tok, row_tok, row_dst, row_gate, h32, wg, wu, wd)


def _moe_combine_kernel(x_ref, y1_ref, y2_ref, *rest, final):
    if final:
        gf_ref, o_ref = rest
    else:
        (o_ref,) = rest
    xo = x_ref[...] + y1_ref[...] + y2_ref[...]
    o_ref[...] = _rms(xo, gf_ref[...]) if final else xo


def _moe_combine(x2, ys, final_g, tm):
    T = x2.shape[0]
    nb = T // tm
    final = final_g is not None
    in_specs = [pl.BlockSpec((tm, D_MODEL), lambda i: (i, 0)),
                pl.BlockSpec((tm, D_MODEL), lambda i: (i, 0)),
                pl.BlockSpec((tm, D_MODEL), lambda i: (i + nb, 0))]
    args = [x2, ys, ys]
    if final:
        in_specs.append(pl.BlockSpec((1, D_MODEL), lambda i: (0, 0)))
        args.append(final_g)
    return pl.pallas_call(
        functools.partial(_moe_combine_kernel, final=final),
        grid=(nb,),
        in_specs=in_specs,
        out_specs=pl.BlockSpec((tm, D_MODEL), lambda i: (i, 0)),
        out_shape=jax.ShapeDtypeStruct((T, D_MODEL), F32),
        compiler_params=_cparams(("parallel",)),
        name="moe_combine",
    )(*args)


def _moe(h32, x2, ridx, rgate, wg, wu, wd, final_g, tm):
    tables = _route_tables(ridx, rgate, x2.shape[0])
    ys = _moe_experts(h32, tables, wg, wu, wd)
    return _moe_combine(x2, ys, final_g, tm)


def kernel(x, norm_mix_g, w_in, shift_mu, rwkv_w0, rwkv_w2, rwkv_a0, rwkv_a2, rwkv_g2, rwkv_k_k, rwkv_k_a, rwkv_r_k, rwkv_ln_w, rwkv_ln_b, attn_rel_bias, attn_norm_g, w_out, norm_ffn_g, ffn_w_gate, ffn_w_up, ffn_w_down, moe_router, moe_w_gate, moe_w_up, moe_w_down, norm_final_g):
    B, S, D = x.shape
    depth = w_in.shape[0]
    T = B * S
    tm = min(512, T)
    row = lambda t: t.reshape(1, -1).astype(F32)

    hi = jnp.arange(RWKV_WIDTH) // HEAD_DIM
    bd = (hi[:, None] == hi[None, :]).astype(BF16)
    rbt = jnp.pad(jnp.swapaxes(attn_rel_bias, 1, 2).astype(F32),
                  ((0, 0), (0, 0), (0, REL_ROWS - attn_rel_bias.shape[1])))
    bias_tabs = _bias_tables(rbt)

    x2 = x.reshape(T, D).astype(F32)
    for l in range(depth):
        w_shift = w_in[l, :, :SHIFT_COLS].astype(BF16)
        w_attn = w_in[l, :, SHIFT_COLS:].astype(BF16)
        ps, qkv = _inproj(x2, row(norm_mix_g[l]), w_shift, w_attn, tm)

        zeros = jnp.zeros((DECAY_LORA, RWKV_WIDTH), F32)
        wa = jnp.concatenate([jnp.concatenate([rwkv_w2[l], zeros], axis=1),
                              jnp.concatenate([zeros, rwkv_a2[l]], axis=1)], axis=0).astype(BF16)
        vec = jnp.stack([rwkv_w0[l], rwkv_a0[l], rwkv_k_k[l], rwkv_k_a[l], rwkv_r_k[l],
                         rwkv_ln_w[l], rwkv_ln_b[l], jnp.zeros_like(rwkv_w0[l])]).astype(F32)
        y_rwkv = _rwkv(ps.reshape(B, S, SHIFT_COLS), row(shift_mu[l]), vec, wa,
                       rwkv_g2[l].astype(BF16), bd)
        y_attn = _attn(qkv.reshape(B, S, 3 * ATTN_WIDTH), bias_tabs, l, row(attn_norm_g[l]))

        is_moe = l % 2 == 1
        li = l // 2
        router = None
        if is_moe:
            router = jnp.pad(moe_router[li].astype(F32), ((0, 0), (0, LANES - N_EXPERTS)))
        outs = _outproj(y_rwkv.reshape(T, RWKV_WIDTH), y_attn.reshape(T, ATTN_WIDTH), x2,
                        w_out[l].astype(BF16), row(norm_ffn_g[l]), router, tm)
        final_g = row(norm_final_g) if l == depth - 1 else None
        if is_moe:
            x_mid, h32, ridx, rgate = outs
            x2 = _moe(h32, x_mid, ridx, rgate, moe_w_gate[li].astype(BF16), moe_w_up[li].astype(BF16),
                      moe_w_down[li].astype(BF16), final_g, tm)
        else:
            x_mid, h = outs
            x2 = _ffn(h, x_mid, ffn_w_gate[li].astype(BF16), ffn_w_up[li].astype(BF16),
                      ffn_w_down[li].astype(BF16), final_g, tm)
    return x2.reshape(B, S, D).astype(x.dtype)
```

```python
import functools

import jax
import jax.numpy as jnp
from jax import lax
from jax.experimental import pallas as pl
from jax.experimental.pallas import tpu as pltpu

F32 = jnp.float32
BF16 = jnp.bfloat16

D_MODEL = 1024
CHUNK = 64
N_LEFT_CHUNKS = 8
HEAD_DIM = 64
RWKV_WIDTH = 512
ATTN_WIDTH = 512
DECAY_LORA = 64
AAA_LORA = 64
GATE_LORA = 128
REL_CLIP = 128
N_EXPERTS = 8
RMS_EPS = 1e-6
GN_EPS = 64e-5
MASK_VALUE = -1e30
SHIFT_COLS = 3 * RWKV_WIDTH + DECAY_LORA + AAA_LORA + GATE_LORA

LANES = 128
PAIR = 2 * CHUNK
N_PAIRS = RWKV_WIDTH // LANES
ATTN_WINDOW = (N_LEFT_CHUNKS + 2) * CHUNK
N_BIAS_TABLES = N_LEFT_CHUNKS + 2
BIAS_BASE = 768
REL_ROWS = 384
VMEM_LIMIT = 56 * 1024 * 1024


def _cparams(sem):
    return pltpu.CompilerParams(dimension_semantics=sem, vmem_limit_bytes=VMEM_LIMIT)


def _mm(a, b):
    return jnp.dot(a.astype(BF16), b.astype(BF16), preferred_element_type=F32)


def _mm_nt(a, b):
    return lax.dot_general(a.astype(BF16), b.astype(BF16), (((1,), (1,)), ((), ())),
                           preferred_element_type=F32)


def _mm_tn(a, b):
    return lax.dot_general(a.astype(BF16), b.astype(BF16), (((0,), (0,)), ((), ())),
                           preferred_element_type=F32)


def _split_terms(x, n):
    terms, rem = [], x
    for _ in range(n):
        hi = rem.astype(BF16)
        terms.append(hi)
        rem = rem - hi.astype(F32)
    return terms


def _dot_exact_rhs(x, w_bf16, n):
    acc = None
    for t in _split_terms(x, n):
        d = jnp.dot(t, w_bf16, preferred_element_type=F32)
        acc = d if acc is None else acc + d
    return acc


def _dot_exact_lhs(w_bf16, x, n):
    acc = None
    for t in _split_terms(x, n):
        d = jnp.dot(w_bf16, t, preferred_element_type=F32)
        acc = d if acc is None else acc + d
    return acc


def _dot_f32(a, b):
    a1, a2 = _split_terms(a, 2)
    b1, b2 = _split_terms(b, 2)
    n = b.shape[1]
    t = jnp.dot(a1, jnp.concatenate([b1, b2], axis=1), preferred_element_type=F32)
    return t[:, :n] + t[:, n:] + jnp.dot(a2, b1, preferred_element_type=F32)


def _sigmoid(x):
    return 1.0 / (1.0 + jnp.exp(-x))


def _rms(x, g):
    return x * lax.rsqrt(jnp.mean(x * x, axis=-1, keepdims=True) + RMS_EPS) * g


def _inproj_kernel(x_ref, g_ref, ws_ref, wa_ref, ps_ref, qkv_ref):
    hb = _rms(x_ref[...], g_ref[...]).astype(BF16)
    ps_ref[...] = jnp.dot(hb, ws_ref[...], preferred_element_type=F32)
    qkv_ref[...] = jnp.dot(hb, wa_ref[...], preferred_element_type=F32).astype(BF16)


def _inproj(x2, g, w_shift, w_attn, tm):
    T = x2.shape[0]
    ns, na = w_shift.shape[1], w_attn.shape[1]
    return pl.pallas_call(
        _inproj_kernel,
        grid=(T // tm,),
        in_specs=[pl.BlockSpec((tm, D_MODEL), lambda i: (i, 0)),
                  pl.BlockSpec((1, D_MODEL), lambda i: (0, 0)),
                  pl.BlockSpec((D_MODEL, ns), lambda i: (0, 0)),
                  pl.BlockSpec((D_MODEL, na), lambda i: (0, 0))],
        out_specs=[pl.BlockSpec((tm, ns), lambda i: (i, 0)),
                   pl.BlockSpec((tm, na), lambda i: (i, 0))],
        out_shape=[jax.ShapeDtypeStruct((T, ns), F32),
                   jax.ShapeDtypeStruct((T, na), BF16)],
        compiler_params=_cparams(("parallel",)),
        name="inproj",
    )(x2, g, w_shift, w_attn)


def _rwkv_kernel(ps_ref, prev_ref, mu_ref, vec_ref, wa_ref, g2_ref, bd_ref, y_ref, h_ref):
    c = pl.program_id(1)

    @pl.when(c == 0)
    def _init():
        h_ref[...] = jnp.zeros(h_ref.shape, F32)

    p = ps_ref[0]
    last = jnp.where(c > 0, prev_ref[0][7:8, :], 0.0)
    row = lax.broadcasted_iota(jnp.int32, p.shape, 0)
    prev = jnp.where(row == 0, last, pltpu.roll(p, 1, 0))
    xs = p + mu_ref[...] * (prev - p)

    W = RWKV_WIDTH
    r, k, v = xs[:, 0:W], xs[:, W:2 * W], xs[:, 2 * W:3 * W]
    z0 = xs[:, 3 * W:3 * W + LANES]
    gd = xs[:, 3 * W + LANES:3 * W + 2 * LANES]
    m1 = lax.broadcasted_iota(jnp.int32, (CHUNK, LANES), 1) < HEAD_DIM
    z0 = jnp.where(lax.broadcasted_iota(jnp.int32, z0.shape, 1) < DECAY_LORA, jnp.tanh(z0), z0)
    lora = _mm(z0, wa_ref[...])
    vec = vec_ref[...]
    w0, a0, k_k, k_a, r_k, ln_w, ln_b = (vec[i:i + 1] for i in range(7))
    w = w0 + lora[:, :W]
    a = _sigmoid(a0 + lora[:, W:])
    g = _mm(_sigmoid(gd), g2_ref[...])
    softplus_neg_w = jnp.maximum(-w, 0.0) + jnp.log(1.0 + jnp.exp(-jnp.abs(w)))
    lw = -jnp.exp(-softplus_neg_w - 0.5)

    bd = bd_ref[...]
    kk = k * k_k
    kk = kk / jnp.maximum(jnp.sqrt(_dot_exact_rhs(kk * kk, bd, 1)), 1e-12)
    k2 = k * (1.0 + (a - 1.0) * k_a)
    kka = kk * a

    rows = p.shape[0]
    n_chunks = rows // CHUNK
    ti = lax.broadcasted_iota(jnp.int32, (rows, rows), 0)
    tj = lax.broadcasted_iota(jnp.int32, (rows, rows), 1)
    tri = jnp.where((ti >= tj) & ((ti >> 6) == (tj >> 6)), 1.0, 0.0).astype(BF16)
    L = _dot_exact_lhs(tri, lw, 2)

    ri = lax.broadcasted_iota(jnp.int32, (PAIR, PAIR), 0)
    ci = lax.broadcasted_iota(jnp.int32, (PAIR, PAIR), 1)
    same_head = (ri >> 6) == (ci >> 6)
    strict = same_head & (ri > ci)
    incl = same_head & (ri >= ci)
    eye = ri == ci
    eye_f = jnp.where(eye, 1.0, 0.0)

    units = []
    for j in range(n_chunks):
        rs = slice(CHUNK * j, CHUNK * (j + 1))
        Lj, lwj = L[rs], lw[rs]
        Lc = Lj[CHUNK - 1:CHUNK]
        inv = jnp.exp(-Lj)
        to_end = jnp.exp(Lc - Lj)
        gC = jnp.exp(Lc)
        Rt = r[rs] * jnp.exp(Lj)
        At = -kk[rs] * jnp.exp(Lj - lwj)
        Bt, Kt = kka[rs] * inv, k2[rs] * inv
        Bh, Kh = kka[rs] * to_end, k2[rs] * to_end
        vj = v[rs]
        for pi in range(N_PAIRS):
            sl = slice(LANES * pi, LANES * (pi + 1))

            def stack(x):
                xp = x[:, sl]
                return jnp.concatenate([jnp.where(m1, xp, 0.0), jnp.where(m1, 0.0, xp)], axis=0)

            units.append(dict(
                j=j, pi=pi, gC=gC[:, sl], sRt=stack(Rt),
                sAt=stack(At).astype(BF16), sV=stack(vj).astype(BF16),
                sBt=stack(Bt).astype(BF16), sKt=stack(Kt).astype(BF16),
                sBh=stack(Bh).astype(BF16), sKh=stack(Kh).astype(BF16)))

    for u in units:
        big = _mm_nt(jnp.concatenate([u["sAt"], u["sRt"].astype(BF16)], axis=0),
                     jnp.concatenate([u["sBt"], u["sKt"]], axis=0))
        u["AB"] = jnp.where(strict, big[:PAIR, :PAIR], 0.0)
        u["AK"] = jnp.where(strict, big[:PAIR, PAIR:], 0.0)
        u["RB"] = jnp.where(incl, big[PAIR:, :PAIR], 0.0)
        u["RK"] = jnp.where(incl, big[PAIR:, PAIR:], 0.0)
    for u in units:
        u["X"] = eye_f + u["AB"]
        u["Pw"] = _mm(u["AB"], u["AB"])
        u["W1"] = _mm(u["AK"], u["sV"])
    for _ in range(4):
        for u in units:
            PX = _mm(u["Pw"], jnp.concatenate([u["Pw"], u["X"]], axis=1))
            u["Pw"] = PX[:, :PAIR]
            u["X"] = u["X"] + PX[:, PAIR:]
    for u in units:
        u["Tm"] = u["X"] + _mm(u["Pw"], u["X"])
    for u in units:
        u["PQ"] = _mm(u["Tm"], jnp.concatenate([u["sAt"], u["W1"].astype(BF16)], axis=1)).astype(BF16)
    for u in units:
        PQ = u["PQ"]
        Pm, Q = PQ[:, :PAIR], PQ[:, PAIR:]
        RBPQ = _mm(u["RB"], PQ)
        u["Rp"] = u["sRt"] + RBPQ[:, :PAIR]
        u["Y0"] = RBPQ[:, PAIR:] + _mm(u["RK"], u["sV"])
        u["Mm"] = jnp.where(eye, u["gC"], 0.0) + _mm_tn(u["sBh"], Pm)
        u["G"] = _mm_tn(jnp.concatenate([u["sBh"], u["sKh"]], axis=0),
                        jnp.concatenate([Q, u["sV"]], axis=0))
    H = [h_ref[pi] for pi in range(N_PAIRS)]
    y_rows = []
    for j in range(n_chunks):
        ys = []
        for u in units[j * N_PAIRS:(j + 1) * N_PAIRS]:
            pi = u["pi"]
            YH = _mm(jnp.concatenate([u["Rp"], u["Mm"]], axis=0), H[pi])
            Ysm = YH[:PAIR] + u["Y0"]
            H[pi] = YH[PAIR:] + u["G"]
            ys.append(Ysm[:CHUNK] + Ysm[CHUNK:])
        y_rows.append(jnp.concatenate(ys, axis=1))
    for pi in range(N_PAIRS):
        h_ref[pi] = H[pi]
    y = y_rows[0] if n_chunks == 1 else jnp.concatenate(y_rows, axis=0)

    inv_n = 1.0 / HEAD_DIM
    mean = _dot_exact_rhs(y, bd, 1) * inv_n
    d = y - mean
    var = _dot_exact_rhs(d * d, bd, 1) * inv_n
    yn = d * lax.rsqrt(var + GN_EPS) * ln_w + ln_b
    bonus = _dot_exact_rhs(r * k2 * r_k, bd, 1) * v
    y_ref[0] = ((yn + bonus) * g).astype(BF16)


RWKV_BLOCK_CHUNKS = 4


def _rwkv(ps3, mu, vec, wa, g2, bd):
    B, S, _ = ps3.shape
    rows = RWKV_BLOCK_CHUNKS * CHUNK
    nc = S // rows
    rows8 = rows // 8
    return pl.pallas_call(
        _rwkv_kernel,
        grid=(B, nc),
        in_specs=[pl.BlockSpec((1, rows, SHIFT_COLS), lambda b, c: (b, c, 0)),
                  pl.BlockSpec((1, 8, SHIFT_COLS), lambda b, c: (b, jnp.maximum(c * rows8 - 1, 0), 0)),
                  pl.BlockSpec((1, SHIFT_COLS), lambda b, c: (0, 0)),
                  pl.BlockSpec((8, RWKV_WIDTH), lambda b, c: (0, 0)),
                  pl.BlockSpec((LANES, 2 * RWKV_WIDTH), lambda b, c: (0, 0)),
                  pl.BlockSpec((GATE_LORA, RWKV_WIDTH), lambda b, c: (0, 0)),
                  pl.BlockSpec((RWKV_WIDTH, RWKV_WIDTH), lambda b, c: (0, 0))],
        out_specs=pl.BlockSpec((1, rows, RWKV_WIDTH), lambda b, c: (b, c, 0)),
        out_shape=jax.ShapeDtypeStruct((B, S, RWKV_WIDTH), BF16),
        scratch_shapes=[pltpu.VMEM((N_PAIRS, PAIR, LANES), F32)],
        compiler_params=_cparams(("parallel", "arbitrary")),
        name="rwkv7",
    )(ps3, ps3, mu, vec, wa, g2, bd)


def _bias_kernel(rbt_ref, o_ref):
    e = pl.program_id(1)
    xi = lax.broadcasted_iota(jnp.int32, (REL_ROWS, BIAS_BASE), 1)
    ji = lax.broadcasted_iota(jnp.int32, (REL_ROWS, BIAS_BASE), 0)
    off = jnp.where(xi < BIAS_BASE - CHUNK, xi, xi - BIAS_BASE)
    idx = jnp.clip(e * CHUNK - off, -REL_CLIP, REL_CLIP) + REL_CLIP
    onehot = jnp.where(idx == ji, 1.0, 0.0).astype(BF16)
    base = _dot_exact_rhs(rbt_ref[0], onehot, 3)
    kj = lax.broadcasted_iota(jnp.int32, (CHUNK, ATTN_WINDOW), 1)
    kc = kj >> 6
    valid = (kc <= e) & (kc >= e - N_LEFT_CHUNKS)
    for h in range(8):
        rows = jnp.broadcast_to(base[h:h + 1, :], (CHUNK, BIAS_BASE))
        toep = pltpu.roll(rows, 0, 1, stride=1, stride_axis=0)
        o_ref[0, 0, h * CHUNK:(h + 1) * CHUNK, :] = jnp.where(valid, toep[:, :ATTN_WINDOW], MASK_VALUE)


def _bias_tables(rbt):
    L = rbt.shape[0]
    return pl.pallas_call(
        _bias_kernel,
        grid=(L, N_BIAS_TABLES),
        in_specs=[pl.BlockSpec((1, 8, REL_ROWS), lambda l, e: (l, 0, 0))],
        out_specs=pl.BlockSpec((1, 1, 8 * CHUNK, ATTN_WINDOW), lambda l, e: (l, e, 0, 0)),
        out_shape=jax.ShapeDtypeStruct((L, N_BIAS_TABLES, 8 * CHUNK, ATTN_WINDOW), F32),
        compiler_params=_cparams(("parallel", "parallel")),
        name="bias_tables",
    )(rbt)


ATTN_BLOCK_CHUNKS = 4


def _attn_kernel(q_ref, k_ref, v_ref, *rest):
    bias_refs, g_ref, o_ref = rest[:ATTN_BLOCK_CHUNKS], rest[-2], rest[-1]
    n0 = pl.program_id(1) * ATTN_BLOCK_CHUNKS
    q = q_ref[0] * jnp.asarray(HEAD_DIM ** -0.5, BF16)
    m1 = lax.broadcasted_iota(jnp.int32, (CHUNK, LANES), 1) < HEAD_DIM
    zero = jnp.zeros((), BF16)
    n_pairs = ATTN_WIDTH // LANES
    units = []
    for j in range(ATTN_BLOCK_CHUNKS):
        start = pl.multiple_of(jnp.maximum(n0 + j - (N_LEFT_CHUNKS + 1), 0) * CHUNK, CHUNK)
        kw = k_ref[0, pl.ds(start, ATTN_WINDOW), :]
        vw = v_ref[0, pl.ds(start, ATTN_WINDOW), :]
        qj = q[CHUNK * j:CHUNK * (j + 1)]
        for pi in range(n_pairs):
            sl = slice(LANES * pi, LANES * (pi + 1))
            qp = qj[:, sl]
            qs = jnp.concatenate([jnp.where(m1, qp, zero), jnp.where(m1, zero, qp)], axis=0)
            units.append(dict(qs=qs, k=kw[:, sl], v=vw[:, sl],
                              bias=bias_refs[j][0, 0, PAIR * pi:PAIR * (pi + 1), :]))
    for u in units:
        u["s"] = lax.dot_general(u["qs"], u["k"], (((1,), (1,)), ((), ())),
                                 preferred_element_type=F32) + u["bias"]
    for u in units:
        s = u["s"]
        ex = jnp.exp(s - jnp.max(s, axis=1, keepdims=True))
        u["den"] = jnp.sum(ex, axis=1, keepdims=True)
        u["ex"] = ex.astype(BF16)
    for u in units:
        o = jnp.dot(u["ex"], u["v"], preferred_element_type=F32) / u["den"]
        u["o"] = jnp.where(m1, o[:CHUNK], o[CHUNK:])
    rows = [jnp.concatenate([u["o"] for u in units[j * n_pairs:(j + 1) * n_pairs]], axis=1)
            for j in range(ATTN_BLOCK_CHUNKS)]
    o = jnp.concatenate(rows, axis=0)
    o_ref[0] = _rms(o, g_ref[...]).astype(BF16)


def _attn(qkv3, bias_l, l, g):
    B, S, _ = qkv3.shape
    rows = ATTN_BLOCK_CHUNKS * CHUNK
    nc = S // rows

    def bias_spec(j):
        return pl.BlockSpec(
            (1, 1, 8 * CHUNK, ATTN_WINDOW),
            lambda b, n: (l, jnp.minimum(n * ATTN_BLOCK_CHUNKS + j, N_BIAS_TABLES - 1), 0, 0))

    return pl.pallas_call(
        _attn_kernel,
        grid=(B, nc),
        in_specs=[pl.BlockSpec((1, rows, ATTN_WIDTH), lambda b, n: (b, n, 0)),
                  pl.BlockSpec((1, S, ATTN_WIDTH), lambda b, n: (b, 0, 1)),
                  pl.BlockSpec((1, S, ATTN_WIDTH), lambda b, n: (b, 0, 2))]
                 + [bias_spec(j) for j in range(ATTN_BLOCK_CHUNKS)]
                 + [pl.BlockSpec((1, ATTN_WIDTH), lambda b, n: (0, 0))],
        out_specs=pl.BlockSpec((1, rows, ATTN_WIDTH), lambda b, n: (b, n, 0)),
        out_shape=jax.ShapeDtypeStruct((B, S, ATTN_WIDTH), BF16),
        compiler_params=_cparams(("parallel", "arbitrary")),
        name="chunk_attn",
    )(qkv3, qkv3, qkv3, *([bias_l] * ATTN_BLOCK_CHUNKS), g)


def _outproj_kernel(yr_ref, ya_ref, x_ref, w_ref, g_ref, *rest, with_router):
    if with_router:
        router_ref, xo_ref, h_ref, ridx_ref, rgate_ref = rest
    else:
        xo_ref, h_ref = rest
    y = jnp.concatenate([yr_ref[...], ya_ref[...]], axis=1)
    xn = x_ref[...] + jnp.dot(y, w_ref[...], preferred_element_type=F32)
    xo_ref[...] = xn
    h = _rms(xn, g_ref[...])
    if not with_router:
        h_ref[...] = h.astype(BF16)
        return
    h_ref[...] = h.astype(BF16).astype(F32)
    lane = lax.broadcasted_iota(jnp.int32, ridx_ref.shape, 1)
    neg = jnp.asarray(-jnp.inf, F32)
    lg = jnp.where(lane < N_EXPERTS, _dot_f32(h, router_ref[...]), neg)
    top1 = jnp.max(lg, axis=1, keepdims=True)
    idx1 = jnp.min(jnp.where(lg == top1, lane, LANES), axis=1, keepdims=True)
    lg2 = jnp.where(lane == idx1, neg, lg)
    top2 = jnp.max(lg2, axis=1, keepdims=True)
    idx2 = jnp.min(jnp.where(lg2 == top2, lane, LANES), axis=1, keepdims=True)
    ex = jnp.exp(top2 - top1)
    ridx_ref[...] = jnp.where(lane == 0, idx1, jnp.where(lane == 1, idx2, 0))
    rgate_ref[...] = jnp.where(lane == 0, 1.0 / (1.0 + ex), jnp.where(lane == 1, ex / (1.0 + ex), 0.0))


def _outproj(yr, ya, x2, w, g, router, tm):
    T = x2.shape[0]
    with_router = router is not None
    in_specs = [pl.BlockSpec((tm, RWKV_WIDTH), lambda i: (i, 0)),
                pl.BlockSpec((tm, ATTN_WIDTH), lambda i: (i, 0)),
                pl.BlockSpec((tm, D_MODEL), lambda i: (i, 0)),
                pl.BlockSpec((D_MODEL, D_MODEL), lambda i: (0, 0)),
                pl.BlockSpec((1, D_MODEL), lambda i: (0, 0))]
    out_specs = [pl.BlockSpec((tm, D_MODEL), lambda i: (i, 0)),
                 pl.BlockSpec((tm, D_MODEL), lambda i: (i, 0))]
    out_shape = [jax.ShapeDtypeStruct((T, D_MODEL), F32),
                 jax.ShapeDtypeStruct((T, D_MODEL), F32 if with_router else BF16)]
    args = [yr, ya, x2, w, g]
    if with_router:
        in_specs.append(pl.BlockSpec((D_MODEL, LANES), lambda i: (0, 0)))
        out_specs += [pl.BlockSpec((tm, LANES), lambda i: (i, 0))] * 2
        out_shape += [jax.ShapeDtypeStruct((T, LANES), jnp.int32), jax.ShapeDtypeStruct((T, LANES), F32)]
        args.append(router)
    return pl.pallas_call(
        functools.partial(_outproj_kernel, with_router=with_router),
        grid=(T // tm,),
        in_specs=in_specs, out_specs=out_specs, out_shape=out_shape,
        compiler_params=_cparams(("parallel",)),
        name="outproj_router" if with_router else "outproj",
    )(*args)


FF_TILE = 256


def _ffn_kernel(h_ref, x_ref, wg_ref, wu_ref, wd_ref, *rest, final):
    if final:
        gf_ref, o_ref = rest
    else:
        (o_ref,) = rest
    h = h_ref[...]
    acc = None
    for f in range(0, wg_ref.shape[1], FF_TILE):
        gate = jnp.dot(h, wg_ref[:, f:f + FF_TILE], preferred_element_type=F32)
        up = jnp.dot(h, wu_ref[:, f:f + FF_TILE], preferred_element_type=F32)
        act = (gate * _sigmoid(gate) * up).astype(BF16)
        d = jnp.dot(act, wd_ref[f:f + FF_TILE, :], preferred_element_type=F32)
        acc = d if acc is None else acc + d
    xo = x_ref[...] + acc
    o_ref[...] = _rms(xo, gf_ref[...]) if final else xo


def _ffn(h, x2, wg, wu, wd, final_g, tm):
    T = x2.shape[0]
    F = wg.shape[1]
    final = final_g is not None
    in_specs = [pl.BlockSpec((tm, D_MODEL), lambda i: (i, 0)),
                pl.BlockSpec((tm, D_MODEL), lambda i: (i, 0)),
                pl.BlockSpec((D_MODEL, F), lambda i: (0, 0)),
                pl.BlockSpec((D_MODEL, F), lambda i: (0, 0)),
                pl.BlockSpec((F, D_MODEL), lambda i: (0, 0))]
    args = [h, x2, wg, wu, wd]
    if final:
        in_specs.append(pl.BlockSpec((1, D_MODEL), lambda i: (0, 0)))
        args.append(final_g)
    return pl.pallas_call(
        functools.partial(_ffn_kernel, final=final),
        grid=(T // tm,),
        in_specs=in_specs,
        out_specs=pl.BlockSpec((tm, D_MODEL), lambda i: (i, 0)),
        out_shape=jax.ShapeDtypeStruct((T, D_MODEL), F32),
        compiler_params=_cparams(("parallel",)),
        name="ffn_dense",
    )(*args)


MOE_TILE = 256
TOP_K = 2


def _route_tables(ridx, rgate, T):
    n_assign = TOP_K * T
    n_tiles = n_assign // MOE_TILE + N_EXPERTS
    n_rows = n_tiles * MOE_TILE
    e_flat = ridx[:, :TOP_K].T.reshape(-1)
    g_flat = rgate[:, :TOP_K].T.reshape(-1)
    onehot = (e_flat[:, None] == jnp.arange(N_EXPERTS, dtype=jnp.int32)[None, :]).astype(jnp.int32)
    csum = jnp.cumsum(onehot, axis=0)
    rank = jnp.sum((csum - onehot) * onehot, axis=1)
    counts = csum[-1]
    padded = ((counts + MOE_TILE - 1) // MOE_TILE) * MOE_TILE
    gend = jnp.cumsum(padded)
    dest = (gend - padded)[e_flat] + rank
    a_of_row = jnp.full((n_rows,), -1, jnp.int32).at[dest].set(jnp.arange(n_assign, dtype=jnp.int32))
    valid = a_of_row >= 0
    a_safe = jnp.maximum(a_of_row, 0)
    row_tok = jnp.where(valid, a_safe % T, 0)
    pad_rank = jnp.cumsum(jnp.logical_not(valid).astype(jnp.int32)) - 1
    row_dst = jnp.where(valid, a_of_row, n_assign + pad_rank)
    row_gate = jnp.where(valid, g_flat[a_safe], 0.0)
    tile_start = jnp.arange(n_tiles, dtype=jnp.int32) * MOE_TILE
    tile_exp = jnp.sum((tile_start[:, None] >= gend[None, :]).astype(jnp.int32), axis=1)
    tile_exp = jnp.minimum(tile_exp, N_EXPERTS - 1)
    return (row_tok.reshape(n_tiles, 1, MOE_TILE), row_dst.reshape(n_tiles, 1, MOE_TILE),
            row_gate.reshape(n_rows, 1), tile_exp)


def _moe_expert_kernel(texp_ref, tok_ref, tok_next_ref, dst_ref, gate_ref, h_hbm, wg_ref, wu_ref, wd_ref,
                       ys_hbm, xbuf, ybuf, gsem, ssem):
    del texp_ref
    i = pl.program_id(0)
    n = pl.num_programs(0)
    slot = i % 2

    def gather_copy(tok_smem, r, s):
        return pltpu.make_async_copy(h_hbm.at[pl.ds(tok_smem[0, 0, r], 1), :],
                                     xbuf.at[s, pl.ds(r, 1), :], gsem.at[s])

    def gather_all(s):
        return pltpu.make_async_copy(h_hbm.at[pl.ds(0, MOE_TILE), :], xbuf.at[s], gsem.at[s])

    def scatter_copy(r, s):
        return pltpu.make_async_copy(ybuf.at[s, pl.ds(r, 1), :],
                                     ys_hbm.at[pl.ds(dst_ref[0, 0, r], 1), :], ssem.at[s])

    def scatter_all(s):
        return pltpu.make_async_copy(ybuf.at[s], ys_hbm.at[pl.ds(0, MOE_TILE), :], ssem.at[s])

    @pl.when(i == 0)
    def _prologue():
        for r in range(MOE_TILE):
            gather_copy(tok_ref, r, 0).start()

    gather_all(slot).wait()
    for r in range(MOE_TILE):
        gather_copy(tok_next_ref, r, 1 - slot).start()

    x = xbuf[slot].astype(BF16)
    gate = jnp.dot(x, wg_ref[0], preferred_element_type=F32)
    up = jnp.dot(x, wu_ref[0], preferred_element_type=F32)
    act = (gate_ref[...] * (gate * _sigmoid(gate) * up)).astype(BF16)
    y = jnp.dot(act, wd_ref[0], preferred_element_type=F32)

    @pl.when(i >= 2)
    def _reuse():
        scatter_all(slot).wait()

    ybuf[slot] = y
    for r in range(MOE_TILE):
        scatter_copy(r, slot).start()

    @pl.when(i == n - 1)
    def _epilogue():
        gather_all(1 - slot).wait()
        scatter_all(slot).wait()

        @pl.when(n >= 2)
        def _():
            scatter_all(1 - slot).wait()


def _moe_experts(h32, tables, wg, wu, wd):
    row_tok, row_dst, row_gate, tile_exp = tables
    n_tiles = row_tok.shape[0]
    _, _, F = wg.shape
    smem_tile = lambda imap: pl.BlockSpec((1, 1, MOE_TILE), imap, memory_space=pltpu.SMEM)
    grid_spec = pltpu.PrefetchScalarGridSpec(
        num_scalar_prefetch=1,
        grid=(n_tiles,),
        in_specs=[smem_tile(lambda i, te: (i, 0, 0)),
                  smem_tile(lambda i, te: (jnp.minimum(i + 1, n_tiles - 1), 0, 0)),
                  smem_tile(lambda i, te: (i, 0, 0)),
                  pl.BlockSpec((MOE_TILE, 1), lambda i, te: (i, 0)),
                  pl.BlockSpec(memory_space=pl.ANY),
                  pl.BlockSpec((1, D_MODEL, F), lambda i, te: (te[i], 0, 0)),
                  pl.BlockSpec((1, D_MODEL, F), lambda i, te: (te[i], 0, 0)),
                  pl.BlockSpec((1, F, D_MODEL), lambda i, te: (te[i], 0, 0))],
        out_specs=pl.BlockSpec(memory_space=pl.ANY),
        scratch_shapes=[pltpu.VMEM((2, MOE_TILE, D_MODEL), F32),
                        pltpu.VMEM((2, MOE_TILE, D_MODEL), F32),
                        pltpu.SemaphoreType.DMA((2,)),
                        pltpu.SemaphoreType.DMA((2,))])
    return pl.pallas_call(
        _moe_expert_kernel,
        grid_spec=grid_spec,
        out_shape=jax.ShapeDtypeStruct((n_tiles * MOE_TILE, D_MODEL), F32),
        compiler_params=_cparams(("arbitrary",)),
        name="moe_experts",
    )(tile_exp, row_tok, row_tok, row_dst, row_gate, h32, wg, wu, wd)


def _moe_combine_kernel(x_ref, y1_ref, y2_ref, *rest, final):
    if final:
        gf_ref, o_ref = rest
    else:
        (o_ref,) = rest
    xo = x_ref[...] + y1_ref[...] + y2_ref[...]
    o_ref[...] = _rms(xo, gf_ref[...]) if final else xo


def _moe_combine(x2, ys, final_g, tm):
    T = x2.shape[0]
    nb = T // tm
    final = final_g is not None
    in_specs = [pl.BlockSpec((tm, D_MODEL), lambda i: (i, 0)),
                pl.BlockSpec((tm, D_MODEL), lambda i: (i, 0)),
                pl.BlockSpec((tm, D_MODEL), lambda i: (i + nb, 0))]
    args = [x2, ys, ys]
    if final:
        in_specs.append(pl.BlockSpec((1, D_MODEL), lambda i: (0, 0)))
        args.append(final_g)
    return pl.pallas_call(
        functools.partial(_moe_combine_kernel, final=final),
        grid=(nb,),
        in_specs=in_specs,
        out_specs=pl.BlockSpec((tm, D_MODEL), lambda i: (i, 0)),
        out_shape=jax.ShapeDtypeStruct((T, D_MODEL), F32),
        compiler_params=_cparams(("parallel",)),
        name="moe_combine",
    )(*args)


def _moe(h32, x2, ridx, rgate, wg, wu, wd, final_g, tm):
    tables = _route_tables(ridx, rgate, x2.shape[0])
    ys = _moe_experts(h32, tables, wg, wu, wd)
    return _moe_combine(x2, ys, final_g, tm)


def kernel(x, norm_mix_g, w_in, shift_mu, rwkv_w0, rwkv_w2, rwkv_a0, rwkv_a2, rwkv_g2, rwkv_k_k, rwkv_k_a, rwkv_r_k, rwkv_ln_w, rwkv_ln_b, attn_rel_bias, attn_norm_g, w_out, norm_ffn_g, ffn_w_gate, ffn_w_up, ffn_w_down, moe_router, moe_w_gate, moe_w_up, moe_w_down, norm_final_g):
    B, S, D = x.shape
    depth = w_in.shape[0]
    T = B * S
    tm = min(512, T)
    row = lambda t: t.reshape(1, -1).astype(F32)

    hi = jnp.arange(RWKV_WIDTH) // HEAD_DIM
    bd = (hi[:, None] == hi[None, :]).astype(BF16)
    rbt = jnp.pad(jnp.swapaxes(attn_rel_bias, 1, 2).astype(F32),
                  ((0, 0), (0, 0), (0, REL_ROWS - attn_rel_bias.shape[1])))
    bias_tabs = _bias_tables(rbt)

    x2 = x.reshape(T, D).astype(F32)
    for l in range(depth):
        w_shift = w_in[l, :, :SHIFT_COLS].astype(BF16)
        w_attn = w_in[l, :, SHIFT_COLS:].astype(BF16)
        ps, qkv = _inproj(x2, row(norm_mix_g[l]), w_shift, w_attn, tm)

        zeros = jnp.zeros((DECAY_LORA, RWKV_WIDTH), F32)
        wa = jnp.concatenate([jnp.concatenate([rwkv_w2[l], zeros], axis=1),
                              jnp.concatenate([zeros, rwkv_a2[l]], axis=1)], axis=0).astype(BF16)
        vec = jnp.stack([rwkv_w0[l], rwkv_a0[l], rwkv_k_k[l], rwkv_k_a[l], rwkv_r_k[l],
                         rwkv_ln_w[l], rwkv_ln_b[l], jnp.zeros_like(rwkv_w0[l])]).astype(F32)
        y_rwkv = _rwkv(ps.reshape(B, S, SHIFT_COLS), row(shift_mu[l]), vec, wa,
                       rwkv_g2[l].astype(BF16), bd)
        y_attn = _attn(qkv.reshape(B, S, 3 * ATTN_WIDTH), bias_tabs, l, row(attn_norm_g[l]))

        is_moe = l % 2 == 1
        li = l // 2
        router = None
        if is_moe:
            router = jnp.pad(moe_router[li].astype(F32), ((0, 0), (0, LANES - N_EXPERTS)))
        outs = _outproj(y_rwkv.reshape(T, RWKV_WIDTH), y_attn.reshape(T, ATTN_WIDTH), x2,
                        w_out[l].astype(BF16), row(norm_ffn_g[l]), router, tm)
        final_g = row(norm_final_g) if l == depth - 1 else None
        if is_moe:
            x_mid, h32, ridx, rgate = outs
            x2 = _moe(h32, x_mid, ridx, rgate, moe_w_gate[li].astype(BF16), moe_w_up[li].astype(BF16),
                      moe_w_down[li].astype(BF16), final_g, tm)
        else:
            x_mid, h = outs
            x2 = _ffn(h, x_mid, ffn_w_gate[li].astype(BF16), ffn_w_up[li].astype(BF16),
                      ffn_w_down[li].astype(BF16), final_g, tm)
    return x2.reshape(B, S, D).astype(x.dtype)
```

```python
import functools

import jax
import jax.numpy as jnp
from jax import lax
from jax.experimental import pallas as pl
from jax.experimental.pallas import tpu as pltpu

F32 = jnp.float32
BF16 = jnp.bfloat16

D_MODEL = 1024
CHUNK = 64
N_LEFT_CHUNKS = 8
HEAD_DIM = 64
RWKV_WIDTH = 512
ATTN_WIDTH = 512
DECAY_LORA = 64
AAA_LORA = 64
GATE_LORA = 128
REL_CLIP = 128
N_EXPERTS = 8
RMS_EPS = 1e-6
GN_EPS = 64e-5
MASK_VALUE = -1e30
SHIFT_COLS = 3 * RWKV_WIDTH + DECAY_LORA + AAA_LORA + GATE_LORA

LANES = 128
PAIR = 2 * CHUNK
N_PAIRS = RWKV_WIDTH // LANES
ATTN_WINDOW = (N_LEFT_CHUNKS + 2) * CHUNK
N_BIAS_TABLES = N_LEFT_CHUNKS + 2
BIAS_BASE = 768
REL_ROWS = 384
VMEM_LIMIT = 56 * 1024 * 1024


def _cparams(sem):
    return pltpu.CompilerParams(dimension_semantics=sem, vmem_limit_bytes=VMEM_LIMIT)


def _mm(a, b):
    return jnp.dot(a.astype(BF16), b.astype(BF16), preferred_element_type=F32)


def _mm_nt(a, b):
    return lax.dot_general(a.astype(BF16), b.astype(BF16), (((1,), (1,)), ((), ())),
                           preferred_element_type=F32)


def _mm_tn(a, b):
    return lax.dot_general(a.astype(BF16), b.astype(BF16), (((0,), (0,)), ((), ())),
                           preferred_element_type=F32)


def _split_terms(x, n):
    terms, rem = [], x
    for _ in range(n):
        hi = rem.astype(BF16)
        terms.append(hi)
        rem = rem - hi.astype(F32)
    return terms


def _dot_exact_rhs(x, w_bf16, n):
    acc = None
    for t in _split_terms(x, n):
        d = jnp.dot(t, w_bf16, preferred_element_type=F32)
        acc = d if acc is None else acc + d
    return acc


def _dot_exact_lhs(w_bf16, x, n):
    acc = None
    for t in _split_terms(x, n):
        d = jnp.dot(w_bf16, t, preferred_element_type=F32)
        acc = d if acc is None else acc + d
    return acc


def _dot_f32(a, b):
    a1, a2 = _split_terms(a, 2)
    b1, b2 = _split_terms(b, 2)
    n = b.shape[1]
    t = jnp.dot(a1, jnp.concatenate([b1, b2], axis=1), preferred_element_type=F32)
    return t[:, :n] + t[:, n:] + jnp.dot(a2, b1, preferred_element_type=F32)


def _sigmoid(x):
    return 1.0 / (1.0 + jnp.exp(-x))


def _rms(x, g):
    return x * lax.rsqrt(jnp.mean(x * x, axis=-1, keepdims=True) + RMS_EPS) * g


def _inproj_kernel(x_ref, g_ref, ws_ref, wa_ref, ps_ref, qkv_ref):
    hb = _rms(x_ref[...], g_ref[...]).astype(BF16)
    ps_ref[...] = jnp.dot(hb, ws_ref[...], preferred_element_type=F32)
    qkv_ref[...] = jnp.dot(hb, wa_ref[...], preferred_element_type=F32).astype(BF16)


def _inproj(x2, g, w_shift, w_attn, tm):
    T = x2.shape[0]
    ns, na = w_shift.shape[1], w_attn.shape[1]
    return pl.pallas_call(
        _inproj_kernel,
        grid=(T // tm,),
        in_specs=[pl.BlockSpec((tm, D_MODEL), lambda i: (i, 0)),
                  pl.BlockSpec((1, D_MODEL), lambda i: (0, 0)),
                  pl.BlockSpec((D_MODEL, ns), lambda i: (0, 0)),
                  pl.BlockSpec((D_MODEL, na), lambda i: (0, 0))],
        out_specs=[pl.BlockSpec((tm, ns), lambda i: (i, 0)),
                   pl.BlockSpec((tm, na), lambda i: (i, 0))],
        out_shape=[jax.ShapeDtypeStruct((T, ns), F32),
                   jax.ShapeDtypeStruct((T, na), BF16)],
        compiler_params=_cparams(("parallel",)),
        name="inproj",
    )(x2, g, w_shift, w_attn)


def _rwkv_kernel(ps_ref, prev_ref, mu_ref, vec_ref, wa_ref, g2_ref, bd_ref, y_ref, h_ref):
    c = pl.program_id(1)

    @pl.when(c == 0)
    def _init():
        h_ref[...] = jnp.zeros(h_ref.shape, F32)

    p = ps_ref[0]
    last = jnp.where(c > 0, prev_ref[0][7:8, :], 0.0)
    row = lax.broadcasted_iota(jnp.int32, p.shape, 0)
    prev = jnp.where(row == 0, last, pltpu.roll(p, 1, 0))
    xs = p + mu_ref[...] * (prev - p)

    W = RWKV_WIDTH
    r, k, v = xs[:, 0:W], xs[:, W:2 * W], xs[:, 2 * W:3 * W]
    z0 = xs[:, 3 * W:3 * W + LANES]
    gd = xs[:, 3 * W + LANES:3 * W + 2 * LANES]
    m1 = lax.broadcasted_iota(jnp.int32, (CHUNK, LANES), 1) < HEAD_DIM
    z0 = jnp.where(lax.broadcasted_iota(jnp.int32, z0.shape, 1) < DECAY_LORA, jnp.tanh(z0), z0)
    lora = _mm(z0, wa_ref[...])
    vec = vec_ref[...]
    w0, a0, k_k, k_a, r_k, ln_w, ln_b = (vec[i:i + 1] for i in range(7))
    w = w0 + lora[:, :W]
    a = _sigmoid(a0 + lora[:, W:])
    g = _mm(_sigmoid(gd), g2_ref[...])
    softplus_neg_w = jnp.maximum(-w, 0.0) + jnp.log(1.0 + jnp.exp(-jnp.abs(w)))
    lw = -jnp.exp(-softplus_neg_w - 0.5)

    bd = bd_ref[...]
    kk = k * k_k
    kk = kk / jnp.maximum(jnp.sqrt(_dot_exact_rhs(kk * kk, bd, 1)), 1e-12)
    k2 = k * (1.0 + (a - 1.0) * k_a)
    kka = kk * a

    rows = p.shape[0]
    n_chunks = rows // CHUNK
    ti = lax.broadcasted_iota(jnp.int32, (rows, rows), 0)
    tj = lax.broadcasted_iota(jnp.int32, (rows, rows), 1)
    tri = jnp.where((ti >= tj) & ((ti >> 6) == (tj >> 6)), 1.0, 0.0).astype(BF16)
    L = _dot_exact_lhs(tri, lw, 2)

    ri = lax.broadcasted_iota(jnp.int32, (PAIR, PAIR), 0)
    ci = lax.broadcasted_iota(jnp.int32, (PAIR, PAIR), 1)
    same_head = (ri >> 6) == (ci >> 6)
    strict = same_head & (ri > ci)
    incl = same_head & (ri >= ci)
    eye = ri == ci
    eye_f = jnp.where(eye, 1.0, 0.0)

    units = []
    for j in range(n_chunks):
        rs = slice(CHUNK * j, CHUNK * (j + 1))
        Lj, lwj = L[rs], lw[rs]
        Lc = Lj[CHUNK - 1:CHUNK]
        inv = jnp.exp(-Lj)
        to_end = jnp.exp(Lc - Lj)
        gC = jnp.exp(Lc)
        Rt = r[rs] * jnp.exp(Lj)
        At = -kk[rs] * jnp.exp(Lj - lwj)
        Bt, Kt = kka[rs] * inv, k2[rs] * inv
        Bh, Kh = kka[rs] * to_end, k2[rs] * to_end
        vj = v[rs]
        for pi in range(N_PAIRS):
            sl = slice(LANES * pi, LANES * (pi + 1))

            def stack(x):
                xp = x[:, sl]
                return jnp.concatenate([jnp.where(m1, xp, 0.0), jnp.where(m1, 0.0, xp)], axis=0)

            units.append(dict(
                j=j, pi=pi, gC=gC[:, sl], sRt=stack(Rt),
                sAt=stack(At).astype(BF16), sV=stack(vj).astype(BF16),
                sBt=stack(Bt).astype(BF16), sKt=stack(Kt).astype(BF16),
                sBh=stack(Bh).astype(BF16), sKh=stack(Kh).astype(BF16)))

    for u in units:
        big = _mm_nt(jnp.concatenate([u["sAt"], u["sRt"].astype(BF16)], axis=0),
                     jnp.concatenate([u["sBt"], u["sKt"]], axis=0))
        u["AB"] = jnp.where(strict, big[:PAIR, :PAIR], 0.0)
        u["AK"] = jnp.where(strict, big[:PAIR, PAIR:], 0.0)
        u["RB"] = jnp.where(incl, big[PAIR:, :PAIR], 0.0)
        u["RK"] = jnp.where(incl, big[PAIR:, PAIR:], 0.0)
    for u in units:
        u["X"] = eye_f + u["AB"]
        u["Pw"] = _mm(u["AB"], u["AB"])
        u["W1"] = _mm(u["AK"], u["sV"])
    for _ in range(4):
        for u in units:
            PX = _mm(u["Pw"], jnp.concatenate([u["Pw"], u["X"]], axis=1))
            u["Pw"] = PX[:, :PAIR]
            u["X"] = u["X"] + PX[:, PAIR:]
    for u in units:
        u["Tm"] = u["X"] + _mm(u["Pw"], u["X"])
    for u in units:
        u["PQ"] = _mm(u["Tm"], jnp.concatenate([u["sAt"], u["W1"].astype(BF16)], axis=1)).astype(BF16)
    for u in units:
        PQ = u["PQ"]
        Pm, Q = PQ[:, :PAIR], PQ[:, PAIR:]
        RBPQ = _mm(u["RB"], PQ)
        u["Rp"] = u["sRt"] + RBPQ[:, :PAIR]
        u["Y0"] = RBPQ[:, PAIR:] + _mm(u["RK"], u["sV"])
        u["Mm"] = jnp.where(eye, u["gC"], 0.0) + _mm_tn(u["sBh"], Pm)
        u["G"] = _mm_tn(jnp.concatenate([u["sBh"], u["sKh"]], axis=0),
                        jnp.concatenate([Q, u["sV"]], axis=0))
    H = [h_ref[pi] for pi in range(N_PAIRS)]
    y_rows = []
    for j in range(n_chunks):
        ys = []
        for u in units[j * N_PAIRS:(j + 1) * N_PAIRS]:
            pi = u["pi"]
            YH = _mm(jnp.concatenate([u["Rp"], u["Mm"]], axis=0), H[pi])
            Ysm = YH[:PAIR] + u["Y0"]
            H[pi] = YH[PAIR:] + u["G"]
            ys.append(Ysm[:CHUNK] + Ysm[CHUNK:])
        y_rows.append(jnp.concatenate(ys, axis=1))
    for pi in range(N_PAIRS):
        h_ref[pi] = H[pi]
    y = y_rows[0] if n_chunks == 1 else jnp.concatenate(y_rows, axis=0)

    inv_n = 1.0 / HEAD_DIM
    mean = _dot_exact_rhs(y, bd, 1) * inv_n
    d = y - mean
    var = _dot_exact_rhs(d * d, bd, 1) * inv_n
    yn = d * lax.rsqrt(var + GN_EPS) * ln_w + ln_b
    bonus = _dot_exact_rhs(r * k2 * r_k, bd, 1) * v
    y_ref[0] = ((yn + bonus) * g).astype(BF16)


RWKV_BLOCK_CHUNKS = 4


def _rwkv(ps3, mu, vec, wa, g2, bd):
    B, S, _ = ps3.shape
    rows = RWKV_BLOCK_CHUNKS * CHUNK
    nc = S // rows
    rows8 = rows // 8
    return pl.pallas_call(
        _rwkv_kernel,
        grid=(B, nc),
        in_specs=[pl.BlockSpec((1, rows, SHIFT_COLS), lambda b, c: (b, c, 0)),
                  pl.BlockSpec((1, 8, SHIFT_COLS), lambda b, c: (b, jnp.maximum(c * rows8 - 1, 0), 0)),
                  pl.BlockSpec((1, SHIFT_COLS), lambda b, c: (0, 0)),
                  pl.BlockSpec((8, RWKV_WIDTH), lambda b, c: (0, 0)),
                  pl.BlockSpec((LANES, 2 * RWKV_WIDTH), lambda b, c: (0, 0)),
                  pl.BlockSpec((GATE_LORA, RWKV_WIDTH), lambda b, c: (0, 0)),
                  pl.BlockSpec((RWKV_WIDTH, RWKV_WIDTH), lambda b, c: (0, 0))],
        out_specs=pl.BlockSpec((1, rows, RWKV_WIDTH), lambda b, c: (b, c, 0)),
        out_shape=jax.ShapeDtypeStruct((B, S, RWKV_WIDTH), BF16),
        scratch_shapes=[pltpu.VMEM((N_PAIRS, PAIR, LANES), F32)],
        compiler_params=_cparams(("parallel", "arbitrary")),
        name="rwkv7",
    )(ps3, ps3, mu, vec, wa, g2, bd)


def _bias_kernel(rbt_ref, o_ref):
    e = pl.program_id(1)
    xi = lax.broadcasted_iota(jnp.int32, (REL_ROWS, BIAS_BASE), 1)
    ji = lax.broadcasted_iota(jnp.int32, (REL_ROWS, BIAS_BASE), 0)
    off = jnp.where(xi < BIAS_BASE - CHUNK, xi, xi - BIAS_BASE)
    idx = jnp.clip(e * CHUNK - off, -REL_CLIP, REL_CLIP) + REL_CLIP
    onehot = jnp.where(idx == ji, 1.0, 0.0).astype(BF16)
    base = _dot_exact_rhs(rbt_ref[0], onehot, 3)
    kj = lax.broadcasted_iota(jnp.int32, (CHUNK, ATTN_WINDOW), 1)
    kc = kj >> 6
    valid = (kc <= e) & (kc >= e - N_LEFT_CHUNKS)
    for h in range(8):
        rows = jnp.broadcast_to(base[h:h + 1, :], (CHUNK, BIAS_BASE))
        toep = pltpu.roll(rows, 0, 1, stride=1, stride_axis=0)
        o_ref[0, 0, h * CHUNK:(h + 1) * CHUNK, :] = jnp.where(valid, toep[:, :ATTN_WINDOW], MASK_VALUE)


def _bias_tables(rbt):
    L = rbt.shape[0]
    return pl.pallas_call(
        _bias_kernel,
        grid=(L, N_BIAS_TABLES),
        in_specs=[pl.BlockSpec((1, 8, REL_ROWS), lambda l, e: (l, 0, 0))],
        out_specs=pl.BlockSpec((1, 1, 8 * CHUNK, ATTN_WINDOW), lambda l, e: (l, e, 0, 0)),
        out_shape=jax.ShapeDtypeStruct((L, N_BIAS_TABLES, 8 * CHUNK, ATTN_WINDOW), F32),
        compiler_params=_cparams(("parallel", "parallel")),
        name="bias_tables",
    )(rbt)


ATTN_BLOCK_CHUNKS = 4


def _attn_kernel(q_ref, k_ref, v_ref, *rest):
    bias_refs, g_ref, o_ref = rest[:ATTN_BLOCK_CHUNKS], rest[-2], rest[-1]
    n0 = pl.program_id(1) * ATTN_BLOCK_CHUNKS
    q = q_ref[0] * jnp.asarray(HEAD_DIM ** -0.5, BF16)
    m1 = lax.broadcasted_iota(jnp.int32, (CHUNK, LANES), 1) < HEAD_DIM
    zero = jnp.zeros((), BF16)
    n_pairs = ATTN_WIDTH // LANES
    units = []
    for j in range(ATTN_BLOCK_CHUNKS):
        start = pl.multiple_of(jnp.maximum(n0 + j - (N_LEFT_CHUNKS + 1), 0) * CHUNK, CHUNK)
        kw = k_ref[0, pl.ds(start, ATTN_WINDOW), :]
        vw = v_ref[0, pl.ds(start, ATTN_WINDOW), :]
        qj = q[CHUNK * j:CHUNK * (j + 1)]
        for pi in range(n_pairs):
            sl = slice(LANES * pi, LANES * (pi + 1))
            qp = qj[:, sl]
            qs = jnp.concatenate([jnp.where(m1, qp, zero), jnp.where(m1, zero, qp)], axis=0)
            units.append(dict(qs=qs, k=kw[:, sl], v=vw[:, sl],
                              bias=bias_refs[j][0, 0, PAIR * pi:PAIR * (pi + 1), :]))
    for u in units:
        u["s"] = lax.dot_general(u["qs"], u["k"], (((1,), (1,)), ((), ())),
                                 preferred_element_type=F32) + u["bias"]
    for u in units:
        s = u["s"]
        ex = jnp.exp(s - jnp.max(s, axis=1, keepdims=True))
        u["den"] = jnp.sum(ex, axis=1, keepdims=True)
        u["ex"] = ex.astype(BF16)
    for u in units:
        o = jnp.dot(u["ex"], u["v"], preferred_element_type=F32) / u["den"]
        u["o"] = jnp.where(m1, o[:CHUNK], o[CHUNK:])
    rows = [jnp.concatenate([u["o"] for u in units[j * n_pairs:(j + 1) * n_pairs]], axis=1)
            for j in range(ATTN_BLOCK_CHUNKS)]
    o = jnp.concatenate(rows, axis=0)
    o_ref[0] = _rms(o, g_ref[...]).astype(BF16)


def _attn(qkv3, bias_l, l, g):
    B, S, _ = qkv3.shape
    rows = ATTN_BLOCK_CHUNKS * CHUNK
    nc = S // rows

    def bias_spec(j):
        return pl.BlockSpec(
            (1, 1, 8 * CHUNK, ATTN_WINDOW),
            lambda b, n: (l, jnp.minimum(n * ATTN_BLOCK_CHUNKS + j, N_BIAS_TABLES - 1), 0, 0))

    return pl.pallas_call(
        _attn_kernel,
        grid=(B, nc),
        in_specs=[pl.BlockSpec((1, rows, ATTN_WIDTH), lambda b, n: (b, n, 0)),
                  pl.BlockSpec((1, S, ATTN_WIDTH), lambda b, n: (b, 0, 1)),
                  pl.BlockSpec((1, S, ATTN_WIDTH), lambda b, n: (b, 0, 2))]
                 + [bias_spec(j) for j in range(ATTN_BLOCK_CHUNKS)]
                 + [pl.BlockSpec((1, ATTN_WIDTH), lambda b, n: (0, 0))],
        out_specs=pl.BlockSpec((1, rows, ATTN_WIDTH), lambda b, n: (b, n, 0)),
        out_shape=jax.ShapeDtypeStruct((B, S, ATTN_WIDTH), BF16),
        compiler_params=_cparams(("parallel", "arbitrary")),
        name="chunk_attn",
    )(qkv3, qkv3, qkv3, *([bias_l] * ATTN_BLOCK_CHUNKS), g)


def _outproj_kernel(yr_ref, ya_ref, x_ref, w_ref, g_ref, *rest, with_router):
    if with_router:
        router_ref, xo_ref, h_ref, ridx_ref, rgate_ref = rest
    else:
        xo_ref, h_ref = rest
    y = jnp.concatenate([yr_ref[...], ya_ref[...]], axis=1)
    xn = x_ref[...] + jnp.dot(y, w_ref[...], preferred_element_type=F32)
    xo_ref[...] = xn
    h = _rms(xn, g_ref[...])
    h_ref[...] = h.astype(BF16)
    if not with_router:
        return
    lane = lax.broadcasted_iota(jnp.int32, ridx_ref.shape, 1)
    neg = jnp.asarray(-jnp.inf, F32)
    lg = jnp.where(lane < N_EXPERTS, _dot_f32(h, router_ref[...]), neg)
    top1 = jnp.max(lg, axis=1, keepdims=True)
    idx1 = jnp.min(jnp.where(lg == top1, lane, LANES), axis=1, keepdims=True)
    lg2 = jnp.where(lane == idx1, neg, lg)
    top2 = jnp.max(lg2, axis=1, keepdims=True)
    idx2 = jnp.min(jnp.where(lg2 == top2, lane, LANES), axis=1, keepdims=True)
    ex = jnp.exp(top2 - top1)
    ridx_ref[...] = jnp.where(lane == 0, idx1, jnp.where(lane == 1, idx2, 0))
    rgate_ref[...] = jnp.where(lane == 0, 1.0 / (1.0 + ex), jnp.where(lane == 1, ex / (1.0 + ex), 0.0))


def _outproj(yr, ya, x2, w, g, router, tm):
    T = x2.shape[0]
    with_router = router is not None
    in_specs = [pl.BlockSpec((tm, RWKV_WIDTH), lambda i: (i, 0)),
                pl.BlockSpec((tm, ATTN_WIDTH), lambda i: (i, 0)),
                pl.BlockSpec((tm, D_MODEL), lambda i: (i, 0)),
                pl.BlockSpec((D_MODEL, D_MODEL), lambda i: (0, 0)),
                pl.BlockSpec((1, D_MODEL), lambda i: (0, 0))]
    out_specs = [pl.BlockSpec((tm, D_MODEL), lambda i: (i, 0)),
                 pl.BlockSpec((tm, D_MODEL), lambda i: (i, 0))]
    out_shape = [jax.ShapeDtypeStruct((T, D_MODEL), F32),
                 jax.ShapeDtypeStruct((T, D_MODEL), BF16)]
    args = [yr, ya, x2, w, g]
    if with_router:
        in_specs.append(pl.BlockSpec((D_MODEL, LANES), lambda i: (0, 0)))
        out_specs += [pl.BlockSpec((tm, LANES), lambda i: (i, 0))] * 2
        out_shape += [jax.ShapeDtypeStruct((T, LANES), jnp.int32), jax.ShapeDtypeStruct((T, LANES), F32)]
        args.append(router)
    return pl.pallas_call(
        functools.partial(_outproj_kernel, with_router=with_router),
        grid=(T // tm,),
        in_specs=in_specs, out_specs=out_specs, out_shape=out_shape,
        compiler_params=_cparams(("parallel",)),
        name="outproj_router" if with_router else "outproj",
    )(*args)


FF_TILE = 256


def _ffn_kernel(h_ref, x_ref, wg_ref, wu_ref, wd_ref, *rest, final):
    if final:
        gf_ref, o_ref = rest
    else:
        (o_ref,) = rest
    h = h_ref[...]
    acc = None
    for f in range(0, wg_ref.shape[1], FF_TILE):
        gate = jnp.dot(h, wg_ref[:, f:f + FF_TILE], preferred_element_type=F32)
        up = jnp.dot(h, wu_ref[:, f:f + FF_TILE], preferred_element_type=F32)
        act = (gate * _sigmoid(gate) * up).astype(BF16)
        d = jnp.dot(act, wd_ref[f:f + FF_TILE, :], preferred_element_type=F32)
        acc = d if acc is None else acc + d
    xo = x_ref[...] + acc
    o_ref[...] = _rms(xo, gf_ref[...]) if final else xo


def _ffn(h, x2, wg, wu, wd, final_g, tm):
    T = x2.shape[0]
    F = wg.shape[1]
    final = final_g is not None
    in_specs = [pl.BlockSpec((tm, D_MODEL), lambda i: (i, 0)),
                pl.BlockSpec((tm, D_MODEL), lambda i: (i, 0)),
                pl.BlockSpec((D_MODEL, F), lambda i: (0, 0)),
                pl.BlockSpec((D_MODEL, F), lambda i: (0, 0)),
                pl.BlockSpec((F, D_MODEL), lambda i: (0, 0))]
    args = [h, x2, wg, wu, wd]
    if final:
        in_specs.append(pl.BlockSpec((1, D_MODEL), lambda i: (0, 0)))
        args.append(final_g)
    return pl.pallas_call(
        functools.partial(_ffn_kernel, final=final),
        grid=(T // tm,),
        in_specs=in_specs,
        out_specs=pl.BlockSpec((tm, D_MODEL), lambda i: (i, 0)),
        out_shape=jax.ShapeDtypeStruct((T, D_MODEL), F32),
        compiler_params=_cparams(("parallel",)),
        name="ffn_dense",
    )(*args)


MOE_TILE = 256
MOE_TOK_TILE = 256
MOE_LOCAL_ROWS = 640
SEG_PIECES = (256, 128, 64, 32, 16, 8)
TOP_K = 2


def _moe_offsets(ridx, T):
    nt = T // MOE_TOK_TILE
    e12 = ridx[:, :TOP_K]
    onehot = (e12[:, :, None] == jnp.arange(N_EXPERTS, dtype=jnp.int32)[None, None, :]).astype(jnp.int32)
    cnt = onehot.sum(axis=1).reshape(nt, MOE_TOK_TILE, N_EXPERTS).sum(axis=1)
    pc = ((cnt + 7) // 8) * 8
    loff = jnp.cumsum(pc, axis=1) - pc
    tot = pc.sum(axis=0)
    grp = ((tot + MOE_TILE - 1) // MOE_TILE) * MOE_TILE
    gend = jnp.cumsum(grp)
    gstart = gend - grp
    goff = gstart[None, :] + jnp.cumsum(pc, axis=0) - pc
    n_tiles = -(-(TOP_K * T + 8 * (N_EXPERTS - 1) * nt) // MOE_TILE) + N_EXPERTS
    tile_start = jnp.arange(n_tiles, dtype=jnp.int32) * MOE_TILE
    tile_exp = jnp.minimum(jnp.sum((tile_start[:, None] >= gend[None, :]).astype(jnp.int32), axis=1),
                           N_EXPERTS - 1)
    tile_rows = jnp.clip((gstart + tot)[tile_exp] - tile_start, 0, MOE_TILE)
    flat = lambda t: t.reshape(-1).astype(jnp.int32)
    used_tiles = gend[-1] // MOE_TILE
    gaps = jnp.concatenate([gstart + tot, grp - tot, jnp.stack([used_tiles, n_tiles - used_tiles])])
    return (flat(loff), flat(goff), flat(pc), flat(gaps), tile_exp.astype(jnp.int32),
            tile_rows.astype(jnp.int32), n_tiles)


def _local_positions(ridx, loff_ref, j):
    rows = ridx.shape[0]
    lane = lax.broadcasted_iota(jnp.int32, (rows, LANES), 1)
    oh0 = lane == ridx[:, 0:1]
    oh1 = lane == ridx[:, 1:2]
    ti = lax.broadcasted_iota(jnp.int32, (rows, rows), 0)
    tj = lax.broadcasted_iota(jnp.int32, (rows, rows), 1)
    before = jnp.where(tj < ti, 1.0, 0.0).astype(BF16)
    f0 = jnp.where(oh0, 1.0, 0.0)
    f1 = jnp.where(oh1, 1.0, 0.0)
    pre0 = jnp.dot(before, f0.astype(BF16), preferred_element_type=F32)
    pre1 = jnp.dot(before, f1.astype(BF16), preferred_element_type=F32)
    c0 = jnp.sum(f0, axis=0, keepdims=True)
    lane1 = lax.broadcasted_iota(jnp.int32, (1, LANES), 1)
    loff = jnp.zeros((1, LANES), F32)
    for e in range(N_EXPERTS):
        loff = jnp.where(lane1 == e, loff_ref[j * N_EXPERTS + e].astype(F32), loff)
    pos0 = jnp.sum(jnp.where(oh0, loff + pre0, 0.0), axis=1, keepdims=True)
    pos1 = jnp.sum(jnp.where(oh1, loff + c0 + pre1, 0.0), axis=1, keepdims=True)
    return pos0, pos1


def _segment_copies(j, pc_ref, loff_ref, goff_ref, local_ref, hbm_ref, sem, to_hbm):
    out = []
    for e in range(N_EXPERTS):
        n = pc_ref[j * N_EXPERTS + e]
        lo = loff_ref[j * N_EXPERTS + e]
        go = goff_ref[j * N_EXPERTS + e]
        for b in SEG_PIECES:
            done = n & ~(2 * b - 1)
            loc = local_ref.at[pl.ds(pl.multiple_of(lo + done, 8), b), :]
            glob = hbm_ref.at[pl.ds(pl.multiple_of(go + done, 8), b), :]
            cp = pltpu.make_async_copy(loc, glob, sem) if to_hbm else pltpu.make_async_copy(glob, loc, sem)
            out.append(((n & b) != 0, cp))
    return out


def _start_segments(*args):
    for cond, cp in _segment_copies(*args):
        pl.when(cond)(cp.start)


def _wait_segments(*args):
    for cond, cp in _segment_copies(*args):
        pl.when(cond)(cp.wait)


def _gap_copies(gaps_ref, zero_ref, xs_hbm, sem):
    out = []
    for e in range(N_EXPERTS):
        start = gaps_ref[e]
        n = gaps_ref[N_EXPERTS + e]
        for b in SEG_PIECES[1:]:
            done = n & ~(2 * b - 1)
            dst = xs_hbm.at[pl.ds(pl.multiple_of(start + done, 8), b), :]
            out.append(((n & b) != 0, pltpu.make_async_copy(zero_ref.at[pl.ds(0, b), :], dst, sem)))
    return out


def _zero_fill_gaps(gaps_ref, zero_ref, xs_hbm, sem):
    zero_ref[...] = jnp.zeros(zero_ref.shape, F32)
    first_tile = gaps_ref[2 * N_EXPERTS]
    n_tail = gaps_ref[2 * N_EXPERTS + 1]

    def tail_copy(i):
        row0 = pl.multiple_of((first_tile + i) * MOE_TILE, MOE_TILE)
        return pltpu.make_async_copy(zero_ref, xs_hbm.at[pl.ds(row0, MOE_TILE), :], sem)

    for cond, cp in _gap_copies(gaps_ref, zero_ref, xs_hbm, sem):
        pl.when(cond)(cp.start)
    lax.fori_loop(0, n_tail, lambda i, c: (tail_copy(i).start(), c)[1], 0)
    for cond, cp in _gap_copies(gaps_ref, zero_ref, xs_hbm, sem):
        pl.when(cond)(cp.wait)
    lax.fori_loop(0, n_tail, lambda i, c: (tail_copy(i).wait(), c)[1], 0)


def _moe_dispatch_kernel(loff_ref, goff_ref, pc_ref, gaps_ref, ridx_ref, h_ref, xs_hbm, local_ref, zero_ref,
                         sem, zsem):
    j = pl.program_id(0)
    n = pl.num_programs(0)
    slot = j % 2

    @pl.when(j == 0)
    def _gaps():
        _zero_fill_gaps(gaps_ref, zero_ref, xs_hbm, zsem.at[0])

    pos0, pos1 = _local_positions(ridx_ref[...], loff_ref, j)
    col = lax.broadcasted_iota(jnp.int32, (MOE_TOK_TILE, MOE_LOCAL_ROWS), 1).astype(F32)
    place = jnp.where((col == pos0) | (col == pos1), 1.0, 0.0).astype(BF16)
    xs = lax.dot_general(place, h_ref[...], (((0,), (0,)), ((), ())), preferred_element_type=F32)

    def seg(jj, s):
        return (jj, pc_ref, loff_ref, goff_ref, local_ref.at[s], xs_hbm, sem.at[s], True)

    @pl.when(j >= 2)
    def _reuse():
        _wait_segments(*seg(j - 2, slot))

    local_ref[slot] = xs
    _start_segments(*seg(j, slot))

    @pl.when(j == n - 1)
    def _drain():
        _wait_segments(*seg(j, slot))

        @pl.when(n >= 2)
        def _():
            _wait_segments(*seg(j - 1, 1 - slot))


def _moe_dispatch(ridx, h, loff, goff, pc, gaps, n_rows):
    T = h.shape[0]
    grid_spec = pltpu.PrefetchScalarGridSpec(
        num_scalar_prefetch=4,
        grid=(T // MOE_TOK_TILE,),
        in_specs=[pl.BlockSpec((MOE_TOK_TILE, LANES), lambda j, *_: (j, 0)),
                  pl.BlockSpec((MOE_TOK_TILE, D_MODEL), lambda j, *_: (j, 0))],
        out_specs=pl.BlockSpec(memory_space=pl.ANY),
        scratch_shapes=[pltpu.VMEM((2, MOE_LOCAL_ROWS, D_MODEL), F32), pltpu.VMEM((MOE_TILE, D_MODEL), F32),
                        pltpu.SemaphoreType.DMA((2,)), pltpu.SemaphoreType.DMA((1,))])
    return pl.pallas_call(
        _moe_dispatch_kernel,
        grid_spec=grid_spec,
        out_shape=jax.ShapeDtypeStruct((n_rows, D_MODEL), F32),
        compiler_params=_cparams(("arbitrary",)),
        name="moe_dispatch",
    )(loff, goff, pc, gaps, ridx, h)


def _moe_expert_kernel(texp_ref, trows_ref, xs_ref, wg_ref, wu_ref, wd_ref, ys_ref):
    del texp_ref
    rows = trows_ref[pl.program_id(0)]

    @pl.when(rows > 0)
    def _compute():
        x = xs_ref[...].astype(BF16)
        gate = jnp.dot(x, wg_ref[0], preferred_element_type=F32)
        up = jnp.dot(x, wu_ref[0], preferred_element_type=F32)
        act = (gate * _sigmoid(gate) * up).astype(BF16)
        ys_ref[...] = jnp.dot(act, wd_ref[0], preferred_element_type=F32)

    @pl.when(rows == 0)
    def _skip():
        ys_ref[...] = jnp.zeros(ys_ref.shape, F32)


def _moe_experts(xs, tile_exp, tile_rows, wg, wu, wd):
    n_tiles = tile_exp.shape[0]
    _, _, F = wg.shape
    grid_spec = pltpu.PrefetchScalarGridSpec(
        num_scalar_prefetch=2,
        grid=(n_tiles,),
        in_specs=[pl.BlockSpec((MOE_TILE, D_MODEL), lambda i, te, tr: (i, 0)),
                  pl.BlockSpec((1, D_MODEL, F), lambda i, te, tr: (te[i], 0, 0)),
                  pl.BlockSpec((1, D_MODEL, F), lambda i, te, tr: (te[i], 0, 0)),
                  pl.BlockSpec((1, F, D_MODEL), lambda i, te, tr: (te[i], 0, 0))],
        out_specs=pl.BlockSpec((MOE_TILE, D_MODEL), lambda i, te, tr: (i, 0)))
    return pl.pallas_call(
        _moe_expert_kernel,
        grid_spec=grid_spec,
        out_shape=jax.ShapeDtypeStruct((n_tiles * MOE_TILE, D_MODEL), F32),
        compiler_params=_cparams(("arbitrary",)),
        name="moe_experts",
    )(tile_exp, tile_rows, xs, wg, wu, wd)


def _moe_combine_kernel(loff_ref, goff_ref, pc_ref, ridx_ref, rgate_ref, x_ref, ys_hbm, *rest, final):
    if final:
        gf_ref, o_ref, local_ref, sem = rest
    else:
        o_ref, local_ref, sem = rest
    j = pl.program_id(0)
    n = pl.num_programs(0)
    slot = j % 2

    def seg(jj, s):
        return (jj, pc_ref, loff_ref, goff_ref, local_ref.at[s], ys_hbm, sem.at[s], False)

    @pl.when(j == 0)
    def _prologue():
        _start_segments(*seg(j, slot))

    @pl.when(j + 1 < n)
    def _prefetch():
        _start_segments(*seg(j + 1, 1 - slot))

    pos0, pos1 = _local_positions(ridx_ref[...], loff_ref, j)
    col = lax.broadcasted_iota(jnp.int32, (MOE_TOK_TILE, MOE_LOCAL_ROWS), 1).astype(F32)
    pick0 = jnp.where(col == pos0, 1.0, 0.0).astype(BF16)
    pick1 = jnp.where(col == pos1, 1.0, 0.0).astype(BF16)
    used = loff_ref[j * N_EXPERTS + N_EXPERTS - 1] + pc_ref[j * N_EXPERTS + N_EXPERTS - 1]
    _wait_segments(*seg(j, slot))
    ri = lax.broadcasted_iota(jnp.int32, (MOE_LOCAL_ROWS, D_MODEL), 0)
    ys = jnp.where(ri < used, local_ref[slot], 0.0).astype(BF16)
    y0 = jnp.dot(pick0, ys, preferred_element_type=F32)
    y1 = jnp.dot(pick1, ys, preferred_element_type=F32)
    gates = rgate_ref[...]
    xo = x_ref[...] + gates[:, 0:1] * y0 + gates[:, 1:2] * y1
    o_ref[...] = _rms(xo, gf_ref[...]) if final else xo


def _moe_combine(x2, ys, ridx, rgate, loff, goff, pc, final_g):
    T = x2.shape[0]
    final = final_g is not None
    in_specs = [pl.BlockSpec((MOE_TOK_TILE, LANES), lambda j, *_: (j, 0)),
                pl.BlockSpec((MOE_TOK_TILE, LANES), lambda j, *_: (j, 0)),
                pl.BlockSpec((MOE_TOK_TILE, D_MODEL), lambda j, *_: (j, 0)),
                pl.BlockSpec(memory_space=pl.ANY)]
    args = [ridx, rgate, x2, ys]
    if final:
        in_specs.append(pl.BlockSpec((1, D_MODEL), lambda j, *_: (0, 0)))
        args.append(final_g)
    grid_spec = pltpu.PrefetchScalarGridSpec(
        num_scalar_prefetch=3,
        grid=(T // MOE_TOK_TILE,),
        in_specs=in_specs,
        out_specs=pl.BlockSpec((MOE_TOK_TILE, D_MODEL), lambda j, *_: (j, 0)),
        scratch_shapes=[pltpu.VMEM((2, MOE_LOCAL_ROWS, D_MODEL), F32), pltpu.SemaphoreType.DMA((2,))])
    return pl.pallas_call(
        functools.partial(_moe_combine_kernel, final=final),
        grid_spec=grid_spec,
        out_shape=jax.ShapeDtypeStruct((T, D_MODEL), F32),
        compiler_params=_cparams(("arbitrary",)),
        name="moe_combine",
    )(loff, goff, pc, *args)


def _moe(h, x2, ridx, rgate, wg, wu, wd, final_g):
    T = x2.shape[0]
    loff, goff, pc, gaps, tile_exp, tile_rows, n_tiles = _moe_offsets(ridx, T)
    xs = _moe_dispatch(ridx, h, loff, goff, pc, gaps, n_tiles * MOE_TILE)
    ys = _moe_experts(xs, tile_exp, tile_rows, wg, wu, wd)
    return _moe_combine(x2, ys, ridx, rgate, loff, goff, pc, final_g)


def kernel(x, norm_mix_g, w_in, shift_mu, rwkv_w0, rwkv_w2, rwkv_a0, rwkv_a2, rwkv_g2, rwkv_k_k, rwkv_k_a, rwkv_r_k, rwkv_ln_w, rwkv_ln_b, attn_rel_bias, attn_norm_g, w_out, norm_ffn_g, ffn_w_gate, ffn_w_up, ffn_w_down, moe_router, moe_w_gate, moe_w_up, moe_w_down, norm_final_g):
    B, S, D = x.shape
    depth = w_in.shape[0]
    T = B * S
    tm = min(512, T)
    row = lambda t: t.reshape(1, -1).astype(F32)

    hi = jnp.arange(RWKV_WIDTH) // HEAD_DIM
    bd = (hi[:, None] == hi[None, :]).astype(BF16)
    rbt = jnp.pad(jnp.swapaxes(attn_rel_bias, 1, 2).astype(F32),
                  ((0, 0), (0, 0), (0, REL_ROWS - attn_rel_bias.shape[1])))
    bias_tabs = _bias_tables(rbt)

    x2 = x.reshape(T, D).astype(F32)
    for l in range(depth):
        w_shift = w_in[l, :, :SHIFT_COLS].astype(BF16)
        w_attn = w_in[l, :, SHIFT_COLS:].astype(BF16)
        ps, qkv = _inproj(x2, row(norm_mix_g[l]), w_shift, w_attn, tm)

        zeros = jnp.zeros((DECAY_LORA, RWKV_WIDTH), F32)
        wa = jnp.concatenate([jnp.concatenate([rwkv_w2[l], zeros], axis=1),
                              jnp.concatenate([zeros, rwkv_a2[l]], axis=1)], axis=0).astype(BF16)
        vec = jnp.stack([rwkv_w0[l], rwkv_a0[l], rwkv_k_k[l], rwkv_k_a[l], rwkv_r_k[l],
                         rwkv_ln_w[l], rwkv_ln_b[l], jnp.zeros_like(rwkv_w0[l])]).astype(F32)
        y_rwkv = _rwkv(ps.reshape(B, S, SHIFT_COLS), row(shift_mu[l]), vec, wa,
                       rwkv_g2[l].astype(BF16), bd)
        y_attn = _attn(qkv.reshape(B, S, 3 * ATTN_WIDTH), bias_tabs, l, row(attn_norm_g[l]))

        is_moe = l % 2 == 1
        li = l // 2
        router = None
        if is_moe:
            router = jnp.pad(moe_router[li].astype(F32), ((0, 0), (0, LANES - N_EXPERTS)))
        outs = _outproj(y_rwkv.reshape(T, RWKV_WIDTH), y_attn.reshape(T, ATTN_WIDTH), x2,
                        w_out[l].astype(BF16), row(norm_ffn_g[l]), router, tm)
        final_g = row(norm_final_g) if l == depth - 1 else None
        if is_moe:
            x_mid, h, ridx, rgate = outs
            x2 = _moe(h, x_mid, ridx, rgate, moe_w_gate[li].astype(BF16), moe_w_up[li].astype(BF16),
                      moe_w_down[li].astype(BF16), final_g)
        else:
            x_mid, h = outs
            x2 = _ffn(h, x_mid, ffn_w_gate[li].astype(BF16), ffn_w_up[li].astype(BF16),
                      ffn_w_down[li].astype(BF16), final_g, tm)
    return x2.reshape(B, S, D).astype(x.dtype)
```

```python
import functools

import jax
import jax.numpy as jnp
from jax import lax
from jax.experimental import pallas as pl
from jax.experimental.pallas import tpu as pltpu

F32 = jnp.float32
BF16 = jnp.bfloat16

D_MODEL = 1024
CHUNK = 64
N_LEFT_CHUNKS = 8
HEAD_DIM = 64
RWKV_WIDTH = 512
ATTN_WIDTH = 512
DECAY_LORA = 64
AAA_LORA = 64
GATE_LORA = 128
REL_CLIP = 128
N_EXPERTS = 8
RMS_EPS = 1e-6
GN_EPS = 64e-5
MASK_VALUE = -1e30
SHIFT_COLS = 3 * RWKV_WIDTH + DECAY_LORA + AAA_LORA + GATE_LORA

LANES = 128
PAIR = 2 * CHUNK
N_PAIRS = RWKV_WIDTH // LANES
ATTN_WINDOW = (N_LEFT_CHUNKS + 2) * CHUNK
N_BIAS_TABLES = N_LEFT_CHUNKS + 2
BIAS_BASE = 768
REL_ROWS = 384
VMEM_LIMIT = 56 * 1024 * 1024


def _cparams(sem):
    return pltpu.CompilerParams(dimension_semantics=sem, vmem_limit_bytes=VMEM_LIMIT)


def _mm(a, b):
    return jnp.dot(a.astype(BF16), b.astype(BF16), preferred_element_type=F32)


def _mm_nt(a, b):
    return lax.dot_general(a.astype(BF16), b.astype(BF16), (((1,), (1,)), ((), ())),
                           preferred_element_type=F32)


def _mm_tn(a, b):
    return lax.dot_general(a.astype(BF16), b.astype(BF16), (((0,), (0,)), ((), ())),
                           preferred_element_type=F32)


def _split_terms(x, n):
    terms, rem = [], x
    for _ in range(n):
        hi = rem.astype(BF16)
        terms.append(hi)
        rem = rem - hi.astype(F32)
    return terms


def _dot_exact_rhs(x, w_bf16, n):
    acc = None
    for t in _split_terms(x, n):
        d = jnp.dot(t, w_bf16, preferred_element_type=F32)
        acc = d if acc is None else acc + d
    return acc


def _dot_exact_lhs(w_bf16, x, n):
    acc = None
    for t in _split_terms(x, n):
        d = jnp.dot(w_bf16, t, preferred_element_type=F32)
        acc = d if acc is None else acc + d
    return acc


def _dot_f32(a, b):
    a1, a2 = _split_terms(a, 2)
    b1, b2 = _split_terms(b, 2)
    n = b.shape[1]
    t = jnp.dot(a1, jnp.concatenate([b1, b2], axis=1), preferred_element_type=F32)
    return t[:, :n] + t[:, n:] + jnp.dot(a2, b1, preferred_element_type=F32)


def _sigmoid(x):
    return 1.0 / (1.0 + jnp.exp(-x))


def _rms(x, g):
    return x * lax.rsqrt(jnp.mean(x * x, axis=-1, keepdims=True) + RMS_EPS) * g


def _inproj_kernel(x_ref, g_ref, ws_ref, wa_ref, ps_ref, qkv_ref):
    hb = _rms(x_ref[...], g_ref[...]).astype(BF16)
    ps_ref[...] = jnp.dot(hb, ws_ref[...], preferred_element_type=F32)
    qkv_ref[...] = jnp.dot(hb, wa_ref[...], preferred_element_type=F32).astype(BF16)


def _inproj(x2, g, w_shift, w_attn, tm):
    T = x2.shape[0]
    ns, na = w_shift.shape[1], w_attn.shape[1]
    return pl.pallas_call(
        _inproj_kernel,
        grid=(T // tm,),
        in_specs=[pl.BlockSpec((tm, D_MODEL), lambda i: (i, 0)),
                  pl.BlockSpec((1, D_MODEL), lambda i: (0, 0)),
                  pl.BlockSpec((D_MODEL, ns), lambda i: (0, 0)),
                  pl.BlockSpec((D_MODEL, na), lambda i: (0, 0))],
        out_specs=[pl.BlockSpec((tm, ns), lambda i: (i, 0)),
                   pl.BlockSpec((tm, na), lambda i: (i, 0))],
        out_shape=[jax.ShapeDtypeStruct((T, ns), F32),
                   jax.ShapeDtypeStruct((T, na), BF16)],
        compiler_params=_cparams(("parallel",)),
        name="inproj",
    )(x2, g, w_shift, w_attn)


def _rwkv_kernel(ps_ref, prev_ref, mu_ref, vec_ref, wa_ref, g2_ref, bd_ref, y_ref, h_ref):
    c = pl.program_id(1)

    @pl.when(c == 0)
    def _init():
        h_ref[...] = jnp.zeros(h_ref.shape, F32)

    p = ps_ref[0]
    last = jnp.where(c > 0, prev_ref[0][7:8, :], 0.0)
    row = lax.broadcasted_iota(jnp.int32, p.shape, 0)
    prev = jnp.where(row == 0, last, pltpu.roll(p, 1, 0))
    xs = p + mu_ref[...] * (prev - p)

    W = RWKV_WIDTH
    r, k, v = xs[:, 0:W], xs[:, W:2 * W], xs[:, 2 * W:3 * W]
    z0 = xs[:, 3 * W:3 * W + LANES]
    gd = xs[:, 3 * W + LANES:3 * W + 2 * LANES]
    m1 = lax.broadcasted_iota(jnp.int32, (CHUNK, LANES), 1) < HEAD_DIM
    z0 = jnp.where(lax.broadcasted_iota(jnp.int32, z0.shape, 1) < DECAY_LORA, jnp.tanh(z0), z0)
    lora = _mm(z0, wa_ref[...])
    vec = vec_ref[...]
    w0, a0, k_k, k_a, r_k, ln_w, ln_b = (vec[i:i + 1] for i in range(7))
    w = w0 + lora[:, :W]
    a = _sigmoid(a0 + lora[:, W:])
    g = _mm(_sigmoid(gd), g2_ref[...])
    softplus_neg_w = jnp.maximum(-w, 0.0) + jnp.log(1.0 + jnp.exp(-jnp.abs(w)))
    lw = -jnp.exp(-softplus_neg_w - 0.5)

    bd = bd_ref[...]

    def head_sums(x):
        xb = x.astype(BF16)
        return jnp.concatenate(
            [jnp.dot(xb[:, LANES * i:LANES * (i + 1)], bd, preferred_element_type=F32) for i in range(N_PAIRS)],
            axis=1)

    kk = k * k_k
    kk = kk / jnp.maximum(jnp.sqrt(head_sums(kk * kk)), 1e-12)
    k2 = k * (1.0 + (a - 1.0) * k_a)
    kka = kk * a

    rows = p.shape[0]
    n_chunks = rows // CHUNK
    ti = lax.broadcasted_iota(jnp.int32, (rows, rows), 0)
    tj = lax.broadcasted_iota(jnp.int32, (rows, rows), 1)
    tri = jnp.where((ti >= tj) & ((ti >> 6) == (tj >> 6)), 1.0, 0.0).astype(BF16)
    L = _dot_exact_lhs(tri, lw, 2)

    ri = lax.broadcasted_iota(jnp.int32, (PAIR, PAIR), 0)
    ci = lax.broadcasted_iota(jnp.int32, (PAIR, PAIR), 1)
    same_head = (ri >> 6) == (ci >> 6)
    strict = same_head & (ri > ci)
    incl = same_head & (ri >= ci)
    eye = ri == ci
    eye_f = jnp.where(eye, 1.0, 0.0)

    units = []
    for j in range(n_chunks):
        rs = slice(CHUNK * j, CHUNK * (j + 1))
        Lj, lwj = L[rs], lw[rs]
        Lc = Lj[CHUNK - 1:CHUNK]
        inv = jnp.exp(-Lj)
        to_end = jnp.exp(Lc - Lj)
        gC = jnp.exp(Lc)
        Rt = r[rs] * jnp.exp(Lj)
        At = -kk[rs] * jnp.exp(Lj - lwj)
        Bt, Kt = kka[rs] * inv, k2[rs] * inv
        Bh, Kh = kka[rs] * to_end, k2[rs] * to_end
        vj = v[rs]
        for pi in range(N_PAIRS):
            sl = slice(LANES * pi, LANES * (pi + 1))

            def stack(x):
                xp = x[:, sl]
                return jnp.concatenate([jnp.where(m1, xp, 0.0), jnp.where(m1, 0.0, xp)], axis=0)

            units.append(dict(
                j=j, pi=pi, gC=gC[:, sl], sRt=stack(Rt),
                sAt=stack(At).astype(BF16), sV=stack(vj).astype(BF16),
                sBt=stack(Bt).astype(BF16), sKt=stack(Kt).astype(BF16),
                sBh=stack(Bh).astype(BF16), sKh=stack(Kh).astype(BF16)))

    for u in units:
        big = _mm_nt(jnp.concatenate([u["sAt"], u["sRt"].astype(BF16)], axis=0),
                     jnp.concatenate([u["sBt"], u["sKt"]], axis=0))
        u["AB"] = jnp.where(strict, big[:PAIR, :PAIR], 0.0)
        u["AK"] = jnp.where(strict, big[:PAIR, PAIR:], 0.0)
        u["RB"] = jnp.where(incl, big[PAIR:, :PAIR], 0.0)
        u["RK"] = jnp.where(incl, big[PAIR:, PAIR:], 0.0)
    for u in units:
        u["X"] = eye_f + u["AB"]
        u["Pw"] = _mm(u["AB"], u["AB"])
        u["W1"] = _mm(u["AK"], u["sV"])
    for _ in range(4):
        for u in units:
            PX = _mm(u["Pw"], jnp.concatenate([u["Pw"], u["X"]], axis=1))
            u["Pw"] = PX[:, :PAIR]
            u["X"] = u["X"] + PX[:, PAIR:]
    for u in units:
        u["Tm"] = u["X"] + _mm(u["Pw"], u["X"])
    for u in units:
        u["PQ"] = _mm(u["Tm"], jnp.concatenate([u["sAt"], u["W1"].astype(BF16)], axis=1)).astype(BF16)
    for u in units:
        PQ = u["PQ"]
        Pm, Q = PQ[:, :PAIR], PQ[:, PAIR:]
        RBPQ = _mm(u["RB"], PQ)
        u["Rp"] = u["sRt"] + RBPQ[:, :PAIR]
        u["Y0"] = RBPQ[:, PAIR:] + _mm(u["RK"], u["sV"])
        u["Mm"] = jnp.where(eye, u["gC"], 0.0) + _mm_tn(u["sBh"], Pm)
        u["G"] = _mm_tn(jnp.concatenate([u["sBh"], u["sKh"]], axis=0),
                        jnp.concatenate([Q, u["sV"]], axis=0))
    H = [h_ref[pi] for pi in range(N_PAIRS)]
    y_rows = []
    for j in range(n_chunks):
        ys = []
        for u in units[j * N_PAIRS:(j + 1) * N_PAIRS]:
            pi = u["pi"]
            YH = _mm(jnp.concatenate([u["Rp"], u["Mm"]], axis=0), H[pi])
            Ysm = YH[:PAIR] + u["Y0"]
            H[pi] = YH[PAIR:] + u["G"]
            ys.append(Ysm[:CHUNK] + Ysm[CHUNK:])
        y_rows.append(jnp.concatenate(ys, axis=1))
    for pi in range(N_PAIRS):
        h_ref[pi] = H[pi]
    y = y_rows[0] if n_chunks == 1 else jnp.concatenate(y_rows, axis=0)

    inv_n = 1.0 / HEAD_DIM
    mean = head_sums(y) * inv_n
    d = y - mean
    var = head_sums(d * d) * inv_n
    yn = d * lax.rsqrt(var + GN_EPS) * ln_w + ln_b
    bonus = head_sums(r * k2 * r_k) * v
    y_ref[0] = ((yn + bonus) * g).astype(BF16)


RWKV_BLOCK_CHUNKS = 4


def _rwkv(ps3, mu, vec, wa, g2, bd):
    B, S, _ = ps3.shape
    rows = RWKV_BLOCK_CHUNKS * CHUNK
    nc = S // rows
    rows8 = rows // 8
    return pl.pallas_call(
        _rwkv_kernel,
        grid=(B, nc),
        in_specs=[pl.BlockSpec((1, rows, SHIFT_COLS), lambda b, c: (b, c, 0)),
                  pl.BlockSpec((1, 8, SHIFT_COLS), lambda b, c: (b, jnp.maximum(c * rows8 - 1, 0), 0)),
                  pl.BlockSpec((1, SHIFT_COLS), lambda b, c: (0, 0)),
                  pl.BlockSpec((8, RWKV_WIDTH), lambda b, c: (0, 0)),
                  pl.BlockSpec((LANES, 2 * RWKV_WIDTH), lambda b, c: (0, 0)),
                  pl.BlockSpec((GATE_LORA, RWKV_WIDTH), lambda b, c: (0, 0)),
                  pl.BlockSpec((LANES, LANES), lambda b, c: (0, 0))],
        out_specs=pl.BlockSpec((1, rows, RWKV_WIDTH), lambda b, c: (b, c, 0)),
        out_shape=jax.ShapeDtypeStruct((B, S, RWKV_WIDTH), BF16),
        scratch_shapes=[pltpu.VMEM((N_PAIRS, PAIR, LANES), F32)],
        compiler_params=_cparams(("parallel", "arbitrary")),
        name="rwkv7",
    )(ps3, ps3, mu, vec, wa, g2, bd)


def _bias_kernel(rbt_ref, o_ref):
    e = pl.program_id(1)
    xi = lax.broadcasted_iota(jnp.int32, (REL_ROWS, BIAS_BASE), 1)
    ji = lax.broadcasted_iota(jnp.int32, (REL_ROWS, BIAS_BASE), 0)
    off = jnp.where(xi < BIAS_BASE - CHUNK, xi, xi - BIAS_BASE)
    idx = jnp.clip(e * CHUNK - off, -REL_CLIP, REL_CLIP) + REL_CLIP
    onehot = jnp.where(idx == ji, 1.0, 0.0).astype(BF16)
    base = _dot_exact_rhs(rbt_ref[0], onehot, 3)
    kj = lax.broadcasted_iota(jnp.int32, (CHUNK, ATTN_WINDOW), 1)
    kc = kj >> 6
    valid = (kc <= e) & (kc >= e - N_LEFT_CHUNKS)
    for h in range(8):
        rows = jnp.broadcast_to(base[h:h + 1, :], (CHUNK, BIAS_BASE))
        toep = pltpu.roll(rows, 0, 1, stride=1, stride_axis=0)
        o_ref[0, 0, h * CHUNK:(h + 1) * CHUNK, :] = jnp.where(valid, toep[:, :ATTN_WINDOW], MASK_VALUE)


def _bias_tables(rbt):
    L = rbt.shape[0]
    return pl.pallas_call(
        _bias_kernel,
        grid=(L, N_BIAS_TABLES),
        in_specs=[pl.BlockSpec((1, 8, REL_ROWS), lambda l, e: (l, 0, 0))],
        out_specs=pl.BlockSpec((1, 1, 8 * CHUNK, ATTN_WINDOW), lambda l, e: (l, e, 0, 0)),
        out_shape=jax.ShapeDtypeStruct((L, N_BIAS_TABLES, 8 * CHUNK, ATTN_WINDOW), F32),
        compiler_params=_cparams(("parallel", "parallel")),
        name="bias_tables",
    )(rbt)


ATTN_BLOCK_CHUNKS = 4


def _attn_kernel(q_ref, k_ref, v_ref, *rest):
    bias_refs, g_ref, o_ref = rest[:ATTN_BLOCK_CHUNKS], rest[-2], rest[-1]
    n0 = pl.program_id(1) * ATTN_BLOCK_CHUNKS
    q = q_ref[0] * jnp.asarray(HEAD_DIM ** -0.5, BF16)
    m1 = lax.broadcasted_iota(jnp.int32, (CHUNK, LANES), 1) < HEAD_DIM
    zero = jnp.zeros((), BF16)
    n_pairs = ATTN_WIDTH // LANES
    units = []
    for j in range(ATTN_BLOCK_CHUNKS):
        start = pl.multiple_of(jnp.maximum(n0 + j - (N_LEFT_CHUNKS + 1), 0) * CHUNK, CHUNK)
        kw = k_ref[0, pl.ds(start, ATTN_WINDOW), :]
        vw = v_ref[0, pl.ds(start, ATTN_WINDOW), :]
        qj = q[CHUNK * j:CHUNK * (j + 1)]
        for pi in range(n_pairs):
            sl = slice(LANES * pi, LANES * (pi + 1))
            qp = qj[:, sl]
            qs = jnp.concatenate([jnp.where(m1, qp, zero), jnp.where(m1, zero, qp)], axis=0)
            units.append(dict(qs=qs, k=kw[:, sl], v=vw[:, sl],
                              bias=bias_refs[j][0, 0, PAIR * pi:PAIR * (pi + 1), :]))
    for u in units:
        u["s"] = lax.dot_general(u["qs"], u["k"], (((1,), (1,)), ((), ())),
                                 preferred_element_type=F32) + u["bias"]
    for u in units:
        s = u["s"]
        ex = jnp.exp(s - jnp.max(s, axis=1, keepdims=True))
        u["den"] = jnp.sum(ex, axis=1, keepdims=True)
        u["ex"] = ex.astype(BF16)
    for u in units:
        o = jnp.dot(u["ex"], u["v"], preferred_element_type=F32) / u["den"]
        u["o"] = jnp.where(m1, o[:CHUNK], o[CHUNK:])
    rows = [jnp.concatenate([u["o"] for u in units[j * n_pairs:(j + 1) * n_pairs]], axis=1)
            for j in range(ATTN_BLOCK_CHUNKS)]
    o = jnp.concatenate(rows, axis=0)
    o_ref[0] = _rms(o, g_ref[...]).astype(BF16)


def _attn(qkv3, bias_l, l, g):
    B, S, _ = qkv3.shape
    rows = ATTN_BLOCK_CHUNKS * CHUNK
    nc = S // rows

    def bias_spec(j):
        return pl.BlockSpec(
            (1, 1, 8 * CHUNK, ATTN_WINDOW),
            lambda b, n: (l, jnp.minimum(n * ATTN_BLOCK_CHUNKS + j, N_BIAS_TABLES - 1), 0, 0))

    return pl.pallas_call(
        _attn_kernel,
        grid=(B, nc),
        in_specs=[pl.BlockSpec((1, rows, ATTN_WIDTH), lambda b, n: (b, n, 0)),
                  pl.BlockSpec((1, S, ATTN_WIDTH), lambda b, n: (b, 0, 1)),
                  pl.BlockSpec((1, S, ATTN_WIDTH), lambda b, n: (b, 0, 2))]
                 + [bias_spec(j) for j in range(ATTN_BLOCK_CHUNKS)]
                 + [pl.BlockSpec((1, ATTN_WIDTH), lambda b, n: (0, 0))],
        out_specs=pl.BlockSpec((1, rows, ATTN_WIDTH), lambda b, n: (b, n, 0)),
        out_shape=jax.ShapeDtypeStruct((B, S, ATTN_WIDTH), BF16),
        compiler_params=_cparams(("parallel", "arbitrary")),
        name="chunk_attn",
    )(qkv3, qkv3, qkv3, *([bias_l] * ATTN_BLOCK_CHUNKS), g)


def _outproj_kernel(yr_ref, ya_ref, x_ref, w_ref, g_ref, router_ref, xo_ref, h_ref, ridx_ref, rgate_ref):
    y = jnp.concatenate([yr_ref[...], ya_ref[...]], axis=1)
    xn = x_ref[...] + jnp.dot(y, w_ref[...], preferred_element_type=F32)
    xo_ref[...] = xn
    h = _rms(xn, g_ref[...])
    h_ref[...] = h.astype(BF16)
    lane = lax.broadcasted_iota(jnp.int32, ridx_ref.shape, 1)
    neg = jnp.asarray(-jnp.inf, F32)
    lg = jnp.where(lane < N_EXPERTS, _dot_f32(h, router_ref[...]), neg)
    top1 = jnp.max(lg, axis=1, keepdims=True)
    idx1 = jnp.min(jnp.where(lg == top1, lane, LANES), axis=1, keepdims=True)
    lg2 = jnp.where(lane == idx1, neg, lg)
    top2 = jnp.max(lg2, axis=1, keepdims=True)
    idx2 = jnp.min(jnp.where(lg2 == top2, lane, LANES), axis=1, keepdims=True)
    ex = jnp.exp(top2 - top1)
    ridx_ref[...] = jnp.where(lane == 0, idx1, jnp.where(lane == 1, idx2, 0))
    rgate_ref[...] = jnp.where(lane == 0, 1.0 / (1.0 + ex), jnp.where(lane == 1, ex / (1.0 + ex), 0.0))


def _outproj(yr, ya, x2, w, g, router, tm):
    T = x2.shape[0]
    row_tile = lambda width: pl.BlockSpec((tm, width), lambda i: (i, 0))
    return pl.pallas_call(
        _outproj_kernel,
        grid=(T // tm,),
        in_specs=[row_tile(RWKV_WIDTH), row_tile(ATTN_WIDTH), row_tile(D_MODEL),
                  pl.BlockSpec((D_MODEL, D_MODEL), lambda i: (0, 0)),
                  pl.BlockSpec((1, D_MODEL), lambda i: (0, 0)),
                  pl.BlockSpec((D_MODEL, LANES), lambda i: (0, 0))],
        out_specs=[row_tile(D_MODEL), row_tile(D_MODEL), row_tile(LANES), row_tile(LANES)],
        out_shape=[jax.ShapeDtypeStruct((T, D_MODEL), F32), jax.ShapeDtypeStruct((T, D_MODEL), BF16),
                   jax.ShapeDtypeStruct((T, LANES), jnp.int32), jax.ShapeDtypeStruct((T, LANES), F32)],
        compiler_params=_cparams(("parallel",)),
        name="outproj_router",
    )(yr, ya, x2, w, g, router)


FF_TILE = 256


def _ffn_kernel(yr_ref, ya_ref, x_ref, wo_ref, g_ref, wg_ref, wu_ref, wd_ref, *rest, final):
    if final:
        gf_ref, o_ref = rest
    else:
        (o_ref,) = rest
    y = jnp.concatenate([yr_ref[...], ya_ref[...]], axis=1)
    xn = x_ref[...] + jnp.dot(y, wo_ref[...], preferred_element_type=F32)
    h = _rms(xn, g_ref[...]).astype(BF16)
    acc = None
    for f in range(0, wg_ref.shape[1], FF_TILE):
        gate = jnp.dot(h, wg_ref[:, f:f + FF_TILE], preferred_element_type=F32)
        up = jnp.dot(h, wu_ref[:, f:f + FF_TILE], preferred_element_type=F32)
        act = (gate * _sigmoid(gate) * up).astype(BF16)
        d = jnp.dot(act, wd_ref[f:f + FF_TILE, :], preferred_element_type=F32)
        acc = d if acc is None else acc + d
    xo = xn + acc
    o_ref[...] = _rms(xo, gf_ref[...]) if final else xo


def _ffn(yr, ya, x2, w_out, g, wg, wu, wd, final_g, tm):
    T = x2.shape[0]
    F = wg.shape[1]
    final = final_g is not None
    in_specs = [pl.BlockSpec((tm, RWKV_WIDTH), lambda i: (i, 0)),
                pl.BlockSpec((tm, ATTN_WIDTH), lambda i: (i, 0)),
                pl.BlockSpec((tm, D_MODEL), lambda i: (i, 0)),
                pl.BlockSpec((D_MODEL, D_MODEL), lambda i: (0, 0)),
                pl.BlockSpec((1, D_MODEL), lambda i: (0, 0)),
                pl.BlockSpec((D_MODEL, F), lambda i: (0, 0)),
                pl.BlockSpec((D_MODEL, F), lambda i: (0, 0)),
                pl.BlockSpec((F, D_MODEL), lambda i: (0, 0))]
    args = [yr, ya, x2, w_out, g, wg, wu, wd]
    if final:
        in_specs.append(pl.BlockSpec((1, D_MODEL), lambda i: (0, 0)))
        args.append(final_g)
    return pl.pallas_call(
        functools.partial(_ffn_kernel, final=final),
        grid=(T // tm,),
        in_specs=in_specs,
        out_specs=pl.BlockSpec((tm, D_MODEL), lambda i: (i, 0)),
        out_shape=jax.ShapeDtypeStruct((T, D_MODEL), F32),
        compiler_params=_cparams(("parallel",)),
        name="ffn_dense",
    )(*args)


MOE_TILE = 512
MOE_TOK_TILE = 256
MOE_LOCAL_ROWS = 640
SEG_PIECES = (256, 128, 64, 32, 16, 8)
GAP_PIECES = tuple(b for b in (256, 128, 64, 32, 16, 8) if b < MOE_TILE)
TOP_K = 2


def _moe_offsets(ridx, T):
    nt = T // MOE_TOK_TILE
    e12 = ridx[:, :TOP_K]
    onehot = (e12[:, :, None] == jnp.arange(N_EXPERTS, dtype=jnp.int32)[None, None, :]).astype(jnp.int32)
    cnt = onehot.sum(axis=1).reshape(nt, MOE_TOK_TILE, N_EXPERTS).sum(axis=1)
    pc = ((cnt + 7) // 8) * 8
    loff = jnp.cumsum(pc, axis=1) - pc
    tot = pc.sum(axis=0)
    grp = ((tot + MOE_TILE - 1) // MOE_TILE) * MOE_TILE
    gend = jnp.cumsum(grp)
    gstart = gend - grp
    goff = gstart[None, :] + jnp.cumsum(pc, axis=0) - pc
    n_tiles = -(-(TOP_K * T + 8 * (N_EXPERTS - 1) * nt) // MOE_TILE) + N_EXPERTS
    tile_start = jnp.arange(n_tiles, dtype=jnp.int32) * MOE_TILE
    tile_exp = jnp.minimum(jnp.sum((tile_start[:, None] >= gend[None, :]).astype(jnp.int32), axis=1),
                           N_EXPERTS - 1)
    tile_rows = jnp.clip((gstart + tot)[tile_exp] - tile_start, 0, MOE_TILE)
    flat = lambda t: t.reshape(-1).astype(jnp.int32)
    used_tiles = gend[-1] // MOE_TILE
    gaps = jnp.concatenate([gstart + tot, grp - tot, jnp.stack([used_tiles, n_tiles - used_tiles])])
    return (flat(loff), flat(goff), flat(pc), flat(gaps), tile_exp.astype(jnp.int32),
            tile_rows.astype(jnp.int32), n_tiles)


def _local_positions(ridx, loff_ref, j):
    rows = ridx.shape[0]
    lane = lax.broadcasted_iota(jnp.int32, (rows, LANES), 1)
    oh0 = lane == ridx[:, 0:1]
    oh1 = lane == ridx[:, 1:2]
    ti = lax.broadcasted_iota(jnp.int32, (rows, rows), 0)
    tj = lax.broadcasted_iota(jnp.int32, (rows, rows), 1)
    before = jnp.where(tj < ti, 1.0, 0.0).astype(BF16)
    f0 = jnp.where(oh0, 1.0, 0.0)
    f1 = jnp.where(oh1, 1.0, 0.0)
    pre0 = jnp.dot(before, f0.astype(BF16), preferred_element_type=F32)
    pre1 = jnp.dot(before, f1.astype(BF16), preferred_element_type=F32)
    c0 = jnp.sum(f0, axis=0, keepdims=True)
    lane1 = lax.broadcasted_iota(jnp.int32, (1, LANES), 1)
    loff = jnp.zeros((1, LANES), F32)
    for e in range(N_EXPERTS):
        loff = jnp.where(lane1 == e, loff_ref[j * N_EXPERTS + e].astype(F32), loff)
    pos0 = jnp.sum(jnp.where(oh0, loff + pre0, 0.0), axis=1, keepdims=True)
    pos1 = jnp.sum(jnp.where(oh1, loff + c0 + pre1, 0.0), axis=1, keepdims=True)
    return pos0, pos1


def _segment_copies(j, pc_ref, loff_ref, goff_ref, local_ref, hbm_ref, sem, to_hbm):
    out = []
    for e in range(N_EXPERTS):
        n = pc_ref[j * N_EXPERTS + e]
        lo = loff_ref[j * N_EXPERTS + e]
        go = goff_ref[j * N_EXPERTS + e]
        for b in SEG_PIECES:
            done = n & ~(2 * b - 1)
            loc = local_ref.at[pl.ds(pl.multiple_of(lo + done, 8), b), :]
            glob = hbm_ref.at[pl.ds(pl.multiple_of(go + done, 8), b), :]
            cp = pltpu.make_async_copy(loc, glob, sem) if to_hbm else pltpu.make_async_copy(glob, loc, sem)
            out.append(((n & b) != 0, cp))
    return out


def _start_segments(*args):
    for cond, cp in _segment_copies(*args):
        pl.when(cond)(cp.start)


def _wait_segments(*args):
    for cond, cp in _segment_copies(*args):
        pl.when(cond)(cp.wait)


def _gap_copies(gaps_ref, zero_ref, xs_hbm, sem):
    out = []
    for e in range(N_EXPERTS):
        start = gaps_ref[e]
        n = gaps_ref[N_EXPERTS + e]
        for b in GAP_PIECES:
            done = n & ~(2 * b - 1)
            dst = xs_hbm.at[pl.ds(pl.multiple_of(start + done, 8), b), :]
            out.append(((n & b) != 0, pltpu.make_async_copy(zero_ref.at[pl.ds(0, b), :], dst, sem)))
    return out


def _zero_fill_gaps(gaps_ref, zero_ref, xs_hbm, sem):
    zero_ref[...] = jnp.zeros(zero_ref.shape, F32)
    first_tile = gaps_ref[2 * N_EXPERTS]
    n_tail = gaps_ref[2 * N_EXPERTS + 1]

    def tail_copy(i):
        row0 = pl.multiple_of((first_tile + i) * MOE_TILE, MOE_TILE)
        return pltpu.make_async_copy(zero_ref, xs_hbm.at[pl.ds(row0, MOE_TILE), :], sem)

    for cond, cp in _gap_copies(gaps_ref, zero_ref, xs_hbm, sem):
        pl.when(cond)(cp.start)
    lax.fori_loop(0, n_tail, lambda i, c: (tail_copy(i).start(), c)[1], 0)
    for cond, cp in _gap_copies(gaps_ref, zero_ref, xs_hbm, sem):
        pl.when(cond)(cp.wait)
    lax.fori_loop(0, n_tail, lambda i, c: (tail_copy(i).wait(), c)[1], 0)


def _moe_dispatch_kernel(loff_ref, goff_ref, pc_ref, gaps_ref, ridx_ref, h_ref, xs_hbm, local_ref, zero_ref,
                         sem, zsem):
    j = pl.program_id(0)
    n = pl.num_programs(0)
    slot = j % 2

    @pl.when(j == 0)
    def _gaps():
        _zero_fill_gaps(gaps_ref, zero_ref, xs_hbm, zsem.at[0])

    pos0, pos1 = _local_positions(ridx_ref[...], loff_ref, j)
    col = lax.broadcasted_iota(jnp.int32, (MOE_TOK_TILE, MOE_LOCAL_ROWS), 1).astype(F32)
    place = jnp.where((col == pos0) | (col == pos1), 1.0, 0.0).astype(BF16)
    xs = lax.dot_general(place, h_ref[...], (((0,), (0,)), ((), ())), preferred_element_type=F32)

    def seg(jj, s):
        return (jj, pc_ref, loff_ref, goff_ref, local_ref.at[s], xs_hbm, sem.at[s], True)

    @pl.when(j >= 2)
    def _reuse():
        _wait_segments(*seg(j - 2, slot))

    local_ref[slot] = xs
    _start_segments(*seg(j, slot))

    @pl.when(j == n - 1)
    def _drain():
        _wait_segments(*seg(j, slot))

        @pl.when(n >= 2)
        def _():
            _wait_segments(*seg(j - 1, 1 - slot))


def _moe_dispatch(ridx, h, loff, goff, pc, gaps, n_rows):
    T = h.shape[0]
    grid_spec = pltpu.PrefetchScalarGridSpec(
        num_scalar_prefetch=4,
        grid=(T // MOE_TOK_TILE,),
        in_specs=[pl.BlockSpec((MOE_TOK_TILE, LANES), lambda j, *_: (j, 0)),
                  pl.BlockSpec((MOE_TOK_TILE, D_MODEL), lambda j, *_: (j, 0))],
        out_specs=pl.BlockSpec(memory_space=pl.ANY),
        scratch_shapes=[pltpu.VMEM((2, MOE_LOCAL_ROWS, D_MODEL), F32), pltpu.VMEM((MOE_TILE, D_MODEL), F32),
                        pltpu.SemaphoreType.DMA((2,)), pltpu.SemaphoreType.DMA((1,))])
    return pl.pallas_call(
        _moe_dispatch_kernel,
        grid_spec=grid_spec,
        out_shape=jax.ShapeDtypeStruct((n_rows, D_MODEL), F32),
        compiler_params=_cparams(("arbitrary",)),
        name="moe_dispatch",
    )(loff, goff, pc, gaps, ridx, h)


def _moe_expert_kernel(texp_ref, trows_ref, xs_ref, wg_ref, wu_ref, wd_ref, ys_ref):
    del texp_ref
    rows = trows_ref[pl.program_id(0)]

    @pl.when(rows > 0)
    def _compute():
        x = xs_ref[...].astype(BF16)
        gate = jnp.dot(x, wg_ref[0], preferred_element_type=F32)
        up = jnp.dot(x, wu_ref[0], preferred_element_type=F32)
        act = (gate * _sigmoid(gate) * up).astype(BF16)
        ys_ref[...] = jnp.dot(act, wd_ref[0], preferred_element_type=F32)

    @pl.when(rows == 0)
    def _skip():
        ys_ref[...] = jnp.zeros(ys_ref.shape, F32)


def _moe_experts(xs, tile_exp, tile_rows, wg, wu, wd):
    n_tiles = tile_exp.shape[0]
    _, _, F = wg.shape
    grid_spec = pltpu.PrefetchScalarGridSpec(
        num_scalar_prefetch=2,
        grid=(n_tiles,),
        in_specs=[pl.BlockSpec((MOE_TILE, D_MODEL), lambda i, te, tr: (i, 0)),
                  pl.BlockSpec((1, D_MODEL, F), lambda i, te, tr: (te[i], 0, 0)),
                  pl.BlockSpec((1, D_MODEL, F), lambda i, te, tr: (te[i], 0, 0)),
                  pl.BlockSpec((1, F, D_MODEL), lambda i, te, tr: (te[i], 0, 0))],
        out_specs=pl.BlockSpec((MOE_TILE, D_MODEL), lambda i, te, tr: (i, 0)))
    return pl.pallas_call(
        _moe_expert_kernel,
        grid_spec=grid_spec,
        out_shape=jax.ShapeDtypeStruct((n_tiles * MOE_TILE, D_MODEL), F32),
        compiler_params=_cparams(("arbitrary",)),
        name="moe_experts",
    )(tile_exp, tile_rows, xs, wg, wu, wd)


def _moe_combine_kernel(loff_ref, goff_ref, pc_ref, ridx_ref, rgate_ref, x_ref, ys_hbm, *rest, final):
    if final:
        gf_ref, o_ref, local_ref, sem = rest
    else:
        o_ref, local_ref, sem = rest
    j = pl.program_id(0)
    n = pl.num_programs(0)
    slot = j % 2

    def seg(jj, s):
        return (jj, pc_ref, loff_ref, goff_ref, local_ref.at[s], ys_hbm, sem.at[s], False)

    @pl.when(j == 0)
    def _prologue():
        _start_segments(*seg(j, slot))

    @pl.when(j + 1 < n)
    def _prefetch():
        _start_segments(*seg(j + 1, 1 - slot))

    pos0, pos1 = _local_positions(ridx_ref[...], loff_ref, j)
    col = lax.broadcasted_iota(jnp.int32, (MOE_TOK_TILE, MOE_LOCAL_ROWS), 1).astype(F32)
    pick0 = jnp.where(col == pos0, 1.0, 0.0).astype(BF16)
    pick1 = jnp.where(col == pos1, 1.0, 0.0).astype(BF16)
    used = loff_ref[j * N_EXPERTS + N_EXPERTS - 1] + pc_ref[j * N_EXPERTS + N_EXPERTS - 1]
    _wait_segments(*seg(j, slot))
    ri = lax.broadcasted_iota(jnp.int32, (MOE_LOCAL_ROWS, D_MODEL), 0)
    ys = jnp.where(ri < used, local_ref[slot], 0.0).astype(BF16)
    y0 = jnp.dot(pick0, ys, preferred_element_type=F32)
    y1 = jnp.dot(pick1, ys, preferred_element_type=F32)
    gates = rgate_ref[...]
    xo = x_ref[...] + gates[:, 0:1] * y0 + gates[:, 1:2] * y1
    o_ref[...] = _rms(xo, gf_ref[...]) if final else xo


def _moe_combine(x2, ys, ridx, rgate, loff, goff, pc, final_g):
    T = x2.shape[0]
    final = final_g is not None
    in_specs = [pl.BlockSpec((MOE_TOK_TILE, LANES), lambda j, *_: (j, 0)),
                pl.BlockSpec((MOE_TOK_TILE, LANES), lambda j, *_: (j, 0)),
                pl.BlockSpec((MOE_TOK_TILE, D_MODEL), lambda j, *_: (j, 0)),
                pl.BlockSpec(memory_space=pl.ANY)]
    args = [ridx, rgate, x2, ys]
    if final:
        in_specs.append(pl.BlockSpec((1, D_MODEL), lambda j, *_: (0, 0)))
        args.append(final_g)
    grid_spec = pltpu.PrefetchScalarGridSpec(
        num_scalar_prefetch=3,
        grid=(T // MOE_TOK_TILE,),
        in_specs=in_specs,
        out_specs=pl.BlockSpec((MOE_TOK_TILE, D_MODEL), lambda j, *_: (j, 0)),
        scratch_shapes=[pltpu.VMEM((2, MOE_LOCAL_ROWS, D_MODEL), F32), pltpu.SemaphoreType.DMA((2,))])
    return pl.pallas_call(
        functools.partial(_moe_combine_kernel, final=final),
        grid_spec=grid_spec,
        out_shape=jax.ShapeDtypeStruct((T, D_MODEL), F32),
        compiler_params=_cparams(("arbitrary",)),
        name="moe_combine",
    )(loff, goff, pc, *args)


def _moe(h, x2, ridx, rgate, wg, wu, wd, final_g):
    T = x2.shape[0]
    loff, goff, pc, gaps, tile_exp, tile_rows, n_tiles = _moe_offsets(ridx, T)
    xs = _moe_dispatch(ridx, h, loff, goff, pc, gaps, n_tiles * MOE_TILE)
    ys = _moe_experts(xs, tile_exp, tile_rows, wg, wu, wd)
    return _moe_combine(x2, ys, ridx, rgate, loff, goff, pc, final_g)


def kernel(x, norm_mix_g, w_in, shift_mu, rwkv_w0, rwkv_w2, rwkv_a0, rwkv_a2, rwkv_g2, rwkv_k_k, rwkv_k_a, rwkv_r_k, rwkv_ln_w, rwkv_ln_b, attn_rel_bias, attn_norm_g, w_out, norm_ffn_g, ffn_w_gate, ffn_w_up, ffn_w_down, moe_router, moe_w_gate, moe_w_up, moe_w_down, norm_final_g):
    B, S, D = x.shape
    depth = w_in.shape[0]
    T = B * S
    tm = min(512, T)
    row = lambda t: t.reshape(1, -1).astype(F32)

    hi = jnp.arange(LANES) // HEAD_DIM
    bd = (hi[:, None] == hi[None, :]).astype(BF16)
    rbt = jnp.pad(jnp.swapaxes(attn_rel_bias, 1, 2).astype(F32),
                  ((0, 0), (0, 0), (0, REL_ROWS - attn_rel_bias.shape[1])))
    bias_tabs = _bias_tables(rbt)

    x2 = x.reshape(T, D).astype(F32)
    for l in range(depth):
        w_shift = w_in[l, :, :SHIFT_COLS].astype(BF16)
        w_attn = w_in[l, :, SHIFT_COLS:].astype(BF16)
        ps, qkv = _inproj(x2, row(norm_mix_g[l]), w_shift, w_attn, tm)

        zeros = jnp.zeros((DECAY_LORA, RWKV_WIDTH), F32)
        wa = jnp.concatenate([jnp.concatenate([rwkv_w2[l], zeros], axis=1),
                              jnp.concatenate([zeros, rwkv_a2[l]], axis=1)], axis=0).astype(BF16)
        vec = jnp.stack([rwkv_w0[l], rwkv_a0[l], rwkv_k_k[l], rwkv_k_a[l], rwkv_r_k[l],
                         rwkv_ln_w[l], rwkv_ln_b[l], jnp.zeros_like(rwkv_w0[l])]).astype(F32)
        y_rwkv = _rwkv(ps.reshape(B, S, SHIFT_COLS), row(shift_mu[l]), vec, wa,
                       rwkv_g2[l].astype(BF16), bd)
        y_attn = _attn(qkv.reshape(B, S, 3 * ATTN_WIDTH), bias_tabs, l, row(attn_norm_g[l]))

        li = l // 2
        final_g = row(norm_final_g) if l == depth - 1 else None
        mixed = (y_rwkv.reshape(T, RWKV_WIDTH), y_attn.reshape(T, ATTN_WIDTH), x2,
                 w_out[l].astype(BF16), row(norm_ffn_g[l]))
        if l % 2 == 1:
            router = jnp.pad(moe_router[li].astype(F32), ((0, 0), (0, LANES - N_EXPERTS)))
            x_mid, h, ridx, rgate = _outproj(*mixed, router, tm)
            x2 = _moe(h, x_mid, ridx, rgate, moe_w_gate[li].astype(BF16), moe_w_up[li].astype(BF16),
                      moe_w_down[li].astype(BF16), final_g)
        else:
            x2 = _ffn(*mixed, ffn_w_gate[li].astype(BF16), ffn_w_up[li].astype(BF16),
                      ffn_w_down[li].astype(BF16), final_g, tm)
    return x2.reshape(B, S, D).astype(x.dtype)
```

```python
import functools

import jax
import jax.numpy as jnp
from jax import lax
from jax.experimental import pallas as pl
from jax.experimental.pallas import tpu as pltpu

F32 = jnp.float32
BF16 = jnp.bfloat16

D_MODEL = 1024
CHUNK = 64
N_LEFT_CHUNKS = 8
HEAD_DIM = 64
RWKV_WIDTH = 512
ATTN_WIDTH = 512
DECAY_LORA = 64
AAA_LORA = 64
GATE_LORA = 128
REL_CLIP = 128
N_EXPERTS = 8
RMS_EPS = 1e-6
GN_EPS = 64e-5
MASK_VALUE = -1e30
SHIFT_COLS = 3 * RWKV_WIDTH + DECAY_LORA + AAA_LORA + GATE_LORA

LANES = 128
PAIR = 2 * CHUNK
N_PAIRS = RWKV_WIDTH // LANES
ATTN_WINDOW = (N_LEFT_CHUNKS + 2) * CHUNK
N_BIAS_TABLES = N_LEFT_CHUNKS + 2
BIAS_BASE = 768
REL_ROWS = 384
VMEM_LIMIT = 56 * 1024 * 1024


def _cparams(sem):
    return pltpu.CompilerParams(dimension_semantics=sem, vmem_limit_bytes=VMEM_LIMIT)


def _mm(a, b):
    return jnp.dot(a.astype(BF16), b.astype(BF16), preferred_element_type=F32)


def _mm_nt(a, b):
    return lax.dot_general(a.astype(BF16), b.astype(BF16), (((1,), (1,)), ((), ())),
                           preferred_element_type=F32)


def _mm_tn(a, b):
    return lax.dot_general(a.astype(BF16), b.astype(BF16), (((0,), (0,)), ((), ())),
                           preferred_element_type=F32)


def _split_terms(x, n):
    terms, rem = [], x
    for _ in range(n):
        hi = rem.astype(BF16)
        terms.append(hi)
        rem = rem - hi.astype(F32)
    return terms


def _dot_exact_rhs(x, w_bf16, n):
    acc = None
    for t in _split_terms(x, n):
        d = jnp.dot(t, w_bf16, preferred_element_type=F32)
        acc = d if acc is None else acc + d
    return acc


def _dot_exact_lhs(w_bf16, x, n):
    acc = None
    for t in _split_terms(x, n):
        d = jnp.dot(w_bf16, t, preferred_element_type=F32)
        acc = d if acc is None else acc + d
    return acc


def _dot_f32(a, b):
    a1, a2 = _split_terms(a, 2)
    b1, b2 = _split_terms(b, 2)
    n = b.shape[1]
    t = jnp.dot(a1, jnp.concatenate([b1, b2], axis=1), preferred_element_type=F32)
    return t[:, :n] + t[:, n:] + jnp.dot(a2, b1, preferred_element_type=F32)


def _sigmoid(x):
    return 1.0 / (1.0 + jnp.exp(-x))


def _rms(x, g):
    return x * lax.rsqrt(jnp.mean(x * x, axis=-1, keepdims=True) + RMS_EPS) * g


def _inproj_kernel(x_ref, g_ref, ws_ref, wa_ref, ps_ref, qkv_ref):
    hb = _rms(x_ref[...], g_ref[...]).astype(BF16)
    ps_ref[...] = jnp.dot(hb, ws_ref[...], preferred_element_type=F32)
    qkv_ref[...] = jnp.dot(hb, wa_ref[...], preferred_element_type=F32).astype(BF16)


def _inproj(x2, g, w_shift, w_attn, tm):
    T = x2.shape[0]
    ns, na = w_shift.shape[1], w_attn.shape[1]
    return pl.pallas_call(
        _inproj_kernel,
        grid=(T // tm,),
        in_specs=[pl.BlockSpec((tm, D_MODEL), lambda i: (i, 0)),
                  pl.BlockSpec((1, D_MODEL), lambda i: (0, 0)),
                  pl.BlockSpec((D_MODEL, ns), lambda i: (0, 0)),
                  pl.BlockSpec((D_MODEL, na), lambda i: (0, 0))],
        out_specs=[pl.BlockSpec((tm, ns), lambda i: (i, 0)),
                   pl.BlockSpec((tm, na), lambda i: (i, 0))],
        out_shape=[jax.ShapeDtypeStruct((T, ns), F32),
                   jax.ShapeDtypeStruct((T, na), BF16)],
        compiler_params=_cparams(("parallel",)),
        name="inproj",
    )(x2, g, w_shift, w_attn)


def _rwkv_kernel(ps_ref, prev_ref, mu_ref, vec_ref, wa_ref, g2_ref, bd_ref, y_ref, h_ref):
    c = pl.program_id(1)

    @pl.when(c == 0)
    def _init():
        h_ref[...] = jnp.zeros(h_ref.shape, F32)

    p = ps_ref[0]
    last = jnp.where(c > 0, prev_ref[0][7:8, :], 0.0)
    row = lax.broadcasted_iota(jnp.int32, p.shape, 0)
    prev = jnp.where(row == 0, last, pltpu.roll(p, 1, 0))
    xs = p + mu_ref[...] * (prev - p)

    W = RWKV_WIDTH
    r, k, v = xs[:, 0:W], xs[:, W:2 * W], xs[:, 2 * W:3 * W]
    z0 = xs[:, 3 * W:3 * W + LANES]
    gd = xs[:, 3 * W + LANES:3 * W + 2 * LANES]
    m1 = lax.broadcasted_iota(jnp.int32, (CHUNK, LANES), 1) < HEAD_DIM
    z0 = jnp.where(lax.broadcasted_iota(jnp.int32, z0.shape, 1) < DECAY_LORA, jnp.tanh(z0), z0)
    lora = _mm(z0, wa_ref[...])
    vec = vec_ref[...]
    w0, a0, k_k, k_a, r_k, ln_w, ln_b = (vec[i:i + 1] for i in range(7))
    w = w0 + lora[:, :W]
    a = _sigmoid(a0 + lora[:, W:])
    g = _mm(_sigmoid(gd), g2_ref[...])
    softplus_neg_w = jnp.maximum(-w, 0.0) + jnp.log(1.0 + jnp.exp(-jnp.abs(w)))
    lw = -jnp.exp(-softplus_neg_w - 0.5)

    bd = bd_ref[...]

    def head_sums(x):
        xb = x.astype(BF16)
        return jnp.concatenate(
            [jnp.dot(xb[:, LANES * i:LANES * (i + 1)], bd, preferred_element_type=F32) for i in range(N_PAIRS)],
            axis=1)

    kk = k * k_k
    kk = kk / jnp.maximum(jnp.sqrt(head_sums(kk * kk)), 1e-12)
    k2 = k * (1.0 + (a - 1.0) * k_a)
    kka = kk * a

    rows = p.shape[0]
    n_chunks = rows // CHUNK
    ti = lax.broadcasted_iota(jnp.int32, (rows, rows), 0)
    tj = lax.broadcasted_iota(jnp.int32, (rows, rows), 1)
    tri = jnp.where((ti >= tj) & ((ti >> 6) == (tj >> 6)), 1.0, 0.0).astype(BF16)
    L = _dot_exact_lhs(tri, lw, 2)

    ri = lax.broadcasted_iota(jnp.int32, (PAIR, PAIR), 0)
    ci = lax.broadcasted_iota(jnp.int32, (PAIR, PAIR), 1)
    same_head = (ri >> 6) == (ci >> 6)
    strict = same_head & (ri > ci)
    incl = same_head & (ri >= ci)
    eye = ri == ci
    eye_f = jnp.where(eye, 1.0, 0.0)

    units = []
    for j in range(n_chunks):
        rs = slice(CHUNK * j, CHUNK * (j + 1))
        Lj, lwj = L[rs], lw[rs]
        Lc = Lj[CHUNK - 1:CHUNK]
        inv = jnp.exp(-Lj)
        to_end = jnp.exp(Lc - Lj)
        gC = jnp.exp(Lc)
        Rt = r[rs] * jnp.exp(Lj)
        At = -kk[rs] * jnp.exp(Lj - lwj)
        Bt, Kt = kka[rs] * inv, k2[rs] * inv
        Bh, Kh = kka[rs] * to_end, k2[rs] * to_end
        vj = v[rs]
        for pi in range(N_PAIRS):
            sl = slice(LANES * pi, LANES * (pi + 1))

            def stack(x):
                xp = x[:, sl]
                return jnp.concatenate([jnp.where(m1, xp, 0.0), jnp.where(m1, 0.0, xp)], axis=0)

            units.append(dict(
                j=j, pi=pi, gC=gC[:, sl], sRt=stack(Rt),
                sAt=stack(At).astype(BF16), sV=stack(vj).astype(BF16),
                sBt=stack(Bt).astype(BF16), sKt=stack(Kt).astype(BF16),
                sBh=stack(Bh).astype(BF16), sKh=stack(Kh).astype(BF16)))

    for u in units:
        big = _mm_nt(jnp.concatenate([u["sAt"], u["sRt"].astype(BF16)], axis=0),
                     jnp.concatenate([u["sBt"], u["sKt"]], axis=0))
        u["AB"] = jnp.where(strict, big[:PAIR, :PAIR], 0.0)
        u["AK"] = jnp.where(strict, big[:PAIR, PAIR:], 0.0)
        u["RB"] = jnp.where(incl, big[PAIR:, :PAIR], 0.0)
        u["RK"] = jnp.where(incl, big[PAIR:, PAIR:], 0.0)
    for u in units:
        u["X"] = eye_f + u["AB"]
        u["Pw"] = _mm(u["AB"], u["AB"])
        u["W1"] = _mm(u["AK"], u["sV"])
    for _ in range(4):
        for u in units:
            PX = _mm(u["Pw"], jnp.concatenate([u["Pw"], u["X"]], axis=1))
            u["Pw"] = PX[:, :PAIR]
            u["X"] = u["X"] + PX[:, PAIR:]
    for u in units:
        u["Tm"] = u["X"] + _mm(u["Pw"], u["X"])
    for u in units:
        u["PQ"] = _mm(u["Tm"], jnp.concatenate([u["sAt"], u["W1"].astype(BF16)], axis=1)).astype(BF16)
    for u in units:
        PQ = u["PQ"]
        Pm, Q = PQ[:, :PAIR], PQ[:, PAIR:]
        RBPQ = _mm(u["RB"], PQ)
        u["Rp"] = u["sRt"] + RBPQ[:, :PAIR]
        u["Y0"] = RBPQ[:, PAIR:] + _mm(u["RK"], u["sV"])
        u["Mm"] = jnp.where(eye, u["gC"], 0.0) + _mm_tn(u["sBh"], Pm)
        u["G"] = _mm_tn(jnp.concatenate([u["sBh"], u["sKh"]], axis=0),
                        jnp.concatenate([Q, u["sV"]], axis=0))
    H = [h_ref[pi] for pi in range(N_PAIRS)]
    y_rows = []
    for j in range(n_chunks):
        ys = []
        for u in units[j * N_PAIRS:(j + 1) * N_PAIRS]:
            pi = u["pi"]
            YH = _mm(jnp.concatenate([u["Rp"], u["Mm"]], axis=0), H[pi])
            Ysm = YH[:PAIR] + u["Y0"]
            H[pi] = YH[PAIR:] + u["G"]
            ys.append(Ysm[:CHUNK] + Ysm[CHUNK:])
        y_rows.append(jnp.concatenate(ys, axis=1))
    for pi in range(N_PAIRS):
        h_ref[pi] = H[pi]
    y = y_rows[0] if n_chunks == 1 else jnp.concatenate(y_rows, axis=0)

    inv_n = 1.0 / HEAD_DIM
    mean = head_sums(y) * inv_n
    d = y - mean
    var = head_sums(d * d) * inv_n
    yn = d * lax.rsqrt(var + GN_EPS) * ln_w + ln_b
    bonus = head_sums(r * k2 * r_k) * v
    y_ref[0] = ((yn + bonus) * g).astype(BF16)


RWKV_BLOCK_CHUNKS = 4


def _rwkv(ps3, mu, vec, wa, g2, bd):
    B, S, _ = ps3.shape
    rows = RWKV_BLOCK_CHUNKS * CHUNK
    nc = S // rows
    rows8 = rows // 8
    return pl.pallas_call(
        _rwkv_kernel,
        grid=(B, nc),
        in_specs=[pl.BlockSpec((1, rows, SHIFT_COLS), lambda b, c: (b, c, 0)),
                  pl.BlockSpec((1, 8, SHIFT_COLS), lambda b, c: (b, jnp.maximum(c * rows8 - 1, 0), 0)),
                  pl.BlockSpec((1, SHIFT_COLS), lambda b, c: (0, 0)),
                  pl.BlockSpec((8, RWKV_WIDTH), lambda b, c: (0, 0)),
                  pl.BlockSpec((LANES, 2 * RWKV_WIDTH), lambda b, c: (0, 0)),
                  pl.BlockSpec((GATE_LORA, RWKV_WIDTH), lambda b, c: (0, 0)),
                  pl.BlockSpec((LANES, LANES), lambda b, c: (0, 0))],
        out_specs=pl.BlockSpec((1, rows, RWKV_WIDTH), lambda b, c: (b, c, 0)),
        out_shape=jax.ShapeDtypeStruct((B, S, RWKV_WIDTH), BF16),
        scratch_shapes=[pltpu.VMEM((N_PAIRS, PAIR, LANES), F32)],
        compiler_params=_cparams(("parallel", "arbitrary")),
        name="rwkv7",
    )(ps3, ps3, mu, vec, wa, g2, bd)


def _bias_kernel(rbt_ref, o_ref):
    e = pl.program_id(1)
    xi = lax.broadcasted_iota(jnp.int32, (REL_ROWS, BIAS_BASE), 1)
    ji = lax.broadcasted_iota(jnp.int32, (REL_ROWS, BIAS_BASE), 0)
    off = jnp.where(xi < BIAS_BASE - CHUNK, xi, xi - BIAS_BASE)
    idx = jnp.clip(e * CHUNK - off, -REL_CLIP, REL_CLIP) + REL_CLIP
    onehot = jnp.where(idx == ji, 1.0, 0.0).astype(BF16)
    base = _dot_exact_rhs(rbt_ref[0], onehot, 3)
    kj = lax.broadcasted_iota(jnp.int32, (CHUNK, ATTN_WINDOW), 1)
    kc = kj >> 6
    valid = (kc <= e) & (kc >= e - N_LEFT_CHUNKS)
    for h in range(8):
        rows = jnp.broadcast_to(base[h:h + 1, :], (CHUNK, BIAS_BASE))
        toep = pltpu.roll(rows, 0, 1, stride=1, stride_axis=0)
        o_ref[0, 0, h * CHUNK:(h + 1) * CHUNK, :] = jnp.where(valid, toep[:, :ATTN_WINDOW], MASK_VALUE)


def _bias_tables(rbt):
    L = rbt.shape[0]
    return pl.pallas_call(
        _bias_kernel,
        grid=(L, N_BIAS_TABLES),
        in_specs=[pl.BlockSpec((1, 8, REL_ROWS), lambda l, e: (l, 0, 0))],
        out_specs=pl.BlockSpec((1, 1, 8 * CHUNK, ATTN_WINDOW), lambda l, e: (l, e, 0, 0)),
        out_shape=jax.ShapeDtypeStruct((L, N_BIAS_TABLES, 8 * CHUNK, ATTN_WINDOW), F32),
        compiler_params=_cparams(("parallel", "parallel")),
        name="bias_tables",
    )(rbt)


ATTN_BLOCK_CHUNKS = 8


def _attn_kernel(q_ref, k_ref, v_ref, *rest):
    bias_refs, g_ref, o_ref = rest[:ATTN_BLOCK_CHUNKS], rest[-2], rest[-1]
    n0 = pl.program_id(1) * ATTN_BLOCK_CHUNKS
    q = q_ref[0] * jnp.asarray(HEAD_DIM ** -0.5, BF16)
    m1 = lax.broadcasted_iota(jnp.int32, (CHUNK, LANES), 1) < HEAD_DIM
    zero = jnp.zeros((), BF16)
    n_pairs = ATTN_WIDTH // LANES
    units = []
    for j in range(ATTN_BLOCK_CHUNKS):
        start = pl.multiple_of(jnp.maximum(n0 + j - (N_LEFT_CHUNKS + 1), 0) * CHUNK, CHUNK)
        kw = k_ref[0, pl.ds(start, ATTN_WINDOW), :]
        vw = v_ref[0, pl.ds(start, ATTN_WINDOW), :]
        qj = q[CHUNK * j:CHUNK * (j + 1)]
        for pi in range(n_pairs):
            sl = slice(LANES * pi, LANES * (pi + 1))
            qp = qj[:, sl]
            qs = jnp.concatenate([jnp.where(m1, qp, zero), jnp.where(m1, zero, qp)], axis=0)
            units.append(dict(qs=qs, k=kw[:, sl], v=vw[:, sl],
                              bias=bias_refs[j][0, 0, PAIR * pi:PAIR * (pi + 1), :]))
    for u in units:
        u["s"] = lax.dot_general(u["qs"], u["k"], (((1,), (1,)), ((), ())),
                                 preferred_element_type=F32) + u["bias"]
    for u in units:
        s = u["s"]
        ex = jnp.exp(s - jnp.max(s, axis=1, keepdims=True))
        u["den"] = jnp.sum(ex, axis=1, keepdims=True)
        u["ex"] = ex.astype(BF16)
    for u in units:
        o = jnp.dot(u["ex"], u["v"], preferred_element_type=F32) / u["den"]
        u["o"] = jnp.where(m1, o[:CHUNK], o[CHUNK:])
    rows = [jnp.concatenate([u["o"] for u in units[j * n_pairs:(j + 1) * n_pairs]], axis=1)
            for j in range(ATTN_BLOCK_CHUNKS)]
    o = jnp.concatenate(rows, axis=0)
    o_ref[0] = _rms(o, g_ref[...]).astype(BF16)


def _attn(qkv3, bias_l, l, g):
    B, S, _ = qkv3.shape
    rows = ATTN_BLOCK_CHUNKS * CHUNK
    nc = S // rows

    def bias_spec(j):
        return pl.BlockSpec(
            (1, 1, 8 * CHUNK, ATTN_WINDOW),
            lambda b, n: (l, jnp.minimum(n * ATTN_BLOCK_CHUNKS + j, N_BIAS_TABLES - 1), 0, 0))

    return pl.pallas_call(
        _attn_kernel,
        grid=(B, nc),
        in_specs=[pl.BlockSpec((1, rows, ATTN_WIDTH), lambda b, n: (b, n, 0)),
                  pl.BlockSpec((1, S, ATTN_WIDTH), lambda b, n: (b, 0, 1)),
                  pl.BlockSpec((1, S, ATTN_WIDTH), lambda b, n: (b, 0, 2))]
                 + [bias_spec(j) for j in range(ATTN_BLOCK_CHUNKS)]
                 + [pl.BlockSpec((1, ATTN_WIDTH), lambda b, n: (0, 0))],
        out_specs=pl.BlockSpec((1, rows, ATTN_WIDTH), lambda b, n: (b, n, 0)),
        out_shape=jax.ShapeDtypeStruct((B, S, ATTN_WIDTH), BF16),
        compiler_params=_cparams(("parallel", "arbitrary")),
        name="chunk_attn",
    )(qkv3, qkv3, qkv3, *([bias_l] * ATTN_BLOCK_CHUNKS), g)


def _outproj_kernel(yr_ref, ya_ref, x_ref, w_ref, g_ref, router_ref, xo_ref, h_ref, ridx_ref, rgate_ref):
    y = jnp.concatenate([yr_ref[...], ya_ref[...]], axis=1)
    xn = x_ref[...] + jnp.dot(y, w_ref[...], preferred_element_type=F32)
    xo_ref[...] = xn
    h = _rms(xn, g_ref[...])
    h_ref[...] = h.astype(BF16)
    lane = lax.broadcasted_iota(jnp.int32, ridx_ref.shape, 1)
    neg = jnp.asarray(-jnp.inf, F32)
    lg = jnp.where(lane < N_EXPERTS, _dot_f32(h, router_ref[...]), neg)
    top1 = jnp.max(lg, axis=1, keepdims=True)
    idx1 = jnp.min(jnp.where(lg == top1, lane, LANES), axis=1, keepdims=True)
    lg2 = jnp.where(lane == idx1, neg, lg)
    top2 = jnp.max(lg2, axis=1, keepdims=True)
    idx2 = jnp.min(jnp.where(lg2 == top2, lane, LANES), axis=1, keepdims=True)
    ex = jnp.exp(top2 - top1)
    ridx_ref[...] = jnp.where(lane == 0, idx1, jnp.where(lane == 1, idx2, 0))
    rgate_ref[...] = jnp.where(lane == 0, 1.0 / (1.0 + ex), jnp.where(lane == 1, ex / (1.0 + ex), 0.0))


def _outproj(yr, ya, x2, w, g, router, tm):
    T = x2.shape[0]
    row_tile = lambda width: pl.BlockSpec((tm, width), lambda i: (i, 0))
    return pl.pallas_call(
        _outproj_kernel,
        grid=(T // tm,),
        in_specs=[row_tile(RWKV_WIDTH), row_tile(ATTN_WIDTH), row_tile(D_MODEL),
                  pl.BlockSpec((D_MODEL, D_MODEL), lambda i: (0, 0)),
                  pl.BlockSpec((1, D_MODEL), lambda i: (0, 0)),
                  pl.BlockSpec((D_MODEL, LANES), lambda i: (0, 0))],
        out_specs=[row_tile(D_MODEL), row_tile(D_MODEL), row_tile(LANES), row_tile(LANES)],
        out_shape=[jax.ShapeDtypeStruct((T, D_MODEL), F32), jax.ShapeDtypeStruct((T, D_MODEL), BF16),
                   jax.ShapeDtypeStruct((T, LANES), jnp.int32), jax.ShapeDtypeStruct((T, LANES), F32)],
        compiler_params=_cparams(("parallel",)),
        name="outproj_router",
    )(yr, ya, x2, w, g, router)


FF_TILE = 256


def _ffn_kernel(yr_ref, ya_ref, x_ref, wo_ref, g_ref, wg_ref, wu_ref, wd_ref, *rest, final):
    if final:
        gf_ref, o_ref = rest
    else:
        (o_ref,) = rest
    y = jnp.concatenate([yr_ref[...], ya_ref[...]], axis=1)
    xn = x_ref[...] + jnp.dot(y, wo_ref[...], preferred_element_type=F32)
    h = _rms(xn, g_ref[...]).astype(BF16)
    acc = None
    for f in range(0, wg_ref.shape[1], FF_TILE):
        gate = jnp.dot(h, wg_ref[:, f:f + FF_TILE], preferred_element_type=F32)
        up = jnp.dot(h, wu_ref[:, f:f + FF_TILE], preferred_element_type=F32)
        act = (gate * _sigmoid(gate) * up).astype(BF16)
        d = jnp.dot(act, wd_ref[f:f + FF_TILE, :], preferred_element_type=F32)
        acc = d if acc is None else acc + d
    xo = xn + acc
    o_ref[...] = _rms(xo, gf_ref[...]) if final else xo


def _ffn(yr, ya, x2, w_out, g, wg, wu, wd, final_g, tm):
    T = x2.shape[0]
    F = wg.shape[1]
    final = final_g is not None
    in_specs = [pl.BlockSpec((tm, RWKV_WIDTH), lambda i: (i, 0)),
                pl.BlockSpec((tm, ATTN_WIDTH), lambda i: (i, 0)),
                pl.BlockSpec((tm, D_MODEL), lambda i: (i, 0)),
                pl.BlockSpec((D_MODEL, D_MODEL), lambda i: (0, 0)),
                pl.BlockSpec((1, D_MODEL), lambda i: (0, 0)),
                pl.BlockSpec((D_MODEL, F), lambda i: (0, 0)),
                pl.BlockSpec((D_MODEL, F), lambda i: (0, 0)),
                pl.BlockSpec((F, D_MODEL), lambda i: (0, 0))]
    args = [yr, ya, x2, w_out, g, wg, wu, wd]
    if final:
        in_specs.append(pl.BlockSpec((1, D_MODEL), lambda i: (0, 0)))
        args.append(final_g)
    return pl.pallas_call(
        functools.partial(_ffn_kernel, final=final),
        grid=(T // tm,),
        in_specs=in_specs,
        out_specs=pl.BlockSpec((tm, D_MODEL), lambda i: (i, 0)),
        out_shape=jax.ShapeDtypeStruct((T, D_MODEL), F32),
        compiler_params=_cparams(("parallel",)),
        name="ffn_dense",
    )(*args)


MOE_TILE = 512
MOE_TOK_TILE = 256
SEG_ALIGN = 8
MOE_LOCAL_ROWS = 640
SEG_PIECES = (256, 128, 64, 32, 16, 8)
GAP_PIECES = tuple(b for b in SEG_PIECES if b < MOE_TILE)
TOP_K = 2


def _moe_offsets(ridx, T):
    nt = T // MOE_TOK_TILE
    e12 = ridx[:, :TOP_K]
    onehot = (e12[:, :, None] == jnp.arange(N_EXPERTS, dtype=jnp.int32)[None, None, :]).astype(jnp.int32)
    cnt = onehot.sum(axis=1).reshape(nt, MOE_TOK_TILE, N_EXPERTS).sum(axis=1)
    pc = ((cnt + SEG_ALIGN - 1) // SEG_ALIGN) * SEG_ALIGN
    loff = jnp.cumsum(pc, axis=1) - pc
    tot = pc.sum(axis=0)
    grp = ((tot + MOE_TILE - 1) // MOE_TILE) * MOE_TILE
    gend = jnp.cumsum(grp)
    gstart = gend - grp
    goff = gstart[None, :] + jnp.cumsum(pc, axis=0) - pc
    n_tiles = -(-(TOP_K * T + (SEG_ALIGN - 1) * N_EXPERTS * nt) // MOE_TILE) + N_EXPERTS
    tile_start = jnp.arange(n_tiles, dtype=jnp.int32) * MOE_TILE
    tile_exp = jnp.minimum(jnp.sum((tile_start[:, None] >= gend[None, :]).astype(jnp.int32), axis=1),
                           N_EXPERTS - 1)
    tile_rows = jnp.clip((gstart + tot)[tile_exp] - tile_start, 0, MOE_TILE)
    flat = lambda t: t.reshape(-1).astype(jnp.int32)
    used_tiles = gend[-1] // MOE_TILE
    gaps = jnp.concatenate([gstart + tot, grp - tot, jnp.stack([used_tiles, n_tiles - used_tiles])])
    return (flat(loff), flat(goff), flat(pc), flat(gaps), tile_exp.astype(jnp.int32),
            tile_rows.astype(jnp.int32), n_tiles)


def _local_positions(ridx, loff_ref, j):
    rows = ridx.shape[0]
    lane = lax.broadcasted_iota(jnp.int32, (rows, LANES), 1)
    oh0 = lane == ridx[:, 0:1]
    oh1 = lane == ridx[:, 1:2]
    ti = lax.broadcasted_iota(jnp.int32, (rows, rows), 0)
    tj = lax.broadcasted_iota(jnp.int32, (rows, rows), 1)
    before = jnp.where(tj < ti, 1.0, 0.0).astype(BF16)
    f0 = jnp.where(oh0, 1.0, 0.0)
    f1 = jnp.where(oh1, 1.0, 0.0)
    pre0 = jnp.dot(before, f0.astype(BF16), preferred_element_type=F32)
    pre1 = jnp.dot(before, f1.astype(BF16), preferred_element_type=F32)
    c0 = jnp.sum(f0, axis=0, keepdims=True)
    lane1 = lax.broadcasted_iota(jnp.int32, (1, LANES), 1)
    loff = jnp.zeros((1, LANES), F32)
    for e in range(N_EXPERTS):
        loff = jnp.where(lane1 == e, loff_ref[j * N_EXPERTS + e].astype(F32), loff)
    pos0 = jnp.sum(jnp.where(oh0, loff + pre0, 0.0), axis=1, keepdims=True)
    pos1 = jnp.sum(jnp.where(oh1, loff + c0 + pre1, 0.0), axis=1, keepdims=True)
    return pos0, pos1


def _segment_copies(j, pc_ref, loff_ref, goff_ref, local_ref, hbm_ref, sem, to_hbm):
    out = []
    for e in range(N_EXPERTS):
        n = pc_ref[j * N_EXPERTS + e]
        lo = loff_ref[j * N_EXPERTS + e]
        go = goff_ref[j * N_EXPERTS + e]
        for b in SEG_PIECES:
            done = n & ~(2 * b - 1)
            loc = local_ref.at[pl.ds(pl.multiple_of(lo + done, SEG_ALIGN), b), :]
            glob = hbm_ref.at[pl.ds(pl.multiple_of(go + done, SEG_ALIGN), b), :]
            cp = pltpu.make_async_copy(loc, glob, sem) if to_hbm else pltpu.make_async_copy(glob, loc, sem)
            out.append(((n & b) != 0, cp))
    return out


def _start_segments(*args):
    for cond, cp in _segment_copies(*args):
        pl.when(cond)(cp.start)


def _wait_segments(j, pc_ref, loff_ref, goff_ref, local_ref, hbm_ref, sem, to_hbm):
    del goff_ref
    last = j * N_EXPERTS + N_EXPERTS - 1
    total = loff_ref[last] + pc_ref[last]
    for b in (512,) + SEG_PIECES:
        loc = local_ref.at[pl.ds(0, b), :]
        glob = hbm_ref.at[pl.ds(0, b), :]
        cp = pltpu.make_async_copy(loc, glob, sem) if to_hbm else pltpu.make_async_copy(glob, loc, sem)
        pl.when((total & b) != 0)(cp.wait)


def _gap_copies(gaps_ref, zero_ref, xs_hbm, sem):
    out = []
    for e in range(N_EXPERTS):
        start = gaps_ref[e]
        n = gaps_ref[N_EXPERTS + e]
        for b in GAP_PIECES:
            done = n & ~(2 * b - 1)
            dst = xs_hbm.at[pl.ds(pl.multiple_of(start + done, SEG_ALIGN), b), :]
            out.append(((n & b) != 0, pltpu.make_async_copy(zero_ref.at[pl.ds(0, b), :], dst, sem)))
    return out


def _zero_fill_gaps(gaps_ref, zero_ref, xs_hbm, sem):
    zero_ref[...] = jnp.zeros(zero_ref.shape, F32)
    first_tile = gaps_ref[2 * N_EXPERTS]
    n_tail = gaps_ref[2 * N_EXPERTS + 1]

    def tail_copy(i):
        row0 = pl.multiple_of((first_tile + i) * MOE_TILE, MOE_TILE)
        return pltpu.make_async_copy(zero_ref, xs_hbm.at[pl.ds(row0, MOE_TILE), :], sem)

    for cond, cp in _gap_copies(gaps_ref, zero_ref, xs_hbm, sem):
        pl.when(cond)(cp.start)
    lax.fori_loop(0, n_tail, lambda i, c: (tail_copy(i).start(), c)[1], 0)
    for cond, cp in _gap_copies(gaps_ref, zero_ref, xs_hbm, sem):
        pl.when(cond)(cp.wait)
    lax.fori_loop(0, n_tail, lambda i, c: (tail_copy(i).wait(), c)[1], 0)


def _moe_dispatch_kernel(loff_ref, goff_ref, pc_ref, gaps_ref, ridx_ref, h_ref, xs_hbm, local_ref, zero_ref,
                         sem, zsem):
    j = pl.program_id(0)
    n = pl.num_programs(0)
    slot = j % 2

    @pl.when(j == 0)
    def _gaps():
        _zero_fill_gaps(gaps_ref, zero_ref, xs_hbm, zsem.at[0])

    pos0, pos1 = _local_positions(ridx_ref[...], loff_ref, j)
    col = lax.broadcasted_iota(jnp.int32, (MOE_TOK_TILE, MOE_LOCAL_ROWS), 1).astype(F32)
    place = jnp.where((col == pos0) | (col == pos1), 1.0, 0.0).astype(BF16)
    xs = lax.dot_general(place, h_ref[...], (((0,), (0,)), ((), ())), preferred_element_type=F32)

    def seg(jj, s):
        return (jj, pc_ref, loff_ref, goff_ref, local_ref.at[s], xs_hbm, sem.at[s], True)

    @pl.when(j >= 2)
    def _reuse():
        _wait_segments(*seg(j - 2, slot))

    local_ref[slot] = xs
    _start_segments(*seg(j, slot))

    @pl.when(j == n - 1)
    def _drain():
        _wait_segments(*seg(j, slot))

        @pl.when(n >= 2)
        def _():
            _wait_segments(*seg(j - 1, 1 - slot))


def _moe_dispatch(ridx, h, loff, goff, pc, gaps, n_rows):
    T = h.shape[0]
    grid_spec = pltpu.PrefetchScalarGridSpec(
        num_scalar_prefetch=4,
        grid=(T // MOE_TOK_TILE,),
        in_specs=[pl.BlockSpec((MOE_TOK_TILE, LANES), lambda j, *_: (j, 0)),
                  pl.BlockSpec((MOE_TOK_TILE, D_MODEL), lambda j, *_: (j, 0))],
        out_specs=pl.BlockSpec(memory_space=pl.ANY),
        scratch_shapes=[pltpu.VMEM((2, MOE_LOCAL_ROWS, D_MODEL), F32), pltpu.VMEM((MOE_TILE, D_MODEL), F32),
                        pltpu.SemaphoreType.DMA((2,)), pltpu.SemaphoreType.DMA((1,))])
    return pl.pallas_call(
        _moe_dispatch_kernel,
        grid_spec=grid_spec,
        out_shape=jax.ShapeDtypeStruct((n_rows, D_MODEL), F32),
        compiler_params=_cparams(("arbitrary",)),
        name="moe_dispatch",
    )(loff, goff, pc, gaps, ridx, h)


def _moe_expert_kernel(texp_ref, trows_ref, xs_ref, wg_ref, wu_ref, wd_ref, ys_ref):
    del texp_ref
    rows = trows_ref[pl.program_id(0)]

    @pl.when(rows > 0)
    def _compute():
        x = xs_ref[...].astype(BF16)
        gate = jnp.dot(x, wg_ref[0], preferred_element_type=F32)
        up = jnp.dot(x, wu_ref[0], preferred_element_type=F32)
        act = (gate * _sigmoid(gate) * up).astype(BF16)
        ys_ref[...] = jnp.dot(act, wd_ref[0], preferred_element_type=F32)

    @pl.when(rows == 0)
    def _skip():
        ys_ref[...] = jnp.zeros(ys_ref.shape, F32)


def _moe_experts(xs, tile_exp, tile_rows, wg, wu, wd):
    n_tiles = tile_exp.shape[0]
    _, _, F = wg.shape
    grid_spec = pltpu.PrefetchScalarGridSpec(
        num_scalar_prefetch=2,
        grid=(n_tiles,),
        in_specs=[pl.BlockSpec((MOE_TILE, D_MODEL), lambda i, te, tr: (i, 0)),
                  pl.BlockSpec((1, D_MODEL, F), lambda i, te, tr: (te[i], 0, 0)),
                  pl.BlockSpec((1, D_MODEL, F), lambda i, te, tr: (te[i], 0, 0)),
                  pl.BlockSpec((1, F, D_MODEL), lambda i, te, tr: (te[i], 0, 0))],
        out_specs=pl.BlockSpec((MOE_TILE, D_MODEL), lambda i, te, tr: (i, 0)))
    return pl.pallas_call(
        _moe_expert_kernel,
        grid_spec=grid_spec,
        out_shape=jax.ShapeDtypeStruct((n_tiles * MOE_TILE, D_MODEL), F32),
        compiler_params=_cparams(("arbitrary",)),
        name="moe_experts",
    )(tile_exp, tile_rows, xs, wg, wu, wd)


def _moe_combine_kernel(loff_ref, goff_ref, pc_ref, ridx_ref, rgate_ref, x_ref, ys_hbm, *rest, final):
    if final:
        gf_ref, o_ref, local_ref, sem = rest
    else:
        o_ref, local_ref, sem = rest
    j = pl.program_id(0)
    n = pl.num_programs(0)
    slot = j % 2

    def seg(jj, s):
        return (jj, pc_ref, loff_ref, goff_ref, local_ref.at[s], ys_hbm, sem.at[s], False)

    @pl.when(j == 0)
    def _prologue():
        _start_segments(*seg(j, slot))

    @pl.when(j + 1 < n)
    def _prefetch():
        _start_segments(*seg(j + 1, 1 - slot))

    pos0, pos1 = _local_positions(ridx_ref[...], loff_ref, j)
    col = lax.broadcasted_iota(jnp.int32, (MOE_TOK_TILE, MOE_LOCAL_ROWS), 1).astype(F32)
    pick0 = jnp.where(col == pos0, 1.0, 0.0).astype(BF16)
    pick1 = jnp.where(col == pos1, 1.0, 0.0).astype(BF16)
    used = loff_ref[j * N_EXPERTS + N_EXPERTS - 1] + pc_ref[j * N_EXPERTS + N_EXPERTS - 1]
    _wait_segments(*seg(j, slot))
    ri = lax.broadcasted_iota(jnp.int32, (MOE_LOCAL_ROWS, D_MODEL), 0)
    ys = jnp.where(ri < used, local_ref[slot], 0.0).astype(BF16)
    y0 = jnp.dot(pick0, ys, preferred_element_type=F32)
    y1 = jnp.dot(pick1, ys, preferred_element_type=F32)
    gates = rgate_ref[...]
    xo = x_ref[...] + gates[:, 0:1] * y0 + gates[:, 1:2] * y1
    o_ref[...] = _rms(xo, gf_ref[...]) if final else xo


def _moe_combine(x2, ys, ridx, rgate, loff, goff, pc, final_g):
    T = x2.shape[0]
    final = final_g is not None
    in_specs = [pl.BlockSpec((MOE_TOK_TILE, LANES), lambda j, *_: (j, 0)),
                pl.BlockSpec((MOE_TOK_TILE, LANES), lambda j, *_: (j, 0)),
                pl.BlockSpec((MOE_TOK_TILE, D_MODEL), lambda j, *_: (j, 0)),
                pl.BlockSpec(memory_space=pl.ANY)]
    args = [ridx, rgate, x2, ys]
    if final:
        in_specs.append(pl.BlockSpec((1, D_MODEL), lambda j, *_: (0, 0)))
        args.append(final_g)
    grid_spec = pltpu.PrefetchScalarGridSpec(
        num_scalar_prefetch=3,
        grid=(T // MOE_TOK_TILE,),
        in_specs=in_specs,
        out_specs=pl.BlockSpec((MOE_TOK_TILE, D_MODEL), lambda j, *_: (j, 0)),
        scratch_shapes=[pltpu.VMEM((2, MOE_LOCAL_ROWS, D_MODEL), F32), pltpu.SemaphoreType.DMA((2,))])
    return pl.pallas_call(
        functools.partial(_moe_combine_kernel, final=final),
        grid_spec=grid_spec,
        out_shape=jax.ShapeDtypeStruct((T, D_MODEL), F32),
        compiler_params=_cparams(("arbitrary",)),
        name="moe_combine",
    )(loff, goff, pc, *args)


def _moe(h, x2, ridx, rgate, wg, wu, wd, final_g):
    T = x2.shape[0]
    loff, goff, pc, gaps, tile_exp, tile_rows, n_tiles = _moe_offsets(ridx, T)
    xs = _moe_dispatch(ridx, h, loff, goff, pc, gaps, n_tiles * MOE_TILE)
    ys = _moe_experts(xs, tile_exp, tile_rows, wg, wu, wd)
    return _moe_combine(x2, ys, ridx, rgate, loff, goff, pc, final_g)


def kernel(x, norm_mix_g, w_in, shift_mu, rwkv_w0, rwkv_w2, rwkv_a0, rwkv_a2, rwkv_g2, rwkv_k_k, rwkv_k_a, rwkv_r_k, rwkv_ln_w, rwkv_ln_b, attn_rel_bias, attn_norm_g, w_out, norm_ffn_g, ffn_w_gate, ffn_w_up, ffn_w_down, moe_router, moe_w_gate, moe_w_up, moe_w_down, norm_final_g):
    B, S, D = x.shape
    depth = w_in.shape[0]
    T = B * S
    tm = min(512, T)
    row = lambda t: t.reshape(1, -1).astype(F32)

    hi = jnp.arange(LANES) // HEAD_DIM
    bd = (hi[:, None] == hi[None, :]).astype(BF16)
    rbt = jnp.pad(jnp.swapaxes(attn_rel_bias, 1, 2).astype(F32),
                  ((0, 0), (0, 0), (0, REL_ROWS - attn_rel_bias.shape[1])))
    bias_tabs = _bias_tables(rbt)

    x2 = x.reshape(T, D).astype(F32)
    for l in range(depth):
        w_shift = w_in[l, :, :SHIFT_COLS].astype(BF16)
        w_attn = w_in[l, :, SHIFT_COLS:].astype(BF16)
        ps, qkv = _inproj(x2, row(norm_mix_g[l]), w_shift, w_attn, tm)

        zeros = jnp.zeros((DECAY_LORA, RWKV_WIDTH), F32)
        wa = jnp.concatenate([jnp.concatenate([rwkv_w2[l], zeros], axis=1),
                              jnp.concatenate([zeros, rwkv_a2[l]], axis=1)], axis=0).astype(BF16)
        vec = jnp.stack([rwkv_w0[l], rwkv_a0[l], rwkv_k_k[l], rwkv_k_a[l], rwkv_r_k[l],
                         rwkv_ln_w[l], rwkv_ln_b[l], jnp.zeros_like(rwkv_w0[l])]).astype(F32)
        y_rwkv = _rwkv(ps.reshape(B, S, SHIFT_COLS), row(shift_mu[l]), vec, wa,
                       rwkv_g2[l].astype(BF16), bd)
        y_attn = _attn(qkv.reshape(B, S, 3 * ATTN_WIDTH), bias_tabs, l, row(attn_norm_g[l]))

        li = l // 2
        final_g = row(norm_final_g) if l == depth - 1 else None
        mixed = (y_rwkv.reshape(T, RWKV_WIDTH), y_attn.reshape(T, ATTN_WIDTH), x2,
                 w_out[l].astype(BF16), row(norm_ffn_g[l]))
        if l % 2 == 1:
            router = jnp.pad(moe_router[li].astype(F32), ((0, 0), (0, LANES - N_EXPERTS)))
            x_mid, h, ridx, rgate = _outproj(*mixed, router, tm)
            x2 = _moe(h, x_mid, ridx, rgate, moe_w_gate[li].astype(BF16), moe_w_up[li].astype(BF16),
                      moe_w_down[li].astype(BF16), final_g)
        else:
            x2 = _ffn(*mixed, ffn_w_gate[li].astype(BF16), ffn_w_up[li].astype(BF16),
                      ffn_w_down[li].astype(BF16), final_g, tm)
    return x2.reshape(B, S, D).astype(x.dtype)
```

```python
import functools

import jax
import jax.numpy as jnp
from jax import lax
from jax.experimental import pallas as pl
from jax.experimental.pallas import tpu as pltpu

F32 = jnp.float32
BF16 = jnp.bfloat16

D_MODEL = 1024
CHUNK = 64
N_LEFT_CHUNKS = 8
HEAD_DIM = 64
RWKV_WIDTH = 512
ATTN_WIDTH = 512
DECAY_LORA = 64
AAA_LORA = 64
GATE_LORA = 128
REL_CLIP = 128
N_EXPERTS = 8
RMS_EPS = 1e-6
GN_EPS = 64e-5
MASK_VALUE = -1e30
SHIFT_COLS = 3 * RWKV_WIDTH + DECAY_LORA + AAA_LORA + GATE_LORA

LANES = 128
PAIR = 2 * CHUNK
N_PAIRS = RWKV_WIDTH // LANES
ATTN_WINDOW = (N_LEFT_CHUNKS + 2) * CHUNK
N_BIAS_TABLES = N_LEFT_CHUNKS + 2
BIAS_BASE = 768
REL_ROWS = 384
VMEM_LIMIT = 56 * 1024 * 1024


def _cparams(sem):
    return pltpu.CompilerParams(dimension_semantics=sem, vmem_limit_bytes=VMEM_LIMIT)


def _mm(a, b):
    return jnp.dot(a.astype(BF16), b.astype(BF16), preferred_element_type=F32)


def _mm_nt(a, b):
    return lax.dot_general(a.astype(BF16), b.astype(BF16), (((1,), (1,)), ((), ())),
                           preferred_element_type=F32)


def _mm_tn(a, b):
    return lax.dot_general(a.astype(BF16), b.astype(BF16), (((0,), (0,)), ((), ())),
                           preferred_element_type=F32)


def _split_terms(x, n):
    terms, rem = [], x
    for _ in range(n):
        hi = rem.astype(BF16)
        terms.append(hi)
        rem = rem - hi.astype(F32)
    return terms


def _dot_exact_rhs(x, w_bf16, n):
    acc = None
    for t in _split_terms(x, n):
        d = jnp.dot(t, w_bf16, preferred_element_type=F32)
        acc = d if acc is None else acc + d
    return acc


def _dot_exact_lhs(w_bf16, x, n):
    acc = None
    for t in _split_terms(x, n):
        d = jnp.dot(w_bf16, t, preferred_element_type=F32)
        acc = d if acc is None else acc + d
    return acc


def _dot_f32(a, b):
    a1, a2 = _split_terms(a, 2)
    b1, b2 = _split_terms(b, 2)
    n = b.shape[1]
    t = jnp.dot(a1, jnp.concatenate([b1, b2], axis=1), preferred_element_type=F32)
    return t[:, :n] + t[:, n:] + jnp.dot(a2, b1, preferred_element_type=F32)


def _sigmoid(x):
    return 1.0 / (1.0 + jnp.exp(-x))


def _rms(x, g):
    return x * lax.rsqrt(jnp.mean(x * x, axis=-1, keepdims=True) + RMS_EPS) * g


def _inproj_kernel(x_ref, g_ref, w_ref, ps_ref, qkv_ref):
    hb = _rms(x_ref[...], g_ref[...]).astype(BF16)
    ps_ref[...] = jnp.dot(hb, w_ref[0, :, :SHIFT_COLS].astype(BF16), preferred_element_type=F32)
    qkv_ref[...] = jnp.dot(hb, w_ref[0, :, SHIFT_COLS:].astype(BF16), preferred_element_type=F32).astype(BF16)


def _inproj(x2, g, w_in, l, tm):
    T = x2.shape[0]
    n_in = w_in.shape[2]
    na = n_in - SHIFT_COLS
    return pl.pallas_call(
        _inproj_kernel,
        grid=(T // tm,),
        in_specs=[pl.BlockSpec((tm, D_MODEL), lambda i: (i, 0)),
                  pl.BlockSpec((1, D_MODEL), lambda i: (0, 0)),
                  pl.BlockSpec((1, D_MODEL, n_in), lambda i: (l, 0, 0))],
        out_specs=[pl.BlockSpec((tm, SHIFT_COLS), lambda i: (i, 0)),
                   pl.BlockSpec((tm, na), lambda i: (i, 0))],
        out_shape=[jax.ShapeDtypeStruct((T, SHIFT_COLS), F32),
                   jax.ShapeDtypeStruct((T, na), BF16)],
        compiler_params=_cparams(("parallel",)),
        name="inproj",
    )(x2, g, w_in)


def _rwkv_kernel(ps_ref, prev_ref, mu_ref, vec_ref, wa_ref, g2_ref, bd_ref, y_ref, h_ref):
    c = pl.program_id(1)

    @pl.when(c == 0)
    def _init():
        h_ref[...] = jnp.zeros(h_ref.shape, F32)

    p = ps_ref[0]
    last = jnp.where(c > 0, prev_ref[0][7:8, :], 0.0)
    row = lax.broadcasted_iota(jnp.int32, p.shape, 0)
    prev = jnp.where(row == 0, last, pltpu.roll(p, 1, 0))
    xs = p + mu_ref[...] * (prev - p)

    W = RWKV_WIDTH
    r, k, v = xs[:, 0:W], xs[:, W:2 * W], xs[:, 2 * W:3 * W]
    z0 = xs[:, 3 * W:3 * W + LANES]
    gd = xs[:, 3 * W + LANES:3 * W + 2 * LANES]
    m1 = lax.broadcasted_iota(jnp.int32, (CHUNK, LANES), 1) < HEAD_DIM
    z0 = jnp.where(lax.broadcasted_iota(jnp.int32, z0.shape, 1) < DECAY_LORA, jnp.tanh(z0), z0)
    lora = _mm(z0, wa_ref[...])
    vec = vec_ref[...]
    w0, a0, k_k, k_a, r_k, ln_w, ln_b = (vec[i:i + 1] for i in range(7))
    w = w0 + lora[:, :W]
    a = _sigmoid(a0 + lora[:, W:])
    g = _mm(_sigmoid(gd), g2_ref[...])
    softplus_neg_w = jnp.maximum(-w, 0.0) + jnp.log(1.0 + jnp.exp(-jnp.abs(w)))
    lw = -jnp.exp(-softplus_neg_w - 0.5)

    bd = bd_ref[...]

    def head_sums(x):
        xb = x.astype(BF16)
        return jnp.concatenate(
            [jnp.dot(xb[:, LANES * i:LANES * (i + 1)], bd, preferred_element_type=F32) for i in range(N_PAIRS)],
            axis=1)

    kk = k * k_k
    kk = kk / jnp.maximum(jnp.sqrt(head_sums(kk * kk)), 1e-12)
    k2 = k * (1.0 + (a - 1.0) * k_a)
    kka = kk * a

    rows = p.shape[0]
    n_chunks = rows // CHUNK
    ti = lax.broadcasted_iota(jnp.int32, (rows, rows), 0)
    tj = lax.broadcasted_iota(jnp.int32, (rows, rows), 1)
    tri = jnp.where((ti >= tj) & ((ti >> 6) == (tj >> 6)), 1.0, 0.0).astype(BF16)
    L = _dot_exact_lhs(tri, lw, 2)

    ri = lax.broadcasted_iota(jnp.int32, (PAIR, PAIR), 0)
    ci = lax.broadcasted_iota(jnp.int32, (PAIR, PAIR), 1)
    same_head = (ri >> 6) == (ci >> 6)
    strict = same_head & (ri > ci)
    incl = same_head & (ri >= ci)
    eye = ri == ci
    eye_f = jnp.where(eye, 1.0, 0.0)

    units = []
    for j in range(n_chunks):
        rs = slice(CHUNK * j, CHUNK * (j + 1))
        Lj, lwj = L[rs], lw[rs]
        Lc = Lj[CHUNK - 1:CHUNK]
        inv = jnp.exp(-Lj)
        to_end = jnp.exp(Lc - Lj)
        gC = jnp.exp(Lc)
        Rt = r[rs] * jnp.exp(Lj)
        At = -kk[rs] * jnp.exp(Lj - lwj)
        Bt, Kt = kka[rs] * inv, k2[rs] * inv
        Bh, Kh = kka[rs] * to_end, k2[rs] * to_end
        vj = v[rs]
        for pi in range(N_PAIRS):
            sl = slice(LANES * pi, LANES * (pi + 1))

            def stack(x):
                xp = x[:, sl]
                return jnp.concatenate([jnp.where(m1, xp, 0.0), jnp.where(m1, 0.0, xp)], axis=0)

            units.append(dict(
                j=j, pi=pi, gC=gC[:, sl], sRt=stack(Rt),
                sAt=stack(At).astype(BF16), sV=stack(vj).astype(BF16),
                sBt=stack(Bt).astype(BF16), sKt=stack(Kt).astype(BF16),
                sBh=stack(Bh).astype(BF16), sKh=stack(Kh).astype(BF16)))

    for u in units:
        big = _mm_nt(jnp.concatenate([u["sAt"], u["sRt"].astype(BF16)], axis=0),
                     jnp.concatenate([u["sBt"], u["sKt"]], axis=0))
        u["AB"] = jnp.where(strict, big[:PAIR, :PAIR], 0.0)
        u["AK"] = jnp.where(strict, big[:PAIR, PAIR:], 0.0)
        u["RB"] = jnp.where(incl, big[PAIR:, :PAIR], 0.0)
        u["RK"] = jnp.where(incl, big[PAIR:, PAIR:], 0.0)
    for u in units:
        u["X"] = eye_f + u["AB"]
        u["Pw"] = _mm(u["AB"], u["AB"])
        u["W1"] = _mm(u["AK"], u["sV"])
    for _ in range(4):
        for u in units:
            PX = _mm(u["Pw"], jnp.concatenate([u["Pw"], u["X"]], axis=1))
            u["Pw"] = PX[:, :PAIR]
            u["X"] = u["X"] + PX[:, PAIR:]
    for u in units:
        u["Tm"] = u["X"] + _mm(u["Pw"], u["X"])
    for u in units:
        u["PQ"] = _mm(u["Tm"], jnp.concatenate([u["sAt"], u["W1"].astype(BF16)], axis=1)).astype(BF16)
    for u in units:
        PQ = u["PQ"]
        Pm, Q = PQ[:, :PAIR], PQ[:, PAIR:]
        RBPQ = _mm(u["RB"], PQ)
        u["Rp"] = u["sRt"] + RBPQ[:, :PAIR]
        u["Y0"] = RBPQ[:, PAIR:] + _mm(u["RK"], u["sV"])
        u["Mm"] = jnp.where(eye, u["gC"], 0.0) + _mm_tn(u["sBh"], Pm)
        u["G"] = _mm_tn(jnp.concatenate([u["sBh"], u["sKh"]], axis=0),
                        jnp.concatenate([Q, u["sV"]], axis=0))
    H = [h_ref[pi] for pi in range(N_PAIRS)]
    y_rows = []
    for j in range(n_chunks):
        ys = []
        for u in units[j * N_PAIRS:(j + 1) * N_PAIRS]:
            pi = u["pi"]
            YH = _mm(jnp.concatenate([u["Rp"], u["Mm"]], axis=0), H[pi])
            Ysm = YH[:PAIR] + u["Y0"]
            H[pi] = YH[PAIR:] + u["G"]
            ys.append(Ysm[:CHUNK] + Ysm[CHUNK:])
        y_rows.append(jnp.concatenate(ys, axis=1))
    for pi in range(N_PAIRS):
        h_ref[pi] = H[pi]
    y = y_rows[0] if n_chunks == 1 else jnp.concatenate(y_rows, axis=0)

    inv_n = 1.0 / HEAD_DIM
    mean = head_sums(y) * inv_n
    d = y - mean
    var = head_sums(d * d) * inv_n
    yn = d * lax.rsqrt(var + GN_EPS) * ln_w + ln_b
    bonus = head_sums(r * k2 * r_k) * v
    y_ref[0] = ((yn + bonus) * g).astype(BF16)


RWKV_BLOCK_CHUNKS = 4


def _rwkv(ps3, mu, vec, wa, g2, bd):
    B, S, _ = ps3.shape
    rows = RWKV_BLOCK_CHUNKS * CHUNK
    nc = S // rows
    rows8 = rows // 8
    return pl.pallas_call(
        _rwkv_kernel,
        grid=(B, nc),
        in_specs=[pl.BlockSpec((1, rows, SHIFT_COLS), lambda b, c: (b, c, 0)),
                  pl.BlockSpec((1, 8, SHIFT_COLS), lambda b, c: (b, jnp.maximum(c * rows8 - 1, 0), 0)),
                  pl.BlockSpec((1, SHIFT_COLS), lambda b, c: (0, 0)),
                  pl.BlockSpec((8, RWKV_WIDTH), lambda b, c: (0, 0)),
                  pl.BlockSpec((LANES, 2 * RWKV_WIDTH), lambda b, c: (0, 0)),
                  pl.BlockSpec((GATE_LORA, RWKV_WIDTH), lambda b, c: (0, 0)),
                  pl.BlockSpec((LANES, LANES), lambda b, c: (0, 0))],
        out_specs=pl.BlockSpec((1, rows, RWKV_WIDTH), lambda b, c: (b, c, 0)),
        out_shape=jax.ShapeDtypeStruct((B, S, RWKV_WIDTH), BF16),
        scratch_shapes=[pltpu.VMEM((N_PAIRS, PAIR, LANES), F32)],
        compiler_params=_cparams(("parallel", "arbitrary")),
        name="rwkv7",
    )(ps3, ps3, mu, vec, wa, g2, bd)


def _bias_kernel(rbt_ref, o_ref):
    e = pl.program_id(1)
    xi = lax.broadcasted_iota(jnp.int32, (REL_ROWS, BIAS_BASE), 1)
    ji = lax.broadcasted_iota(jnp.int32, (REL_ROWS, BIAS_BASE), 0)
    off = jnp.where(xi < BIAS_BASE - CHUNK, xi, xi - BIAS_BASE)
    idx = jnp.clip(e * CHUNK - off, -REL_CLIP, REL_CLIP) + REL_CLIP
    onehot = jnp.where(idx == ji, 1.0, 0.0).astype(BF16)
    base = _dot_exact_rhs(rbt_ref[0], onehot, 3)
    kj = lax.broadcasted_iota(jnp.int32, (CHUNK, ATTN_WINDOW), 1)
    kc = kj >> 6
    valid = (kc <= e) & (kc >= e - N_LEFT_CHUNKS)
    for h in range(8):
        rows = jnp.broadcast_to(base[h:h + 1, :], (CHUNK, BIAS_BASE))
        toep = pltpu.roll(rows, 0, 1, stride=1, stride_axis=0)
        o_ref[0, 0, h * CHUNK:(h + 1) * CHUNK, :] = jnp.where(valid, toep[:, :ATTN_WINDOW], MASK_VALUE)


def _bias_tables(rbt):
    L = rbt.shape[0]
    return pl.pallas_call(
        _bias_kernel,
        grid=(L, N_BIAS_TABLES),
        in_specs=[pl.BlockSpec((1, 8, REL_ROWS), lambda l, e: (l, 0, 0))],
        out_specs=pl.BlockSpec((1, 1, 8 * CHUNK, ATTN_WINDOW), lambda l, e: (l, e, 0, 0)),
        out_shape=jax.ShapeDtypeStruct((L, N_BIAS_TABLES, 8 * CHUNK, ATTN_WINDOW), F32),
        compiler_params=_cparams(("parallel", "parallel")),
        name="bias_tables",
    )(rbt)


ATTN_BLOCK_CHUNKS = 8


def _attn_kernel(q_ref, k_ref, v_ref, *rest):
    bias_refs, g_ref, o_ref = rest[:ATTN_BLOCK_CHUNKS], rest[-2], rest[-1]
    n0 = pl.program_id(1) * ATTN_BLOCK_CHUNKS
    q = q_ref[0] * jnp.asarray(HEAD_DIM ** -0.5, BF16)
    m1 = lax.broadcasted_iota(jnp.int32, (CHUNK, LANES), 1) < HEAD_DIM
    zero = jnp.zeros((), BF16)
    n_pairs = ATTN_WIDTH // LANES
    units = []
    for j in range(ATTN_BLOCK_CHUNKS):
        start = pl.multiple_of(jnp.maximum(n0 + j - (N_LEFT_CHUNKS + 1), 0) * CHUNK, CHUNK)
        kw = k_ref[0, pl.ds(start, ATTN_WINDOW), :]
        vw = v_ref[0, pl.ds(start, ATTN_WINDOW), :]
        qj = q[CHUNK * j:CHUNK * (j + 1)]
        for pi in range(n_pairs):
            sl = slice(LANES * pi, LANES * (pi + 1))
            qp = qj[:, sl]
            qs = jnp.concatenate([jnp.where(m1, qp, zero), jnp.where(m1, zero, qp)], axis=0)
            units.append(dict(qs=qs, k=kw[:, sl], v=vw[:, sl],
                              bias=bias_refs[j][0, 0, PAIR * pi:PAIR * (pi + 1), :]))
    for u in units:
        u["s"] = lax.dot_general(u["qs"], u["k"], (((1,), (1,)), ((), ())),
                                 preferred_element_type=F32) + u["bias"]
    for u in units:
        s = u["s"]
        ex = jnp.exp(s - jnp.max(s, axis=1, keepdims=True))
        u["den"] = jnp.sum(ex, axis=1, keepdims=True)
        u["ex"] = ex.astype(BF16)
    for u in units:
        o = jnp.dot(u["ex"], u["v"], preferred_element_type=F32) / u["den"]
        u["o"] = jnp.where(m1, o[:CHUNK], o[CHUNK:])
    rows = [jnp.concatenate([u["o"] for u in units[j * n_pairs:(j + 1) * n_pairs]], axis=1)
            for j in range(ATTN_BLOCK_CHUNKS)]
    o = jnp.concatenate(rows, axis=0)
    o_ref[0] = _rms(o, g_ref[...]).astype(BF16)


def _attn(qkv3, bias_l, l, g):
    B, S, _ = qkv3.shape
    rows = ATTN_BLOCK_CHUNKS * CHUNK
    nc = S // rows

    def bias_spec(j):
        return pl.BlockSpec(
            (1, 1, 8 * CHUNK, ATTN_WINDOW),
            lambda b, n: (l, jnp.minimum(n * ATTN_BLOCK_CHUNKS + j, N_BIAS_TABLES - 1), 0, 0))

    return pl.pallas_call(
        _attn_kernel,
        grid=(B, nc),
        in_specs=[pl.BlockSpec((1, rows, ATTN_WIDTH), lambda b, n: (b, n, 0)),
                  pl.BlockSpec((1, S, ATTN_WIDTH), lambda b, n: (b, 0, 1)),
                  pl.BlockSpec((1, S, ATTN_WIDTH), lambda b, n: (b, 0, 2))]
                 + [bias_spec(j) for j in range(ATTN_BLOCK_CHUNKS)]
                 + [pl.BlockSpec((1, ATTN_WIDTH), lambda b, n: (0, 0))],
        out_specs=pl.BlockSpec((1, rows, ATTN_WIDTH), lambda b, n: (b, n, 0)),
        out_shape=jax.ShapeDtypeStruct((B, S, ATTN_WIDTH), BF16),
        compiler_params=_cparams(("parallel", "arbitrary")),
        name="chunk_attn",
    )(qkv3, qkv3, qkv3, *([bias_l] * ATTN_BLOCK_CHUNKS), g)


def _outproj_kernel(yr_ref, ya_ref, x_ref, w_ref, g_ref, router_ref, xo_ref, h_ref, ridx_ref, rgate_ref):
    y = jnp.concatenate([yr_ref[...], ya_ref[...]], axis=1)
    xn = x_ref[...] + jnp.dot(y, w_ref[...], preferred_element_type=F32)
    xo_ref[...] = xn
    h = _rms(xn, g_ref[...])
    h_ref[...] = h.astype(BF16)
    lane = lax.broadcasted_iota(jnp.int32, ridx_ref.shape, 1)
    neg = jnp.asarray(-jnp.inf, F32)
    lg = jnp.where(lane < N_EXPERTS, _dot_f32(h, router_ref[...]), neg)
    top1 = jnp.max(lg, axis=1, keepdims=True)
    idx1 = jnp.min(jnp.where(lg == top1, lane, LANES), axis=1, keepdims=True)
    lg2 = jnp.where(lane == idx1, neg, lg)
    top2 = jnp.max(lg2, axis=1, keepdims=True)
    idx2 = jnp.min(jnp.where(lg2 == top2, lane, LANES), axis=1, keepdims=True)
    ex = jnp.exp(top2 - top1)
    ridx_ref[...] = jnp.where(lane == 0, idx1, jnp.where(lane == 1, idx2, 0))
    rgate_ref[...] = jnp.where(lane == 0, 1.0 / (1.0 + ex), jnp.where(lane == 1, ex / (1.0 + ex), 0.0))


def _outproj(yr, ya, x2, w, g, router, tm):
    T = x2.shape[0]
    row_tile = lambda width: pl.BlockSpec((tm, width), lambda i: (i, 0))
    return pl.pallas_call(
        _outproj_kernel,
        grid=(T // tm,),
        in_specs=[row_tile(RWKV_WIDTH), row_tile(ATTN_WIDTH), row_tile(D_MODEL),
                  pl.BlockSpec((D_MODEL, D_MODEL), lambda i: (0, 0)),
                  pl.BlockSpec((1, D_MODEL), lambda i: (0, 0)),
                  pl.BlockSpec((D_MODEL, LANES), lambda i: (0, 0))],
        out_specs=[row_tile(D_MODEL), row_tile(D_MODEL), row_tile(LANES), row_tile(LANES)],
        out_shape=[jax.ShapeDtypeStruct((T, D_MODEL), F32), jax.ShapeDtypeStruct((T, D_MODEL), BF16),
                   jax.ShapeDtypeStruct((T, LANES), jnp.int32), jax.ShapeDtypeStruct((T, LANES), F32)],
        compiler_params=_cparams(("parallel",)),
        name="outproj_router",
    )(yr, ya, x2, w, g, router)


FF_TILE = 256


def _ffn_kernel(yr_ref, ya_ref, x_ref, wo_ref, g_ref, wg_ref, wu_ref, wd_ref, *rest, final):
    if final:
        gf_ref, o_ref = rest
    else:
        (o_ref,) = rest
    y = jnp.concatenate([yr_ref[...], ya_ref[...]], axis=1)
    xn = x_ref[...] + jnp.dot(y, wo_ref[...], preferred_element_type=F32)
    h = _rms(xn, g_ref[...]).astype(BF16)
    acc = None
    for f in range(0, wg_ref.shape[1], FF_TILE):
        gate = jnp.dot(h, wg_ref[:, f:f + FF_TILE], preferred_element_type=F32)
        up = jnp.dot(h, wu_ref[:, f:f + FF_TILE], preferred_element_type=F32)
        act = (gate * _sigmoid(gate) * up).astype(BF16)
        d = jnp.dot(act, wd_ref[f:f + FF_TILE, :], preferred_element_type=F32)
        acc = d if acc is None else acc + d
    xo = xn + acc
    o_ref[...] = _rms(xo, gf_ref[...]) if final else xo


def _ffn(yr, ya, x2, w_out, g, wg, wu, wd, final_g, tm):
    T = x2.shape[0]
    F = wg.shape[1]
    final = final_g is not None
    in_specs = [pl.BlockSpec((tm, RWKV_WIDTH), lambda i: (i, 0)),
                pl.BlockSpec((tm, ATTN_WIDTH), lambda i: (i, 0)),
                pl.BlockSpec((tm, D_MODEL), lambda i: (i, 0)),
                pl.BlockSpec((D_MODEL, D_MODEL), lambda i: (0, 0)),
                pl.BlockSpec((1, D_MODEL), lambda i: (0, 0)),
                pl.BlockSpec((D_MODEL, F), lambda i: (0, 0)),
                pl.BlockSpec((D_MODEL, F), lambda i: (0, 0)),
                pl.BlockSpec((F, D_MODEL), lambda i: (0, 0))]
    args = [yr, ya, x2, w_out, g, wg, wu, wd]
    if final:
        in_specs.append(pl.BlockSpec((1, D_MODEL), lambda i: (0, 0)))
        args.append(final_g)
    return pl.pallas_call(
        functools.partial(_ffn_kernel, final=final),
        grid=(T // tm,),
        in_specs=in_specs,
        out_specs=pl.BlockSpec((tm, D_MODEL), lambda i: (i, 0)),
        out_shape=jax.ShapeDtypeStruct((T, D_MODEL), F32),
        compiler_params=_cparams(("parallel",)),
        name="ffn_dense",
    )(*args)


MOE_TILE = 512
MOE_TOK_TILE = 256
SEG_ALIGN = 8
MOE_LOCAL_ROWS = 640
SEG_PIECES = (256, 128, 64, 32, 16, 8)
GAP_PIECES = tuple(b for b in SEG_PIECES if b < MOE_TILE)
TOP_K = 2


def _moe_offsets(ridx, T):
    nt = T // MOE_TOK_TILE
    e12 = ridx[:, :TOP_K]
    onehot = (e12[:, :, None] == jnp.arange(N_EXPERTS, dtype=jnp.int32)[None, None, :]).astype(jnp.int32)
    cnt = onehot.sum(axis=1).reshape(nt, MOE_TOK_TILE, N_EXPERTS).sum(axis=1)
    pc = ((cnt + SEG_ALIGN - 1) // SEG_ALIGN) * SEG_ALIGN
    loff = jnp.cumsum(pc, axis=1) - pc
    tot = pc.sum(axis=0)
    grp = ((tot + MOE_TILE - 1) // MOE_TILE) * MOE_TILE
    gend = jnp.cumsum(grp)
    gstart = gend - grp
    goff = gstart[None, :] + jnp.cumsum(pc, axis=0) - pc
    n_tiles = -(-(TOP_K * T + (SEG_ALIGN - 1) * N_EXPERTS * nt) // MOE_TILE) + N_EXPERTS
    tile_start = jnp.arange(n_tiles, dtype=jnp.int32) * MOE_TILE
    tile_exp = jnp.minimum(jnp.sum((tile_start[:, None] >= gend[None, :]).astype(jnp.int32), axis=1),
                           N_EXPERTS - 1)
    tile_rows = jnp.clip((gstart + tot)[tile_exp] - tile_start, 0, MOE_TILE)
    flat = lambda t: t.reshape(-1).astype(jnp.int32)
    used_tiles = gend[-1] // MOE_TILE
    gaps = jnp.concatenate([gstart + tot, grp - tot, jnp.stack([used_tiles, n_tiles - used_tiles])])
    return (flat(loff), flat(goff), flat(pc), flat(gaps), tile_exp.astype(jnp.int32),
            tile_rows.astype(jnp.int32), n_tiles)


def _local_positions(ridx, loff_ref, j):
    rows = ridx.shape[0]
    lane = lax.broadcasted_iota(jnp.int32, (rows, LANES), 1)
    oh0 = lane == ridx[:, 0:1]
    oh1 = lane == ridx[:, 1:2]
    ti = lax.broadcasted_iota(jnp.int32, (rows, rows), 0)
    tj = lax.broadcasted_iota(jnp.int32, (rows, rows), 1)
    before = jnp.where(tj < ti, 1.0, 0.0).astype(BF16)
    f0 = jnp.where(oh0, 1.0, 0.0)
    f1 = jnp.where(oh1, 1.0, 0.0)
    pre0 = jnp.dot(before, f0.astype(BF16), preferred_element_type=F32)
    pre1 = jnp.dot(before, f1.astype(BF16), preferred_element_type=F32)
    c0 = jnp.sum(f0, axis=0, keepdims=True)
    lane1 = lax.broadcasted_iota(jnp.int32, (1, LANES), 1)
    loff = jnp.zeros((1, LANES), F32)
    for e in range(N_EXPERTS):
        loff = jnp.where(lane1 == e, loff_ref[j * N_EXPERTS + e].astype(F32), loff)
    pos0 = jnp.sum(jnp.where(oh0, loff + pre0, 0.0), axis=1, keepdims=True)
    pos1 = jnp.sum(jnp.where(oh1, loff + c0 + pre1, 0.0), axis=1, keepdims=True)
    return pos0, pos1


def _segment_copies(j, pc_ref, loff_ref, goff_ref, local_ref, hbm_ref, sem, to_hbm):
    out = []
    for e in range(N_EXPERTS):
        n = pc_ref[j * N_EXPERTS + e]
        lo = loff_ref[j * N_EXPERTS + e]
        go = goff_ref[j * N_EXPERTS + e]
        for b in SEG_PIECES:
            done = n & ~(2 * b - 1)
            loc = local_ref.at[pl.ds(pl.multiple_of(lo + done, SEG_ALIGN), b), :]
            glob = hbm_ref.at[pl.ds(pl.multiple_of(go + done, SEG_ALIGN), b), :]
            cp = pltpu.make_async_copy(loc, glob, sem) if to_hbm else pltpu.make_async_copy(glob, loc, sem)
            out.append(((n & b) != 0, cp))
    return out


def _start_segments(*args):
    for cond, cp in _segment_copies(*args):
        pl.when(cond)(cp.start)


def _wait_segments(j, pc_ref, loff_ref, goff_ref, local_ref, hbm_ref, sem, to_hbm):
    del goff_ref
    last = j * N_EXPERTS + N_EXPERTS - 1
    total = loff_ref[last] + pc_ref[last]
    for b in (512,) + SEG_PIECES:
        loc = local_ref.at[pl.ds(0, b), :]
        glob = hbm_ref.at[pl.ds(0, b), :]
        cp = pltpu.make_async_copy(loc, glob, sem) if to_hbm else pltpu.make_async_copy(glob, loc, sem)
        pl.when((total & b) != 0)(cp.wait)


def _gap_copies(gaps_ref, zero_ref, xs_hbm, sem):
    out = []
    for e in range(N_EXPERTS):
        start = gaps_ref[e]
        n = gaps_ref[N_EXPERTS + e]
        for b in GAP_PIECES:
            done = n & ~(2 * b - 1)
            dst = xs_hbm.at[pl.ds(pl.multiple_of(start + done, SEG_ALIGN), b), :]
            out.append(((n & b) != 0, pltpu.make_async_copy(zero_ref.at[pl.ds(0, b), :], dst, sem)))
    return out


def _zero_fill_gaps(gaps_ref, zero_ref, xs_hbm, sem):
    zero_ref[...] = jnp.zeros(zero_ref.shape, F32)
    first_tile = gaps_ref[2 * N_EXPERTS]
    n_tail = gaps_ref[2 * N_EXPERTS + 1]

    def tail_copy(i):
        row0 = pl.multiple_of((first_tile + i) * MOE_TILE, MOE_TILE)
        return pltpu.make_async_copy(zero_ref, xs_hbm.at[pl.ds(row0, MOE_TILE), :], sem)

    for cond, cp in _gap_copies(gaps_ref, zero_ref, xs_hbm, sem):
        pl.when(cond)(cp.start)
    lax.fori_loop(0, n_tail, lambda i, c: (tail_copy(i).start(), c)[1], 0)
    for cond, cp in _gap_copies(gaps_ref, zero_ref, xs_hbm, sem):
        pl.when(cond)(cp.wait)
    lax.fori_loop(0, n_tail, lambda i, c: (tail_copy(i).wait(), c)[1], 0)


def _moe_dispatch_kernel(loff_ref, goff_ref, pc_ref, gaps_ref, ridx_ref, h_ref, xs_hbm, local_ref, zero_ref,
                         sem, zsem):
    j = pl.program_id(0)
    n = pl.num_programs(0)
    slot = j % 2

    @pl.when(j == 0)
    def _gaps():
        _zero_fill_gaps(gaps_ref, zero_ref, xs_hbm, zsem.at[0])

    pos0, pos1 = _local_positions(ridx_ref[...], loff_ref, j)
    col = lax.broadcasted_iota(jnp.int32, (MOE_TOK_TILE, MOE_LOCAL_ROWS), 1).astype(F32)
    place = jnp.where((col == pos0) | (col == pos1), 1.0, 0.0).astype(BF16)
    xs = lax.dot_general(place, h_ref[...], (((0,), (0,)), ((), ())), preferred_element_type=F32)

    def seg(jj, s):
        return (jj, pc_ref, loff_ref, goff_ref, local_ref.at[s], xs_hbm, sem.at[s], True)

    @pl.when(j >= 2)
    def _reuse():
        _wait_segments(*seg(j - 2, slot))

    local_ref[slot] = xs
    _start_segments(*seg(j, slot))

    @pl.when(j == n - 1)
    def _drain():
        _wait_segments(*seg(j, slot))

        @pl.when(n >= 2)
        def _():
            _wait_segments(*seg(j - 1, 1 - slot))


def _moe_dispatch(ridx, h, loff, goff, pc, gaps, n_rows):
    T = h.shape[0]
    grid_spec = pltpu.PrefetchScalarGridSpec(
        num_scalar_prefetch=4,
        grid=(T // MOE_TOK_TILE,),
        in_specs=[pl.BlockSpec((MOE_TOK_TILE, LANES), lambda j, *_: (j, 0)),
                  pl.BlockSpec((MOE_TOK_TILE, D_MODEL), lambda j, *_: (j, 0))],
        out_specs=pl.BlockSpec(memory_space=pl.ANY),
        scratch_shapes=[pltpu.VMEM((2, MOE_LOCAL_ROWS, D_MODEL), F32), pltpu.VMEM((MOE_TILE, D_MODEL), F32),
                        pltpu.SemaphoreType.DMA((2,)), pltpu.SemaphoreType.DMA((1,))])
    return pl.pallas_call(
        _moe_dispatch_kernel,
        grid_spec=grid_spec,
        out_shape=jax.ShapeDtypeStruct((n_rows, D_MODEL), F32),
        compiler_params=_cparams(("arbitrary",)),
        name="moe_dispatch",
    )(loff, goff, pc, gaps, ridx, h)


def _moe_expert_kernel(texp_ref, trows_ref, xs_ref, wg_ref, wu_ref, wd_ref, ys_ref):
    del texp_ref
    rows = trows_ref[pl.program_id(0)]

    @pl.when(rows > 0)
    def _compute():
        x = xs_ref[...].astype(BF16)
        gate = jnp.dot(x, wg_ref[0].astype(BF16), preferred_element_type=F32)
        up = jnp.dot(x, wu_ref[0].astype(BF16), preferred_element_type=F32)
        act = (gate * _sigmoid(gate) * up).astype(BF16)
        ys_ref[...] = jnp.dot(act, wd_ref[0].astype(BF16), preferred_element_type=F32)

    @pl.when(rows == 0)
    def _skip():
        ys_ref[...] = jnp.zeros(ys_ref.shape, F32)


def _moe_experts(xs, tile_exp, tile_rows, wg, wu, wd):
    n_tiles = tile_exp.shape[0]
    _, _, F = wg.shape
    grid_spec = pltpu.PrefetchScalarGridSpec(
        num_scalar_prefetch=2,
        grid=(n_tiles,),
        in_specs=[pl.BlockSpec((MOE_TILE, D_MODEL), lambda i, te, tr: (i, 0)),
                  pl.BlockSpec((1, D_MODEL, F), lambda i, te, tr: (te[i], 0, 0)),
                  pl.BlockSpec((1, D_MODEL, F), lambda i, te, tr: (te[i], 0, 0)),
                  pl.BlockSpec((1, F, D_MODEL), lambda i, te, tr: (te[i], 0, 0))],
        out_specs=pl.BlockSpec((MOE_TILE, D_MODEL), lambda i, te, tr: (i, 0)))
    return pl.pallas_call(
        _moe_expert_kernel,
        grid_spec=grid_spec,
        out_shape=jax.ShapeDtypeStruct((n_tiles * MOE_TILE, D_MODEL), F32),
        compiler_params=_cparams(("arbitrary",)),
        name="moe_experts",
    )(tile_exp, tile_rows, xs, wg, wu, wd)


def _moe_combine_kernel(loff_ref, goff_ref, pc_ref, ridx_ref, rgate_ref, x_ref, ys_hbm, *rest, final):
    if final:
        gf_ref, o_ref, local_ref, sem = rest
    else:
        o_ref, local_ref, sem = rest
    j = pl.program_id(0)
    n = pl.num_programs(0)
    slot = j % 2

    def seg(jj, s):
        return (jj, pc_ref, loff_ref, goff_ref, local_ref.at[s], ys_hbm, sem.at[s], False)

    @pl.when(j == 0)
    def _prologue():
        _start_segments(*seg(j, slot))

    @pl.when(j + 1 < n)
    def _prefetch():
        _start_segments(*seg(j + 1, 1 - slot))

    pos0, pos1 = _local_positions(ridx_ref[...], loff_ref, j)
    col = lax.broadcasted_iota(jnp.int32, (MOE_TOK_TILE, MOE_LOCAL_ROWS), 1).astype(F32)
    pick0 = jnp.where(col == pos0, 1.0, 0.0).astype(BF16)
    pick1 = jnp.where(col == pos1, 1.0, 0.0).astype(BF16)
    used = loff_ref[j * N_EXPERTS + N_EXPERTS - 1] + pc_ref[j * N_EXPERTS + N_EXPERTS - 1]
    _wait_segments(*seg(j, slot))
    ri = lax.broadcasted_iota(jnp.int32, (MOE_LOCAL_ROWS, D_MODEL), 0)
    ys = jnp.where(ri < used, local_ref[slot], 0.0).astype(BF16)
    y0 = jnp.dot(pick0, ys, preferred_element_type=F32)
    y1 = jnp.dot(pick1, ys, preferred_element_type=F32)
    gates = rgate_ref[...]
    xo = x_ref[...] + gates[:, 0:1] * y0 + gates[:, 1:2] * y1
    o_ref[...] = _rms(xo, gf_ref[...]) if final else xo


def _moe_combine(x2, ys, ridx, rgate, loff, goff, pc, final_g):
    T = x2.shape[0]
    final = final_g is not None
    in_specs = [pl.BlockSpec((MOE_TOK_TILE, LANES), lambda j, *_: (j, 0)),
                pl.BlockSpec((MOE_TOK_TILE, LANES), lambda j, *_: (j, 0)),
                pl.BlockSpec((MOE_TOK_TILE, D_MODEL), lambda j, *_: (j, 0)),
                pl.BlockSpec(memory_space=pl.ANY)]
    args = [ridx, rgate, x2, ys]
    if final:
        in_specs.append(pl.BlockSpec((1, D_MODEL), lambda j, *_: (0, 0)))
        args.append(final_g)
    grid_spec = pltpu.PrefetchScalarGridSpec(
        num_scalar_prefetch=3,
        grid=(T // MOE_TOK_TILE,),
        in_specs=in_specs,
        out_specs=pl.BlockSpec((MOE_TOK_TILE, D_MODEL), lambda j, *_: (j, 0)),
        scratch_shapes=[pltpu.VMEM((2, MOE_LOCAL_ROWS, D_MODEL), F32), pltpu.SemaphoreType.DMA((2,))])
    return pl.pallas_call(
        functools.partial(_moe_combine_kernel, final=final),
        grid_spec=grid_spec,
        out_shape=jax.ShapeDtypeStruct((T, D_MODEL), F32),
        compiler_params=_cparams(("arbitrary",)),
        name="moe_combine",
    )(loff, goff, pc, *args)


def _moe(h, x2, ridx, rgate, wg, wu, wd, final_g):
    T = x2.shape[0]
    loff, goff, pc, gaps, tile_exp, tile_rows, n_tiles = _moe_offsets(ridx, T)
    xs = _moe_dispatch(ridx, h, loff, goff, pc, gaps, n_tiles * MOE_TILE)
    ys = _moe_experts(xs, tile_exp, tile_rows, wg, wu, wd)
    return _moe_combine(x2, ys, ridx, rgate, loff, goff, pc, final_g)


def kernel(x, norm_mix_g, w_in, shift_mu, rwkv_w0, rwkv_w2, rwkv_a0, rwkv_a2, rwkv_g2, rwkv_k_k, rwkv_k_a, rwkv_r_k, rwkv_ln_w, rwkv_ln_b, attn_rel_bias, attn_norm_g, w_out, norm_ffn_g, ffn_w_gate, ffn_w_up, ffn_w_down, moe_router, moe_w_gate, moe_w_up, moe_w_down, norm_final_g):
    B, S, D = x.shape
    depth = w_in.shape[0]
    T = B * S
    tm = min(512, T)
    row = lambda t: t.reshape(1, -1).astype(F32)

    hi = jnp.arange(LANES) // HEAD_DIM
    bd = (hi[:, None] == hi[None, :]).astype(BF16)
    rbt = jnp.pad(jnp.swapaxes(attn_rel_bias, 1, 2).astype(F32),
                  ((0, 0), (0, 0), (0, REL_ROWS - attn_rel_bias.shape[1])))
    bias_tabs = _bias_tables(rbt)

    x2 = x.reshape(T, D).astype(F32)
    for l in range(depth):
        ps, qkv = _inproj(x2, row(norm_mix_g[l]), w_in.astype(F32), l, tm)

        zeros = jnp.zeros((DECAY_LORA, RWKV_WIDTH), F32)
        wa = jnp.concatenate([jnp.concatenate([rwkv_w2[l], zeros], axis=1),
                              jnp.concatenate([zeros, rwkv_a2[l]], axis=1)], axis=0).astype(BF16)
        vec = jnp.stack([rwkv_w0[l], rwkv_a0[l], rwkv_k_k[l], rwkv_k_a[l], rwkv_r_k[l],
                         rwkv_ln_w[l], rwkv_ln_b[l], jnp.zeros_like(rwkv_w0[l])]).astype(F32)
        y_rwkv = _rwkv(ps.reshape(B, S, SHIFT_COLS), row(shift_mu[l]), vec, wa,
                       rwkv_g2[l].astype(BF16), bd)
        y_attn = _attn(qkv.reshape(B, S, 3 * ATTN_WIDTH), bias_tabs, l, row(attn_norm_g[l]))

        li = l // 2
        final_g = row(norm_final_g) if l == depth - 1 else None
        mixed = (y_rwkv.reshape(T, RWKV_WIDTH), y_attn.reshape(T, ATTN_WIDTH), x2,
                 w_out[l].astype(BF16), row(norm_ffn_g[l]))
        if l % 2 == 1:
            router = jnp.pad(moe_router[li].astype(F32), ((0, 0), (0, LANES - N_EXPERTS)))
            x_mid, h, ridx, rgate = _outproj(*mixed, router, tm)
            x2 = _moe(h, x_mid, ridx, rgate, moe_w_gate[li].astype(F32), moe_w_up[li].astype(F32),
                      moe_w_down[li].astype(F32), final_g)
        else:
            x2 = _ffn(*mixed, ffn_w_gate[li].astype(BF16), ffn_w_up[li].astype(BF16),
                      ffn_w_down[li].astype(BF16), final_g, tm)
    return x2.reshape(B, S, D).astype(x.dtype)
```

```python
import functools

import jax
import jax.numpy as jnp
from jax import lax
from jax.experimental import pallas as pl
from jax.experimental.pallas import tpu as pltpu

F32 = jnp.float32
BF16 = jnp.bfloat16

D_MODEL = 1024
CHUNK = 64
N_LEFT_CHUNKS = 8
HEAD_DIM = 64
RWKV_WIDTH = 512
ATTN_WIDTH = 512
DECAY_LORA = 64
AAA_LORA = 64
GATE_LORA = 128
REL_CLIP = 128
N_EXPERTS = 8
RMS_EPS = 1e-6
GN_EPS = 64e-5
MASK_VALUE = -1e30
SHIFT_COLS = 3 * RWKV_WIDTH + DECAY_LORA + AAA_LORA + GATE_LORA

LANES = 128
PAIR = 2 * CHUNK
N_PAIRS = RWKV_WIDTH // LANES
ATTN_WINDOW = (N_LEFT_CHUNKS + 2) * CHUNK
N_BIAS_TABLES = N_LEFT_CHUNKS + 2
BIAS_BASE = 768
REL_ROWS = 384
VMEM_LIMIT = 56 * 1024 * 1024


def _cparams(sem):
    return pltpu.CompilerParams(dimension_semantics=sem, vmem_limit_bytes=VMEM_LIMIT)


def _mm(a, b):
    return jnp.dot(a.astype(BF16), b.astype(BF16), preferred_element_type=F32)


def _mm_nt(a, b):
    return lax.dot_general(a.astype(BF16), b.astype(BF16), (((1,), (1,)), ((), ())),
                           preferred_element_type=F32)


def _mm_tn(a, b):
    return lax.dot_general(a.astype(BF16), b.astype(BF16), (((0,), (0,)), ((), ())),
                           preferred_element_type=F32)


def _split_terms(x, n):
    terms, rem = [], x
    for _ in range(n):
        hi = rem.astype(BF16)
        terms.append(hi)
        rem = rem - hi.astype(F32)
    return terms


def _dot_exact_rhs(x, w_bf16, n):
    acc = None
    for t in _split_terms(x, n):
        d = jnp.dot(t, w_bf16, preferred_element_type=F32)
        acc = d if acc is None else acc + d
    return acc


def _dot_exact_lhs(w_bf16, x, n):
    acc = None
    for t in _split_terms(x, n):
        d = jnp.dot(w_bf16, t, preferred_element_type=F32)
        acc = d if acc is None else acc + d
    return acc


def _dot_f32(a, b):
    a1, a2 = _split_terms(a, 2)
    b1, b2 = _split_terms(b, 2)
    n = b.shape[1]
    t = jnp.dot(a1, jnp.concatenate([b1, b2], axis=1), preferred_element_type=F32)
    return t[:, :n] + t[:, n:] + jnp.dot(a2, b1, preferred_element_type=F32)


def _sigmoid(x):
    return 1.0 / (1.0 + jnp.exp(-x))


def _rms(x, g):
    return x * lax.rsqrt(jnp.mean(x * x, axis=-1, keepdims=True) + RMS_EPS) * g


def _inproj_kernel(x_ref, g_ref, w_ref, ps_ref, qkv_ref):
    hb = _rms(x_ref[...], g_ref[...]).astype(BF16)
    ps_ref[...] = jnp.dot(hb, w_ref[0, :, :SHIFT_COLS].astype(BF16), preferred_element_type=F32)
    qkv_ref[...] = jnp.dot(hb, w_ref[0, :, SHIFT_COLS:].astype(BF16), preferred_element_type=F32).astype(BF16)


def _inproj(x2, g, w_in, l, tm):
    T = x2.shape[0]
    n_in = w_in.shape[2]
    na = n_in - SHIFT_COLS
    return pl.pallas_call(
        _inproj_kernel,
        grid=(T // tm,),
        in_specs=[pl.BlockSpec((tm, D_MODEL), lambda i: (i, 0)),
                  pl.BlockSpec((1, D_MODEL), lambda i: (0, 0)),
                  pl.BlockSpec((1, D_MODEL, n_in), lambda i: (l, 0, 0))],
        out_specs=[pl.BlockSpec((tm, SHIFT_COLS), lambda i: (i, 0)),
                   pl.BlockSpec((tm, na), lambda i: (i, 0))],
        out_shape=[jax.ShapeDtypeStruct((T, SHIFT_COLS), F32),
                   jax.ShapeDtypeStruct((T, na), BF16)],
        compiler_params=_cparams(("parallel",)),
        name="inproj",
    )(x2, g, w_in)


def _rwkv_kernel(ps_ref, prev_ref, mu_ref, vec_ref, wa_ref, g2_ref, bd_ref, y_ref, h_ref):
    c = pl.program_id(1)

    @pl.when(c == 0)
    def _init():
        h_ref[...] = jnp.zeros(h_ref.shape, F32)

    p = ps_ref[0]
    last = jnp.where(c > 0, prev_ref[0][7:8, :], 0.0)
    row = lax.broadcasted_iota(jnp.int32, p.shape, 0)
    prev = jnp.where(row == 0, last, pltpu.roll(p, 1, 0))
    xs = p + mu_ref[...] * (prev - p)

    W = RWKV_WIDTH
    r, k, v = xs[:, 0:W], xs[:, W:2 * W], xs[:, 2 * W:3 * W]
    z0 = xs[:, 3 * W:3 * W + LANES]
    gd = xs[:, 3 * W + LANES:3 * W + 2 * LANES]
    m1 = lax.broadcasted_iota(jnp.int32, (CHUNK, LANES), 1) < HEAD_DIM
    z0 = jnp.where(lax.broadcasted_iota(jnp.int32, z0.shape, 1) < DECAY_LORA, jnp.tanh(z0), z0)
    lora = _mm(z0, wa_ref[...])
    vec = vec_ref[...]
    w0, a0, k_k, k_a, r_k, ln_w, ln_b = (vec[i:i + 1] for i in range(7))
    w = w0 + lora[:, :W]
    a = _sigmoid(a0 + lora[:, W:])
    g = _mm(_sigmoid(gd), g2_ref[...])
    softplus_neg_w = jnp.maximum(-w, 0.0) + jnp.log(1.0 + jnp.exp(-jnp.abs(w)))
    lw = -jnp.exp(-softplus_neg_w - 0.5)

    bd = bd_ref[...]

    def head_sums(x):
        xb = x.astype(BF16)
        return jnp.concatenate(
            [jnp.dot(xb[:, LANES * i:LANES * (i + 1)], bd, preferred_element_type=F32) for i in range(N_PAIRS)],
            axis=1)

    kk = k * k_k
    kk = kk / jnp.maximum(jnp.sqrt(head_sums(kk * kk)), 1e-12)
    k2 = k * (1.0 + (a - 1.0) * k_a)
    kka = kk * a

    rows = p.shape[0]
    n_chunks = rows // CHUNK
    ti = lax.broadcasted_iota(jnp.int32, (rows, rows), 0)
    tj = lax.broadcasted_iota(jnp.int32, (rows, rows), 1)
    tri = jnp.where((ti >= tj) & ((ti >> 6) == (tj >> 6)), 1.0, 0.0).astype(BF16)
    L = _dot_exact_lhs(tri, lw, 2)

    ri = lax.broadcasted_iota(jnp.int32, (PAIR, PAIR), 0)
    ci = lax.broadcasted_iota(jnp.int32, (PAIR, PAIR), 1)
    same_head = (ri >> 6) == (ci >> 6)
    strict = same_head & (ri > ci)
    incl = same_head & (ri >= ci)
    eye = ri == ci
    eye_f = jnp.where(eye, 1.0, 0.0)

    units = []
    for j in range(n_chunks):
        rs = slice(CHUNK * j, CHUNK * (j + 1))
        Lj, lwj = L[rs], lw[rs]
        Lc = Lj[CHUNK - 1:CHUNK]
        inv = jnp.exp(-Lj)
        to_end = jnp.exp(Lc - Lj)
        gC = jnp.exp(Lc)
        Rt = r[rs] * jnp.exp(Lj)
        At = -kk[rs] * jnp.exp(Lj - lwj)
        Bt, Kt = kka[rs] * inv, k2[rs] * inv
        Bh, Kh = kka[rs] * to_end, k2[rs] * to_end
        vj = v[rs]
        for pi in range(N_PAIRS):
            sl = slice(LANES * pi, LANES * (pi + 1))

            def stack(x):
                xp = x[:, sl]
                return jnp.concatenate([jnp.where(m1, xp, 0.0), jnp.where(m1, 0.0, xp)], axis=0)

            units.append(dict(
                j=j, pi=pi, gC=gC[:, sl], sRt=stack(Rt),
                sAt=stack(At).astype(BF16), sV=stack(vj).astype(BF16),
                sBt=stack(Bt).astype(BF16), sKt=stack(Kt).astype(BF16),
                sBh=stack(Bh).astype(BF16), sKh=stack(Kh).astype(BF16)))

    for u in units:
        big = _mm_nt(jnp.concatenate([u["sAt"], u["sRt"].astype(BF16)], axis=0),
                     jnp.concatenate([u["sBt"], u["sKt"]], axis=0))
        u["AB"] = jnp.where(strict, big[:PAIR, :PAIR], 0.0)
        u["AK"] = jnp.where(strict, big[:PAIR, PAIR:], 0.0)
        u["RB"] = jnp.where(incl, big[PAIR:, :PAIR], 0.0)
        u["RK"] = jnp.where(incl, big[PAIR:, PAIR:], 0.0)
    for u in units:
        u["X"] = eye_f + u["AB"]
        u["Pw"] = _mm(u["AB"], u["AB"])
        u["W1"] = _mm(u["AK"], u["sV"])
    for _ in range(4):
        for u in units:
            PX = _mm(u["Pw"], jnp.concatenate([u["Pw"], u["X"]], axis=1))
            u["Pw"] = PX[:, :PAIR]
            u["X"] = u["X"] + PX[:, PAIR:]
    for u in units:
        u["Tm"] = u["X"] + _mm(u["Pw"], u["X"])
    for u in units:
        u["PQ"] = _mm(u["Tm"], jnp.concatenate([u["sAt"], u["W1"].astype(BF16)], axis=1)).astype(BF16)
    for u in units:
        PQ = u["PQ"]
        Pm, Q = PQ[:, :PAIR], PQ[:, PAIR:]
        RBPQ = _mm(u["RB"], PQ)
        u["Rp"] = u["sRt"] + RBPQ[:, :PAIR]
        u["Y0"] = RBPQ[:, PAIR:] + _mm(u["RK"], u["sV"])
        u["Mm"] = jnp.where(eye, u["gC"], 0.0) + _mm_tn(u["sBh"], Pm)
        u["G"] = _mm_tn(jnp.concatenate([u["sBh"], u["sKh"]], axis=0),
                        jnp.concatenate([Q, u["sV"]], axis=0))
    H = [h_ref[pi] for pi in range(N_PAIRS)]
    y_rows = []
    for j in range(n_chunks):
        ys = []
        for u in units[j * N_PAIRS:(j + 1) * N_PAIRS]:
            pi = u["pi"]
            YH = _mm(jnp.concatenate([u["Rp"], u["Mm"]], axis=0), H[pi])
            Ysm = YH[:PAIR] + u["Y0"]
            H[pi] = YH[PAIR:] + u["G"]
            ys.append(Ysm[:CHUNK] + Ysm[CHUNK:])
        y_rows.append(jnp.concatenate(ys, axis=1))
    for pi in range(N_PAIRS):
        h_ref[pi] = H[pi]
    y = y_rows[0] if n_chunks == 1 else jnp.concatenate(y_rows, axis=0)

    inv_n = 1.0 / HEAD_DIM
    mean = head_sums(y) * inv_n
    d = y - mean
    var = head_sums(d * d) * inv_n
    yn = d * lax.rsqrt(var + GN_EPS) * ln_w + ln_b
    bonus = head_sums(r * k2 * r_k) * v
    y_ref[0] = ((yn + bonus) * g).astype(BF16)


RWKV_BLOCK_CHUNKS = 4


def _rwkv(ps3, mu, vec, wa, g2, bd):
    B, S, _ = ps3.shape
    rows = RWKV_BLOCK_CHUNKS * CHUNK
    nc = S // rows
    rows8 = rows // 8
    return pl.pallas_call(
        _rwkv_kernel,
        grid=(B, nc),
        in_specs=[pl.BlockSpec((1, rows, SHIFT_COLS), lambda b, c: (b, c, 0)),
                  pl.BlockSpec((1, 8, SHIFT_COLS), lambda b, c: (b, jnp.maximum(c * rows8 - 1, 0), 0)),
                  pl.BlockSpec((1, SHIFT_COLS), lambda b, c: (0, 0)),
                  pl.BlockSpec((8, RWKV_WIDTH), lambda b, c: (0, 0)),
                  pl.BlockSpec((LANES, 2 * RWKV_WIDTH), lambda b, c: (0, 0)),
                  pl.BlockSpec((GATE_LORA, RWKV_WIDTH), lambda b, c: (0, 0)),
                  pl.BlockSpec((LANES, LANES), lambda b, c: (0, 0))],
        out_specs=pl.BlockSpec((1, rows, RWKV_WIDTH), lambda b, c: (b, c, 0)),
        out_shape=jax.ShapeDtypeStruct((B, S, RWKV_WIDTH), BF16),
        scratch_shapes=[pltpu.VMEM((N_PAIRS, PAIR, LANES), F32)],
        compiler_params=_cparams(("parallel", "arbitrary")),
        name="rwkv7",
    )(ps3, ps3, mu, vec, wa, g2, bd)


def _bias_kernel(rbt_ref, o_ref):
    e = pl.program_id(1)
    xi = lax.broadcasted_iota(jnp.int32, (REL_ROWS, BIAS_BASE), 1)
    ji = lax.broadcasted_iota(jnp.int32, (REL_ROWS, BIAS_BASE), 0)
    off = jnp.where(xi < BIAS_BASE - CHUNK, xi, xi - BIAS_BASE)
    idx = jnp.clip(e * CHUNK - off, -REL_CLIP, REL_CLIP) + REL_CLIP
    onehot = jnp.where(idx == ji, 1.0, 0.0).astype(BF16)
    base = _dot_exact_rhs(rbt_ref[0], onehot, 3)
    kj = lax.broadcasted_iota(jnp.int32, (CHUNK, ATTN_WINDOW), 1)
    kc = kj >> 6
    valid = (kc <= e) & (kc >= e - N_LEFT_CHUNKS)
    for h in range(8):
        rows = jnp.broadcast_to(base[h:h + 1, :], (CHUNK, BIAS_BASE))
        toep = pltpu.roll(rows, 0, 1, stride=1, stride_axis=0)
        o_ref[0, 0, h * CHUNK:(h + 1) * CHUNK, :] = jnp.where(valid, toep[:, :ATTN_WINDOW], MASK_VALUE)


def _bias_tables(rbt):
    L = rbt.shape[0]
    return pl.pallas_call(
        _bias_kernel,
        grid=(L, N_BIAS_TABLES),
        in_specs=[pl.BlockSpec((1, 8, REL_ROWS), lambda l, e: (l, 0, 0))],
        out_specs=pl.BlockSpec((1, 1, 8 * CHUNK, ATTN_WINDOW), lambda l, e: (l, e, 0, 0)),
        out_shape=jax.ShapeDtypeStruct((L, N_BIAS_TABLES, 8 * CHUNK, ATTN_WINDOW), F32),
        compiler_params=_cparams(("parallel", "parallel")),
        name="bias_tables",
    )(rbt)


ATTN_BLOCK_CHUNKS = 8


def _attn_kernel(q_ref, k_ref, v_ref, *rest):
    bias_refs, g_ref, o_ref = rest[:ATTN_BLOCK_CHUNKS], rest[-2], rest[-1]
    n0 = pl.program_id(1) * ATTN_BLOCK_CHUNKS
    q = q_ref[0] * jnp.asarray(HEAD_DIM ** -0.5, BF16)
    m1 = lax.broadcasted_iota(jnp.int32, (CHUNK, LANES), 1) < HEAD_DIM
    zero = jnp.zeros((), BF16)
    n_pairs = ATTN_WIDTH // LANES
    units = []
    for j in range(ATTN_BLOCK_CHUNKS):
        start = pl.multiple_of(jnp.maximum(n0 + j - (N_LEFT_CHUNKS + 1), 0) * CHUNK, CHUNK)
        kw = k_ref[0, pl.ds(start, ATTN_WINDOW), :]
        vw = v_ref[0, pl.ds(start, ATTN_WINDOW), :]
        qj = q[CHUNK * j:CHUNK * (j + 1)]
        for pi in range(n_pairs):
            sl = slice(LANES * pi, LANES * (pi + 1))
            qp = qj[:, sl]
            qs = jnp.concatenate([jnp.where(m1, qp, zero), jnp.where(m1, zero, qp)], axis=0)
            units.append(dict(qs=qs, k=kw[:, sl], v=vw[:, sl],
                              bias=bias_refs[j][0, 0, PAIR * pi:PAIR * (pi + 1), :]))
    for u in units:
        u["s"] = lax.dot_general(u["qs"], u["k"], (((1,), (1,)), ((), ())),
                                 preferred_element_type=F32) + u["bias"]
    for u in units:
        s = u["s"]
        ex = jnp.exp(s - jnp.max(s, axis=1, keepdims=True))
        u["den"] = jnp.sum(ex, axis=1, keepdims=True)
        u["ex"] = ex.astype(BF16)
    for u in units:
        o = jnp.dot(u["ex"], u["v"], preferred_element_type=F32) / u["den"]
        u["o"] = jnp.where(m1, o[:CHUNK], o[CHUNK:])
    rows = [jnp.concatenate([u["o"] for u in units[j * n_pairs:(j + 1) * n_pairs]], axis=1)
            for j in range(ATTN_BLOCK_CHUNKS)]
    o = jnp.concatenate(rows, axis=0)
    o_ref[0] = _rms(o, g_ref[...]).astype(BF16)


def _attn(qkv3, bias_l, l, g):
    B, S, _ = qkv3.shape
    rows = ATTN_BLOCK_CHUNKS * CHUNK
    nc = S // rows

    def bias_spec(j):
        return pl.BlockSpec(
            (1, 1, 8 * CHUNK, ATTN_WINDOW),
            lambda b, n: (l, jnp.minimum(n * ATTN_BLOCK_CHUNKS + j, N_BIAS_TABLES - 1), 0, 0))

    return pl.pallas_call(
        _attn_kernel,
        grid=(B, nc),
        in_specs=[pl.BlockSpec((1, rows, ATTN_WIDTH), lambda b, n: (b, n, 0)),
                  pl.BlockSpec((1, S, ATTN_WIDTH), lambda b, n: (b, 0, 1)),
                  pl.BlockSpec((1, S, ATTN_WIDTH), lambda b, n: (b, 0, 2))]
                 + [bias_spec(j) for j in range(ATTN_BLOCK_CHUNKS)]
                 + [pl.BlockSpec((1, ATTN_WIDTH), lambda b, n: (0, 0))],
        out_specs=pl.BlockSpec((1, rows, ATTN_WIDTH), lambda b, n: (b, n, 0)),
        out_shape=jax.ShapeDtypeStruct((B, S, ATTN_WIDTH), BF16),
        compiler_params=_cparams(("parallel", "arbitrary")),
        name="chunk_attn",
    )(qkv3, qkv3, qkv3, *([bias_l] * ATTN_BLOCK_CHUNKS), g)


def _outproj_kernel(yr_ref, ya_ref, x_ref, w_ref, g_ref, router_ref, xo_ref, h_ref, ridx_ref, rgate_ref):
    y = jnp.concatenate([yr_ref[...], ya_ref[...]], axis=1)
    xn = x_ref[...] + jnp.dot(y, w_ref[...], preferred_element_type=F32)
    xo_ref[...] = xn
    h = _rms(xn, g_ref[...])
    h_ref[...] = h.astype(BF16)
    lane = lax.broadcasted_iota(jnp.int32, ridx_ref.shape, 1)
    neg = jnp.asarray(-jnp.inf, F32)
    lg = jnp.where(lane < N_EXPERTS, _dot_f32(h, router_ref[...]), neg)
    top1 = jnp.max(lg, axis=1, keepdims=True)
    idx1 = jnp.min(jnp.where(lg == top1, lane, LANES), axis=1, keepdims=True)
    lg2 = jnp.where(lane == idx1, neg, lg)
    top2 = jnp.max(lg2, axis=1, keepdims=True)
    idx2 = jnp.min(jnp.where(lg2 == top2, lane, LANES), axis=1, keepdims=True)
    ex = jnp.exp(top2 - top1)
    ridx_ref[...] = jnp.where(lane == 0, idx1, jnp.where(lane == 1, idx2, 0))
    rgate_ref[...] = jnp.where(lane == 0, 1.0 / (1.0 + ex), jnp.where(lane == 1, ex / (1.0 + ex), 0.0))


def _outproj(yr, ya, x2, w, g, router, tm):
    T = x2.shape[0]
    row_tile = lambda width: pl.BlockSpec((tm, width), lambda i: (i, 0))
    return pl.pallas_call(
        _outproj_kernel,
        grid=(T // tm,),
        in_specs=[row_tile(RWKV_WIDTH), row_tile(ATTN_WIDTH), row_tile(D_MODEL),
                  pl.BlockSpec((D_MODEL, D_MODEL), lambda i: (0, 0)),
                  pl.BlockSpec((1, D_MODEL), lambda i: (0, 0)),
                  pl.BlockSpec((D_MODEL, LANES), lambda i: (0, 0))],
        out_specs=[row_tile(D_MODEL), row_tile(D_MODEL), row_tile(LANES), row_tile(LANES)],
        out_shape=[jax.ShapeDtypeStruct((T, D_MODEL), F32), jax.ShapeDtypeStruct((T, D_MODEL), BF16),
                   jax.ShapeDtypeStruct((T, LANES), jnp.int32), jax.ShapeDtypeStruct((T, LANES), F32)],
        compiler_params=_cparams(("parallel",)),
        name="outproj_router",
    )(yr, ya, x2, w, g, router)


FF_TILE = 256


def _ffn_kernel(yr_ref, ya_ref, x_ref, wo_ref, g_ref, wg_ref, wu_ref, wd_ref, *rest, final):
    if final:
        gf_ref, o_ref = rest
    else:
        (o_ref,) = rest
    y = jnp.concatenate([yr_ref[...], ya_ref[...]], axis=1)
    xn = x_ref[...] + jnp.dot(y, wo_ref[...], preferred_element_type=F32)
    h = _rms(xn, g_ref[...]).astype(BF16)
    acc = None
    for f in range(0, wg_ref.shape[1], FF_TILE):
        gate = jnp.dot(h, wg_ref[:, f:f + FF_TILE].astype(BF16), preferred_element_type=F32)
        up = jnp.dot(h, wu_ref[:, f:f + FF_TILE].astype(BF16), preferred_element_type=F32)
        act = (gate * _sigmoid(gate) * up).astype(BF16)
        d = jnp.dot(act, wd_ref[f:f + FF_TILE, :].astype(BF16), preferred_element_type=F32)
        acc = d if acc is None else acc + d
    xo = xn + acc
    o_ref[...] = _rms(xo, gf_ref[...]) if final else xo


def _ffn(yr, ya, x2, w_out, g, wg, wu, wd, final_g, tm):
    T = x2.shape[0]
    F = wg.shape[1]
    final = final_g is not None
    in_specs = [pl.BlockSpec((tm, RWKV_WIDTH), lambda i: (i, 0)),
                pl.BlockSpec((tm, ATTN_WIDTH), lambda i: (i, 0)),
                pl.BlockSpec((tm, D_MODEL), lambda i: (i, 0)),
                pl.BlockSpec((D_MODEL, D_MODEL), lambda i: (0, 0)),
                pl.BlockSpec((1, D_MODEL), lambda i: (0, 0)),
                pl.BlockSpec((D_MODEL, F), lambda i: (0, 0), pipeline_mode=pl.Buffered(1)),
                pl.BlockSpec((D_MODEL, F), lambda i: (0, 0), pipeline_mode=pl.Buffered(1)),
                pl.BlockSpec((F, D_MODEL), lambda i: (0, 0), pipeline_mode=pl.Buffered(1))]
    args = [yr, ya, x2, w_out, g, wg, wu, wd]
    if final:
        in_specs.append(pl.BlockSpec((1, D_MODEL), lambda i: (0, 0)))
        args.append(final_g)
    return pl.pallas_call(
        functools.partial(_ffn_kernel, final=final),
        grid=(T // tm,),
        in_specs=in_specs,
        out_specs=pl.BlockSpec((tm, D_MODEL), lambda i: (i, 0)),
        out_shape=jax.ShapeDtypeStruct((T, D_MODEL), F32),
        compiler_params=_cparams(("parallel",)),
        name="ffn_dense",
    )(*args)


MOE_TILE = 512
MOE_TOK_TILE = 256
SEG_ALIGN = 8
MOE_LOCAL_ROWS = 640
SEG_PIECES = (256, 128, 64, 32, 16, 8)
GAP_PIECES = tuple(b for b in SEG_PIECES if b < MOE_TILE)
TOP_K = 2


def _moe_offsets(ridx, T):
    nt = T // MOE_TOK_TILE
    e12 = ridx[:, :TOP_K]
    onehot = (e12[:, :, None] == jnp.arange(N_EXPERTS, dtype=jnp.int32)[None, None, :]).astype(jnp.int32)
    cnt = onehot.sum(axis=1).reshape(nt, MOE_TOK_TILE, N_EXPERTS).sum(axis=1)
    pc = ((cnt + SEG_ALIGN - 1) // SEG_ALIGN) * SEG_ALIGN
    loff = jnp.cumsum(pc, axis=1) - pc
    tot = pc.sum(axis=0)
    grp = ((tot + MOE_TILE - 1) // MOE_TILE) * MOE_TILE
    gend = jnp.cumsum(grp)
    gstart = gend - grp
    goff = gstart[None, :] + jnp.cumsum(pc, axis=0) - pc
    n_tiles = -(-(TOP_K * T + (SEG_ALIGN - 1) * N_EXPERTS * nt) // MOE_TILE) + N_EXPERTS
    tile_start = jnp.arange(n_tiles, dtype=jnp.int32) * MOE_TILE
    tile_exp = jnp.minimum(jnp.sum((tile_start[:, None] >= gend[None, :]).astype(jnp.int32), axis=1),
                           N_EXPERTS - 1)
    tile_rows = jnp.clip((gstart + tot)[tile_exp] - tile_start, 0, MOE_TILE)
    flat = lambda t: t.reshape(-1).astype(jnp.int32)
    used_tiles = gend[-1] // MOE_TILE
    gaps = jnp.concatenate([gstart + tot, grp - tot, jnp.stack([used_tiles, n_tiles - used_tiles])])
    return (flat(loff), flat(goff), flat(pc), flat(gaps), tile_exp.astype(jnp.int32),
            tile_rows.astype(jnp.int32), n_tiles)


def _local_positions(ridx, loff_ref, j):
    rows = ridx.shape[0]
    lane = lax.broadcasted_iota(jnp.int32, (rows, LANES), 1)
    oh0 = lane == ridx[:, 0:1]
    oh1 = lane == ridx[:, 1:2]
    ti = lax.broadcasted_iota(jnp.int32, (rows, rows), 0)
    tj = lax.broadcasted_iota(jnp.int32, (rows, rows), 1)
    before = jnp.where(tj < ti, 1.0, 0.0).astype(BF16)
    f0 = jnp.where(oh0, 1.0, 0.0)
    f1 = jnp.where(oh1, 1.0, 0.0)
    pre0 = jnp.dot(before, f0.astype(BF16), preferred_element_type=F32)
    pre1 = jnp.dot(before, f1.astype(BF16), preferred_element_type=F32)
    c0 = jnp.sum(f0, axis=0, keepdims=True)
    lane1 = lax.broadcasted_iota(jnp.int32, (1, LANES), 1)
    loff = jnp.zeros((1, LANES), F32)
    for e in range(N_EXPERTS):
        loff = jnp.where(lane1 == e, loff_ref[j * N_EXPERTS + e].astype(F32), loff)
    pos0 = jnp.sum(jnp.where(oh0, loff + pre0, 0.0), axis=1, keepdims=True)
    pos1 = jnp.sum(jnp.where(oh1, loff + c0 + pre1, 0.0), axis=1, keepdims=True)
    return pos0, pos1


def _segment_copies(j, pc_ref, loff_ref, goff_ref, local_ref, hbm_ref, sem, to_hbm):
    out = []
    for e in range(N_EXPERTS):
        n = pc_ref[j * N_EXPERTS + e]
        lo = loff_ref[j * N_EXPERTS + e]
        go = goff_ref[j * N_EXPERTS + e]
        for b in SEG_PIECES:
            done = n & ~(2 * b - 1)
            loc = local_ref.at[pl.ds(pl.multiple_of(lo + done, SEG_ALIGN), b), :]
            glob = hbm_ref.at[pl.ds(pl.multiple_of(go + done, SEG_ALIGN), b), :]
            cp = pltpu.make_async_copy(loc, glob, sem) if to_hbm else pltpu.make_async_copy(glob, loc, sem)
            out.append(((n & b) != 0, cp))
    return out


def _start_segments(*args):
    for cond, cp in _segment_copies(*args):
        pl.when(cond)(cp.start)


def _wait_segments(j, pc_ref, loff_ref, goff_ref, local_ref, hbm_ref, sem, to_hbm):
    del goff_ref
    last = j * N_EXPERTS + N_EXPERTS - 1
    total = loff_ref[last] + pc_ref[last]
    for b in (512,) + SEG_PIECES:
        loc = local_ref.at[pl.ds(0, b), :]
        glob = hbm_ref.at[pl.ds(0, b), :]
        cp = pltpu.make_async_copy(loc, glob, sem) if to_hbm else pltpu.make_async_copy(glob, loc, sem)
        pl.when((total & b) != 0)(cp.wait)


def _gap_copies(gaps_ref, zero_ref, xs_hbm, sem):
    out = []
    for e in range(N_EXPERTS):
        start = gaps_ref[e]
        n = gaps_ref[N_EXPERTS + e]
        for b in GAP_PIECES:
            done = n & ~(2 * b - 1)
            dst = xs_hbm.at[pl.ds(pl.multiple_of(start + done, SEG_ALIGN), b), :]
            out.append(((n & b) != 0, pltpu.make_async_copy(zero_ref.at[pl.ds(0, b), :], dst, sem)))
    return out


def _zero_fill_gaps(gaps_ref, zero_ref, xs_hbm, sem):
    zero_ref[...] = jnp.zeros(zero_ref.shape, F32)
    first_tile = gaps_ref[2 * N_EXPERTS]
    n_tail = gaps_ref[2 * N_EXPERTS + 1]

    def tail_copy(i):
        row0 = pl.multiple_of((first_tile + i) * MOE_TILE, MOE_TILE)
        return pltpu.make_async_copy(zero_ref, xs_hbm.at[pl.ds(row0, MOE_TILE), :], sem)

    for cond, cp in _gap_copies(gaps_ref, zero_ref, xs_hbm, sem):
        pl.when(cond)(cp.start)
    lax.fori_loop(0, n_tail, lambda i, c: (tail_copy(i).start(), c)[1], 0)
    for cond, cp in _gap_copies(gaps_ref, zero_ref, xs_hbm, sem):
        pl.when(cond)(cp.wait)
    lax.fori_loop(0, n_tail, lambda i, c: (tail_copy(i).wait(), c)[1], 0)


def _moe_dispatch_kernel(loff_ref, goff_ref, pc_ref, gaps_ref, ridx_ref, h_ref, xs_hbm, local_ref, zero_ref,
                         sem, zsem):
    j = pl.program_id(0)
    n = pl.num_programs(0)
    slot = j % 2

    @pl.when(j == 0)
    def _gaps():
        _zero_fill_gaps(gaps_ref, zero_ref, xs_hbm, zsem.at[0])

    pos0, pos1 = _local_positions(ridx_ref[...], loff_ref, j)
    col = lax.broadcasted_iota(jnp.int32, (MOE_TOK_TILE, MOE_LOCAL_ROWS), 1).astype(F32)
    place = jnp.where((col == pos0) | (col == pos1), 1.0, 0.0).astype(BF16)
    xs = lax.dot_general(place, h_ref[...], (((0,), (0,)), ((), ())), preferred_element_type=F32)

    def seg(jj, s):
        return (jj, pc_ref, loff_ref, goff_ref, local_ref.at[s], xs_hbm, sem.at[s], True)

    @pl.when(j >= 2)
    def _reuse():
        _wait_segments(*seg(j - 2, slot))

    local_ref[slot] = xs
    _start_segments(*seg(j, slot))

    @pl.when(j == n - 1)
    def _drain():
        _wait_segments(*seg(j, slot))

        @pl.when(n >= 2)
        def _():
            _wait_segments(*seg(j - 1, 1 - slot))


def _moe_dispatch(ridx, h, loff, goff, pc, gaps, n_rows):
    T = h.shape[0]
    grid_spec = pltpu.PrefetchScalarGridSpec(
        num_scalar_prefetch=4,
        grid=(T // MOE_TOK_TILE,),
        in_specs=[pl.BlockSpec((MOE_TOK_TILE, LANES), lambda j, *_: (j, 0)),
                  pl.BlockSpec((MOE_TOK_TILE, D_MODEL), lambda j, *_: (j, 0))],
        out_specs=pl.BlockSpec(memory_space=pl.ANY),
        scratch_shapes=[pltpu.VMEM((2, MOE_LOCAL_ROWS, D_MODEL), F32), pltpu.VMEM((MOE_TILE, D_MODEL), F32),
                        pltpu.SemaphoreType.DMA((2,)), pltpu.SemaphoreType.DMA((1,))])
    return pl.pallas_call(
        _moe_dispatch_kernel,
        grid_spec=grid_spec,
        out_shape=jax.ShapeDtypeStruct((n_rows, D_MODEL), F32),
        compiler_params=_cparams(("arbitrary",)),
        name="moe_dispatch",
    )(loff, goff, pc, gaps, ridx, h)


def _moe_expert_kernel(texp_ref, trows_ref, xs_ref, wg_ref, wu_ref, wd_ref, ys_ref):
    del texp_ref
    rows = trows_ref[pl.program_id(0)]

    @pl.when(rows > 0)
    def _compute():
        x = xs_ref[...].astype(BF16)
        gate = jnp.dot(x, wg_ref[0].astype(BF16), preferred_element_type=F32)
        up = jnp.dot(x, wu_ref[0].astype(BF16), preferred_element_type=F32)
        act = (gate * _sigmoid(gate) * up).astype(BF16)
        ys_ref[...] = jnp.dot(act, wd_ref[0].astype(BF16), preferred_element_type=F32)

    @pl.when(rows == 0)
    def _skip():
        ys_ref[...] = jnp.zeros(ys_ref.shape, F32)


def _moe_experts(xs, tile_exp, tile_rows, wg, wu, wd):
    n_tiles = tile_exp.shape[0]
    _, _, F = wg.shape
    grid_spec = pltpu.PrefetchScalarGridSpec(
        num_scalar_prefetch=2,
        grid=(n_tiles,),
        in_specs=[pl.BlockSpec((MOE_TILE, D_MODEL), lambda i, te, tr: (i, 0)),
                  pl.BlockSpec((1, D_MODEL, F), lambda i, te, tr: (te[i], 0, 0)),
                  pl.BlockSpec((1, D_MODEL, F), lambda i, te, tr: (te[i], 0, 0)),
                  pl.BlockSpec((1, F, D_MODEL), lambda i, te, tr: (te[i], 0, 0))],
        out_specs=pl.BlockSpec((MOE_TILE, D_MODEL), lambda i, te, tr: (i, 0)))
    return pl.pallas_call(
        _moe_expert_kernel,
        grid_spec=grid_spec,
        out_shape=jax.ShapeDtypeStruct((n_tiles * MOE_TILE, D_MODEL), F32),
        compiler_params=_cparams(("arbitrary",)),
        name="moe_experts",
    )(tile_exp, tile_rows, xs, wg, wu, wd)


def _moe_combine_kernel(loff_ref, goff_ref, pc_ref, ridx_ref, rgate_ref, x_ref, ys_hbm, *rest, final):
    if final:
        gf_ref, o_ref, local_ref, sem = rest
    else:
        o_ref, local_ref, sem = rest
    j = pl.program_id(0)
    n = pl.num_programs(0)
    slot = j % 2

    def seg(jj, s):
        return (jj, pc_ref, loff_ref, goff_ref, local_ref.at[s], ys_hbm, sem.at[s], False)

    @pl.when(j == 0)
    def _prologue():
        _start_segments(*seg(j, slot))

    @pl.when(j + 1 < n)
    def _prefetch():
        _start_segments(*seg(j + 1, 1 - slot))

    pos0, pos1 = _local_positions(ridx_ref[...], loff_ref, j)
    col = lax.broadcasted_iota(jnp.int32, (MOE_TOK_TILE, MOE_LOCAL_ROWS), 1).astype(F32)
    pick0 = jnp.where(col == pos0, 1.0, 0.0).astype(BF16)
    pick1 = jnp.where(col == pos1, 1.0, 0.0).astype(BF16)
    used = loff_ref[j * N_EXPERTS + N_EXPERTS - 1] + pc_ref[j * N_EXPERTS + N_EXPERTS - 1]
    _wait_segments(*seg(j, slot))
    ri = lax.broadcasted_iota(jnp.int32, (MOE_LOCAL_ROWS, D_MODEL), 0)
    ys = jnp.where(ri < used, local_ref[slot], 0.0).astype(BF16)
    y0 = jnp.dot(pick0, ys, preferred_element_type=F32)
    y1 = jnp.dot(pick1, ys, preferred_element_type=F32)
    gates = rgate_ref[...]
    xo = x_ref[...] + gates[:, 0:1] * y0 + gates[:, 1:2] * y1
    o_ref[...] = _rms(xo, gf_ref[...]) if final else xo


def _moe_combine(x2, ys, ridx, rgate, loff, goff, pc, final_g):
    T = x2.shape[0]
    final = final_g is not None
    in_specs = [pl.BlockSpec((MOE_TOK_TILE, LANES), lambda j, *_: (j, 0)),
                pl.BlockSpec((MOE_TOK_TILE, LANES), lambda j, *_: (j, 0)),
                pl.BlockSpec((MOE_TOK_TILE, D_MODEL), lambda j, *_: (j, 0)),
                pl.BlockSpec(memory_space=pl.ANY)]
    args = [ridx, rgate, x2, ys]
    if final:
        in_specs.append(pl.BlockSpec((1, D_MODEL), lambda j, *_: (0, 0)))
        args.append(final_g)
    grid_spec = pltpu.PrefetchScalarGridSpec(
        num_scalar_prefetch=3,
        grid=(T // MOE_TOK_TILE,),
        in_specs=in_specs,
        out_specs=pl.BlockSpec((MOE_TOK_TILE, D_MODEL), lambda j, *_: (j, 0)),
        scratch_shapes=[pltpu.VMEM((2, MOE_LOCAL_ROWS, D_MODEL), F32), pltpu.SemaphoreType.DMA((2,))])
    return pl.pallas_call(
        functools.partial(_moe_combine_kernel, final=final),
        grid_spec=grid_spec,
        out_shape=jax.ShapeDtypeStruct((T, D_MODEL), F32),
        compiler_params=_cparams(("arbitrary",)),
        name="moe_combine",
    )(loff, goff, pc, *args)


def _moe(h, x2, ridx, rgate, wg, wu, wd, final_g):
    T = x2.shape[0]
    loff, goff, pc, gaps, tile_exp, tile_rows, n_tiles = _moe_offsets(ridx, T)
    xs = _moe_dispatch(ridx, h, loff, goff, pc, gaps, n_tiles * MOE_TILE)
    ys = _moe_experts(xs, tile_exp, tile_rows, wg, wu, wd)
    return _moe_combine(x2, ys, ridx, rgate, loff, goff, pc, final_g)


def kernel(x, norm_mix_g, w_in, shift_mu, rwkv_w0, rwkv_w2, rwkv_a0, rwkv_a2, rwkv_g2, rwkv_k_k, rwkv_k_a, rwkv_r_k, rwkv_ln_w, rwkv_ln_b, attn_rel_bias, attn_norm_g, w_out, norm_ffn_g, ffn_w_gate, ffn_w_up, ffn_w_down, moe_router, moe_w_gate, moe_w_up, moe_w_down, norm_final_g):
    B, S, D = x.shape
    depth = w_in.shape[0]
    T = B * S
    tm = min(512, T)
    row = lambda t: t.reshape(1, -1).astype(F32)

    hi = jnp.arange(LANES) // HEAD_DIM
    bd = (hi[:, None] == hi[None, :]).astype(BF16)
    rbt = jnp.pad(jnp.swapaxes(attn_rel_bias, 1, 2).astype(F32),
                  ((0, 0), (0, 0), (0, REL_ROWS - attn_rel_bias.shape[1])))
    bias_tabs = _bias_tables(rbt)

    x2 = x.reshape(T, D).astype(F32)
    for l in range(depth):
        ps, qkv = _inproj(x2, row(norm_mix_g[l]), w_in.astype(F32), l, tm)

        zeros = jnp.zeros((DECAY_LORA, RWKV_WIDTH), F32)
        wa = jnp.concatenate([jnp.concatenate([rwkv_w2[l], zeros], axis=1),
                              jnp.concatenate([zeros, rwkv_a2[l]], axis=1)], axis=0).astype(BF16)
        vec = jnp.stack([rwkv_w0[l], rwkv_a0[l], rwkv_k_k[l], rwkv_k_a[l], rwkv_r_k[l],
                         rwkv_ln_w[l], rwkv_ln_b[l], jnp.zeros_like(rwkv_w0[l])]).astype(F32)
        y_rwkv = _rwkv(ps.reshape(B, S, SHIFT_COLS), row(shift_mu[l]), vec, wa,
                       rwkv_g2[l].astype(BF16), bd)
        y_attn = _attn(qkv.reshape(B, S, 3 * ATTN_WIDTH), bias_tabs, l, row(attn_norm_g[l]))

        li = l // 2
        final_g = row(norm_final_g) if l == depth - 1 else None
        mixed = (y_rwkv.reshape(T, RWKV_WIDTH), y_attn.reshape(T, ATTN_WIDTH), x2,
                 w_out[l].astype(BF16), row(norm_ffn_g[l]))
        if l % 2 == 1:
            router = jnp.pad(moe_router[li].astype(F32), ((0, 0), (0, LANES - N_EXPERTS)))
            x_mid, h, ridx, rgate = _outproj(*mixed, router, tm)
            x2 = _moe(h, x_mid, ridx, rgate, moe_w_gate[li].astype(F32), moe_w_up[li].astype(F32),
                      moe_w_down[li].astype(F32), final_g)
        else:
            x2 = _ffn(*mixed, ffn_w_gate[li].astype(F32), ffn_w_up[li].astype(F32),
                      ffn_w_down[li].astype(F32), final_g, tm)
    return x2.reshape(B, S, D).astype(x.dtype)
```

```python
import functools

import jax
import jax.numpy as jnp
from jax import lax
from jax.experimental import pallas as pl
from jax.experimental.pallas import tpu as pltpu

F32 = jnp.float32
BF16 = jnp.bfloat16

D_MODEL = 1024
CHUNK_LOG2 = 6
CHUNK = 1 << CHUNK_LOG2
N_LEFT_CHUNKS = 8
HEAD_DIM = 64
RWKV_WIDTH = 512
ATTN_WIDTH = 512
DECAY_LORA = 64
AAA_LORA = 64
GATE_LORA = 128
REL_CLIP = 128
N_EXPERTS = 8
RMS_EPS = 1e-6
GN_EPS = 64e-5
MASK_VALUE = -1e30
SHIFT_COLS = 3 * RWKV_WIDTH + DECAY_LORA + AAA_LORA + GATE_LORA

LANES = 128
SUBLANES = 8
ATTN_HEADS = ATTN_WIDTH // HEAD_DIM
PAIR = 2 * CHUNK
N_PAIRS = RWKV_WIDTH // LANES
ATTN_WINDOW = (N_LEFT_CHUNKS + 2) * CHUNK
N_BIAS_TABLES = N_LEFT_CHUNKS + 2
BIAS_BASE = 768
REL_ROWS = 384
VMEM_LIMIT = 56 * 1024 * 1024


def _cparams(sem):
    return pltpu.CompilerParams(dimension_semantics=sem, vmem_limit_bytes=VMEM_LIMIT)


def _mm(a, b):
    return jnp.dot(a.astype(BF16), b.astype(BF16), preferred_element_type=F32)


def _mm_nt(a, b):
    return lax.dot_general(a.astype(BF16), b.astype(BF16), (((1,), (1,)), ((), ())),
                           preferred_element_type=F32)


def _mm_tn(a, b):
    return lax.dot_general(a.astype(BF16), b.astype(BF16), (((0,), (0,)), ((), ())),
                           preferred_element_type=F32)


def _split_terms(x, n):
    terms, rem = [], x
    for _ in range(n):
        hi = rem.astype(BF16)
        terms.append(hi)
        rem = rem - hi.astype(F32)
    return terms


def _dot_exact_rhs(x, w_bf16, n):
    acc = None
    for t in _split_terms(x, n):
        d = jnp.dot(t, w_bf16, preferred_element_type=F32)
        acc = d if acc is None else acc + d
    return acc


def _dot_exact_lhs(w_bf16, x, n):
    acc = None
    for t in _split_terms(x, n):
        d = jnp.dot(w_bf16, t, preferred_element_type=F32)
        acc = d if acc is None else acc + d
    return acc


def _dot_f32(a, b):
    a1, a2 = _split_terms(a, 2)
    b1, b2 = _split_terms(b, 2)
    n = b.shape[1]
    t = jnp.dot(a1, jnp.concatenate([b1, b2], axis=1), preferred_element_type=F32)
    return t[:, :n] + t[:, n:] + jnp.dot(a2, b1, preferred_element_type=F32)


def _sigmoid(x):
    return 1.0 / (1.0 + jnp.exp(-x))


def _rms(x, g):
    return x * lax.rsqrt(jnp.mean(x * x, axis=-1, keepdims=True) + RMS_EPS) * g


def _inproj_kernel(x_ref, g_ref, w_ref, ps_ref, qkv_ref):
    hb = _rms(x_ref[...], g_ref[...]).astype(BF16)
    ps_ref[...] = jnp.dot(hb, w_ref[0, :, :SHIFT_COLS].astype(BF16), preferred_element_type=F32)
    qkv_ref[...] = jnp.dot(hb, w_ref[0, :, SHIFT_COLS:].astype(BF16), preferred_element_type=F32).astype(BF16)


def _inproj(x2, g, w_in, l, tm):
    T = x2.shape[0]
    n_in = w_in.shape[2]
    na = n_in - SHIFT_COLS
    return pl.pallas_call(
        _inproj_kernel,
        grid=(T // tm,),
        in_specs=[pl.BlockSpec((tm, D_MODEL), lambda i: (i, 0)),
                  pl.BlockSpec((1, D_MODEL), lambda i: (0, 0)),
                  pl.BlockSpec((1, D_MODEL, n_in), lambda i: (l, 0, 0))],
        out_specs=[pl.BlockSpec((tm, SHIFT_COLS), lambda i: (i, 0)),
                   pl.BlockSpec((tm, na), lambda i: (i, 0))],
        out_shape=[jax.ShapeDtypeStruct((T, SHIFT_COLS), F32),
                   jax.ShapeDtypeStruct((T, na), BF16)],
        compiler_params=_cparams(("parallel",)),
        name="inproj",
    )(x2, g, w_in)


def _rwkv_kernel(ps_ref, prev_ref, mu_ref, vec_ref, wa_ref, g2_ref, bd_ref, y_ref, h_ref):
    c = pl.program_id(1)

    @pl.when(c == 0)
    def _init():
        h_ref[...] = jnp.zeros(h_ref.shape, F32)

    p = ps_ref[0]
    last = jnp.where(c > 0, prev_ref[0][SUBLANES - 1:SUBLANES, :], 0.0)
    row = lax.broadcasted_iota(jnp.int32, p.shape, 0)
    prev = jnp.where(row == 0, last, pltpu.roll(p, 1, 0))
    xs = p + mu_ref[...] * (prev - p)

    W = RWKV_WIDTH
    r, k, v = xs[:, 0:W], xs[:, W:2 * W], xs[:, 2 * W:3 * W]
    z0 = xs[:, 3 * W:3 * W + LANES]
    gd = xs[:, 3 * W + LANES:3 * W + 2 * LANES]
    m1 = lax.broadcasted_iota(jnp.int32, (CHUNK, LANES), 1) < HEAD_DIM
    z0 = jnp.where(lax.broadcasted_iota(jnp.int32, z0.shape, 1) < DECAY_LORA, jnp.tanh(z0), z0)
    lora = _mm(z0, wa_ref[...])
    vec = vec_ref[...]
    w0, a0, k_k, k_a, r_k, ln_w, ln_b = (vec[i:i + 1] for i in range(7))
    w = w0 + lora[:, :W]
    a = _sigmoid(a0 + lora[:, W:])
    g = _mm(_sigmoid(gd), g2_ref[...])
    softplus_neg_w = jnp.maximum(-w, 0.0) + jnp.log(1.0 + jnp.exp(-jnp.abs(w)))
    lw = -jnp.exp(-softplus_neg_w - 0.5)

    bd = bd_ref[...]

    def head_sums(x):
        xb = x.astype(BF16)
        return jnp.concatenate(
            [jnp.dot(xb[:, LANES * i:LANES * (i + 1)], bd, preferred_element_type=F32) for i in range(N_PAIRS)],
            axis=1)

    kk = k * k_k
    kk = kk / jnp.maximum(jnp.sqrt(head_sums(kk * kk)), 1e-12)
    k2 = k * (1.0 + (a - 1.0) * k_a)
    kka = kk * a

    rows = p.shape[0]
    n_chunks = rows // CHUNK
    ti = lax.broadcasted_iota(jnp.int32, (rows, rows), 0)
    tj = lax.broadcasted_iota(jnp.int32, (rows, rows), 1)
    tri = jnp.where((ti >= tj) & ((ti >> CHUNK_LOG2) == (tj >> CHUNK_LOG2)), 1.0, 0.0).astype(BF16)
    L = _dot_exact_lhs(tri, lw, 2)

    ri = lax.broadcasted_iota(jnp.int32, (PAIR, PAIR), 0)
    ci = lax.broadcasted_iota(jnp.int32, (PAIR, PAIR), 1)
    same_head = (ri >> CHUNK_LOG2) == (ci >> CHUNK_LOG2)
    strict = same_head & (ri > ci)
    incl = same_head & (ri >= ci)
    eye = ri == ci
    eye_f = jnp.where(eye, 1.0, 0.0)

    units = []
    for j in range(n_chunks):
        rs = slice(CHUNK * j, CHUNK * (j + 1))
        Lj, lwj = L[rs], lw[rs]
        Lc = Lj[CHUNK - 1:CHUNK]
        inv = jnp.exp(-Lj)
        to_end = jnp.exp(Lc - Lj)
        gC = jnp.exp(Lc)
        Rt = r[rs] * jnp.exp(Lj)
        At = -kk[rs] * jnp.exp(Lj - lwj)
        Bt, Kt = kka[rs] * inv, k2[rs] * inv
        Bh, Kh = kka[rs] * to_end, k2[rs] * to_end
        vj = v[rs]
        for pi in range(N_PAIRS):
            sl = slice(LANES * pi, LANES * (pi + 1))

            def stack(x):
                xp = x[:, sl]
                return jnp.concatenate([jnp.where(m1, xp, 0.0), jnp.where(m1, 0.0, xp)], axis=0)

            units.append(dict(
                j=j, pi=pi, gC=gC[:, sl], sRt=stack(Rt),
                sAt=stack(At).astype(BF16), sV=stack(vj).astype(BF16),
                sBt=stack(Bt).astype(BF16), sKt=stack(Kt).astype(BF16),
                sBh=stack(Bh).astype(BF16), sKh=stack(Kh).astype(BF16)))

    for u in units:
        big = _mm_nt(jnp.concatenate([u["sAt"], u["sRt"].astype(BF16)], axis=0),
                     jnp.concatenate([u["sBt"], u["sKt"]], axis=0))
        u["AB"] = jnp.where(strict, big[:PAIR, :PAIR], 0.0)
        u["AK"] = jnp.where(strict, big[:PAIR, PAIR:], 0.0)
        u["RB"] = jnp.where(incl, big[PAIR:, :PAIR], 0.0)
        u["RK"] = jnp.where(incl, big[PAIR:, PAIR:], 0.0)
    for u in units:
        u["X"] = eye_f + u["AB"]
        u["Pw"] = _mm(u["AB"], u["AB"])
        u["W1"] = _mm(u["AK"], u["sV"])
    for _ in range(4):
        for u in units:
            PX = _mm(u["Pw"], jnp.concatenate([u["Pw"], u["X"]], axis=1))
            u["Pw"] = PX[:, :PAIR]
            u["X"] = u["X"] + PX[:, PAIR:]
    for u in units:
        u["Tm"] = u["X"] + _mm(u["Pw"], u["X"])
    for u in units:
        u["PQ"] = _mm(u["Tm"], jnp.concatenate([u["sAt"], u["W1"].astype(BF16)], axis=1)).astype(BF16)
    for u in units:
        PQ = u["PQ"]
        Pm, Q = PQ[:, :PAIR], PQ[:, PAIR:]
        RBPQ = _mm(u["RB"], PQ)
        u["Rp"] = u["sRt"] + RBPQ[:, :PAIR]
        u["Y0"] = RBPQ[:, PAIR:] + _mm(u["RK"], u["sV"])
        u["Mm"] = jnp.where(eye, u["gC"], 0.0) + _mm_tn(u["sBh"], Pm)
        u["G"] = _mm_tn(jnp.concatenate([u["sBh"], u["sKh"]], axis=0),
                        jnp.concatenate([Q, u["sV"]], axis=0))
    H = [h_ref[pi] for pi in range(N_PAIRS)]
    y_rows = []
    for j in range(n_chunks):
        ys = []
        for u in units[j * N_PAIRS:(j + 1) * N_PAIRS]:
            pi = u["pi"]
            YH = _mm(jnp.concatenate([u["Rp"], u["Mm"]], axis=0), H[pi])
            Ysm = YH[:PAIR] + u["Y0"]
            H[pi] = YH[PAIR:] + u["G"]
            ys.append(Ysm[:CHUNK] + Ysm[CHUNK:])
        y_rows.append(jnp.concatenate(ys, axis=1))
    for pi in range(N_PAIRS):
        h_ref[pi] = H[pi]
    y = y_rows[0] if n_chunks == 1 else jnp.concatenate(y_rows, axis=0)

    inv_n = 1.0 / HEAD_DIM
    mean = head_sums(y) * inv_n
    d = y - mean
    var = head_sums(d * d) * inv_n
    yn = d * lax.rsqrt(var + GN_EPS) * ln_w + ln_b
    bonus = head_sums(r * k2 * r_k) * v
    y_ref[0] = ((yn + bonus) * g).astype(BF16)


RWKV_BLOCK_CHUNKS = 4


def _rwkv(ps3, mu, vec, wa, g2, bd):
    B, S, _ = ps3.shape
    rows = RWKV_BLOCK_CHUNKS * CHUNK
    nc = S // rows
    rows8 = rows // SUBLANES
    return pl.pallas_call(
        _rwkv_kernel,
        grid=(B, nc),
        in_specs=[pl.BlockSpec((1, rows, SHIFT_COLS), lambda b, c: (b, c, 0)),
                  pl.BlockSpec((1, SUBLANES, SHIFT_COLS), lambda b, c: (b, jnp.maximum(c * rows8 - 1, 0), 0)),
                  pl.BlockSpec((1, SHIFT_COLS), lambda b, c: (0, 0)),
                  pl.BlockSpec((SUBLANES, RWKV_WIDTH), lambda b, c: (0, 0)),
                  pl.BlockSpec((LANES, 2 * RWKV_WIDTH), lambda b, c: (0, 0)),
                  pl.BlockSpec((GATE_LORA, RWKV_WIDTH), lambda b, c: (0, 0)),
                  pl.BlockSpec((LANES, LANES), lambda b, c: (0, 0))],
        out_specs=pl.BlockSpec((1, rows, RWKV_WIDTH), lambda b, c: (b, c, 0)),
        out_shape=jax.ShapeDtypeStruct((B, S, RWKV_WIDTH), BF16),
        scratch_shapes=[pltpu.VMEM((N_PAIRS, PAIR, LANES), F32)],
        compiler_params=_cparams(("parallel", "arbitrary")),
        name="rwkv7",
    )(ps3, ps3, mu, vec, wa, g2, bd)


def _bias_kernel(rbt_ref, o_ref):
    e = pl.program_id(1)
    xi = lax.broadcasted_iota(jnp.int32, (REL_ROWS, BIAS_BASE), 1)
    ji = lax.broadcasted_iota(jnp.int32, (REL_ROWS, BIAS_BASE), 0)
    off = jnp.where(xi < BIAS_BASE - CHUNK, xi, xi - BIAS_BASE)
    idx = jnp.clip(e * CHUNK - off, -REL_CLIP, REL_CLIP) + REL_CLIP
    onehot = jnp.where(idx == ji, 1.0, 0.0).astype(BF16)
    base = _dot_exact_rhs(rbt_ref[0], onehot, 3)
    kj = lax.broadcasted_iota(jnp.int32, (CHUNK, ATTN_WINDOW), 1)
    kc = kj >> CHUNK_LOG2
    valid = (kc <= e) & (kc >= e - N_LEFT_CHUNKS)
    for h in range(ATTN_HEADS):
        rows = jnp.broadcast_to(base[h:h + 1, :], (CHUNK, BIAS_BASE))
        toep = pltpu.roll(rows, 0, 1, stride=1, stride_axis=0)
        o_ref[0, 0, h * CHUNK:(h + 1) * CHUNK, :] = jnp.where(valid, toep[:, :ATTN_WINDOW], MASK_VALUE)


def _bias_tables(rbt):
    L = rbt.shape[0]
    return pl.pallas_call(
        _bias_kernel,
        grid=(L, N_BIAS_TABLES),
        in_specs=[pl.BlockSpec((1, 8, REL_ROWS), lambda l, e: (l, 0, 0))],
        out_specs=pl.BlockSpec((1, 1, ATTN_HEADS * CHUNK, ATTN_WINDOW), lambda l, e: (l, e, 0, 0)),
        out_shape=jax.ShapeDtypeStruct((L, N_BIAS_TABLES, ATTN_HEADS * CHUNK, ATTN_WINDOW), F32),
        compiler_params=_cparams(("parallel", "parallel")),
        name="bias_tables",
    )(rbt)


ATTN_BLOCK_CHUNKS = 8


def _attn_kernel(q_ref, k_ref, v_ref, *rest):
    bias_refs, g_ref, o_ref = rest[:ATTN_BLOCK_CHUNKS], rest[-2], rest[-1]
    n0 = pl.program_id(1) * ATTN_BLOCK_CHUNKS
    q = q_ref[0] * jnp.asarray(HEAD_DIM ** -0.5, BF16)
    m1 = lax.broadcasted_iota(jnp.int32, (CHUNK, LANES), 1) < HEAD_DIM
    zero = jnp.zeros((), BF16)
    n_pairs = ATTN_WIDTH // LANES
    units = []
    for j in range(ATTN_BLOCK_CHUNKS):
        start = pl.multiple_of(jnp.maximum(n0 + j - (N_LEFT_CHUNKS + 1), 0) * CHUNK, CHUNK)
        kw = k_ref[0, pl.ds(start, ATTN_WINDOW), :]
        vw = v_ref[0, pl.ds(start, ATTN_WINDOW), :]
        qj = q[CHUNK * j:CHUNK * (j + 1)]
        for pi in range(n_pairs):
            sl = slice(LANES * pi, LANES * (pi + 1))
            qp = qj[:, sl]
            qs = jnp.concatenate([jnp.where(m1, qp, zero), jnp.where(m1, zero, qp)], axis=0)
            units.append(dict(qs=qs, k=kw[:, sl], v=vw[:, sl],
                              bias=bias_refs[j][0, 0, PAIR * pi:PAIR * (pi + 1), :]))
    for u in units:
        u["s"] = lax.dot_general(u["qs"], u["k"], (((1,), (1,)), ((), ())),
                                 preferred_element_type=F32) + u["bias"]
    for u in units:
        s = u["s"]
        ex = jnp.exp(s - jnp.max(s, axis=1, keepdims=True))
        u["den"] = jnp.sum(ex, axis=1, keepdims=True)
        u["ex"] = ex.astype(BF16)
    for u in units:
        o = jnp.dot(u["ex"], u["v"], preferred_element_type=F32) / u["den"]
        u["o"] = jnp.where(m1, o[:CHUNK], o[CHUNK:])
    rows = [jnp.concatenate([u["o"] for u in units[j * n_pairs:(j + 1) * n_pairs]], axis=1)
            for j in range(ATTN_BLOCK_CHUNKS)]
    o = jnp.concatenate(rows, axis=0)
    o_ref[0] = _rms(o, g_ref[...]).astype(BF16)


def _attn(qkv3, bias_l, l, g):
    B, S, _ = qkv3.shape
    rows = ATTN_BLOCK_CHUNKS * CHUNK
    nc = S // rows

    def bias_spec(j):
        return pl.BlockSpec(
            (1, 1, ATTN_HEADS * CHUNK, ATTN_WINDOW),
            lambda b, n: (l, jnp.minimum(n * ATTN_BLOCK_CHUNKS + j, N_BIAS_TABLES - 1), 0, 0))

    return pl.pallas_call(
        _attn_kernel,
        grid=(B, nc),
        in_specs=[pl.BlockSpec((1, rows, ATTN_WIDTH), lambda b, n: (b, n, 0)),
                  pl.BlockSpec((1, S, ATTN_WIDTH), lambda b, n: (b, 0, 1)),
                  pl.BlockSpec((1, S, ATTN_WIDTH), lambda b, n: (b, 0, 2))]
                 + [bias_spec(j) for j in range(ATTN_BLOCK_CHUNKS)]
                 + [pl.BlockSpec((1, ATTN_WIDTH), lambda b, n: (0, 0))],
        out_specs=pl.BlockSpec((1, rows, ATTN_WIDTH), lambda b, n: (b, n, 0)),
        out_shape=jax.ShapeDtypeStruct((B, S, ATTN_WIDTH), BF16),
        compiler_params=_cparams(("parallel", "arbitrary")),
        name="chunk_attn",
    )(qkv3, qkv3, qkv3, *([bias_l] * ATTN_BLOCK_CHUNKS), g)


def _outproj_kernel(yr_ref, ya_ref, x_ref, w_ref, g_ref, router_ref, xo_ref, h_ref, ridx_ref, rgate_ref):
    y = jnp.concatenate([yr_ref[...], ya_ref[...]], axis=1)
    xn = x_ref[...] + jnp.dot(y, w_ref[...], preferred_element_type=F32)
    xo_ref[...] = xn
    h = _rms(xn, g_ref[...])
    h_ref[...] = h.astype(BF16)
    lane = lax.broadcasted_iota(jnp.int32, ridx_ref.shape, 1)
    neg = jnp.asarray(-jnp.inf, F32)
    lg = jnp.where(lane < N_EXPERTS, _dot_f32(h, router_ref[...]), neg)
    top1 = jnp.max(lg, axis=1, keepdims=True)
    idx1 = jnp.min(jnp.where(lg == top1, lane, LANES), axis=1, keepdims=True)
    lg2 = jnp.where(lane == idx1, neg, lg)
    top2 = jnp.max(lg2, axis=1, keepdims=True)
    idx2 = jnp.min(jnp.where(lg2 == top2, lane, LANES), axis=1, keepdims=True)
    ex = jnp.exp(top2 - top1)
    ridx_ref[...] = jnp.where(lane == 0, idx1, jnp.where(lane == 1, idx2, 0))
    rgate_ref[...] = jnp.where(lane == 0, 1.0 / (1.0 + ex), jnp.where(lane == 1, ex / (1.0 + ex), 0.0))


def _outproj(yr, ya, x2, w, g, router, tm):
    T = x2.shape[0]
    row_tile = lambda width: pl.BlockSpec((tm, width), lambda i: (i, 0))
    return pl.pallas_call(
        _outproj_kernel,
        grid=(T // tm,),
        in_specs=[row_tile(RWKV_WIDTH), row_tile(ATTN_WIDTH), row_tile(D_MODEL),
                  pl.BlockSpec((D_MODEL, D_MODEL), lambda i: (0, 0)),
                  pl.BlockSpec((1, D_MODEL), lambda i: (0, 0)),
                  pl.BlockSpec((D_MODEL, LANES), lambda i: (0, 0))],
        out_specs=[row_tile(D_MODEL), row_tile(D_MODEL), row_tile(LANES), row_tile(LANES)],
        out_shape=[jax.ShapeDtypeStruct((T, D_MODEL), F32), jax.ShapeDtypeStruct((T, D_MODEL), BF16),
                   jax.ShapeDtypeStruct((T, LANES), jnp.int32), jax.ShapeDtypeStruct((T, LANES), F32)],
        compiler_params=_cparams(("parallel",)),
        name="outproj_router",
    )(yr, ya, x2, w, g, router)


FF_TILE = 256


def _ffn_kernel(yr_ref, ya_ref, x_ref, wo_ref, g_ref, wg_ref, wu_ref, wd_ref, *rest, final):
    if final:
        gf_ref, o_ref = rest
    else:
        (o_ref,) = rest
    y = jnp.concatenate([yr_ref[...], ya_ref[...]], axis=1)
    xn = x_ref[...] + jnp.dot(y, wo_ref[...], preferred_element_type=F32)
    h = _rms(xn, g_ref[...]).astype(BF16)
    acc = None
    for f in range(0, wg_ref.shape[1], FF_TILE):
        gate = jnp.dot(h, wg_ref[:, f:f + FF_TILE], preferred_element_type=F32)
        up = jnp.dot(h, wu_ref[:, f:f + FF_TILE], preferred_element_type=F32)
        act = (gate * _sigmoid(gate) * up).astype(BF16)
        d = jnp.dot(act, wd_ref[f:f + FF_TILE, :], preferred_element_type=F32)
        acc = d if acc is None else acc + d
    xo = xn + acc
    o_ref[...] = _rms(xo, gf_ref[...]) if final else xo


def _ffn(yr, ya, x2, w_out, g, wg, wu, wd, final_g, tm):
    T = x2.shape[0]
    F = wg.shape[1]
    final = final_g is not None
    in_specs = [pl.BlockSpec((tm, RWKV_WIDTH), lambda i: (i, 0)),
                pl.BlockSpec((tm, ATTN_WIDTH), lambda i: (i, 0)),
                pl.BlockSpec((tm, D_MODEL), lambda i: (i, 0)),
                pl.BlockSpec((D_MODEL, D_MODEL), lambda i: (0, 0)),
                pl.BlockSpec((1, D_MODEL), lambda i: (0, 0)),
                pl.BlockSpec((D_MODEL, F), lambda i: (0, 0)),
                pl.BlockSpec((D_MODEL, F), lambda i: (0, 0)),
                pl.BlockSpec((F, D_MODEL), lambda i: (0, 0))]
    args = [yr, ya, x2, w_out, g, wg, wu, wd]
    if final:
        in_specs.append(pl.BlockSpec((1, D_MODEL), lambda i: (0, 0)))
        args.append(final_g)
    return pl.pallas_call(
        functools.partial(_ffn_kernel, final=final),
        grid=(T // tm,),
        in_specs=in_specs,
        out_specs=pl.BlockSpec((tm, D_MODEL), lambda i: (i, 0)),
        out_shape=jax.ShapeDtypeStruct((T, D_MODEL), F32),
        compiler_params=_cparams(("parallel",)),
        name="ffn_dense",
    )(*args)


MOE_TILE = 512
MOE_TOK_TILE = 256
SEG_ALIGN = 8
MOE_LOCAL_ROWS = 640
SEG_PIECES = (256, 128, 64, 32, 16, 8)
GAP_PIECES = tuple(b for b in SEG_PIECES if b < MOE_TILE)
WAIT_PIECES = (2 * MOE_TOK_TILE,) + SEG_PIECES
TOP_K = 2


def _moe_offsets(ridx, T):
    nt = T // MOE_TOK_TILE
    e12 = ridx[:, :TOP_K]
    onehot = (e12[:, :, None] == jnp.arange(N_EXPERTS, dtype=jnp.int32)[None, None, :]).astype(jnp.int32)
    cnt = onehot.sum(axis=1).reshape(nt, MOE_TOK_TILE, N_EXPERTS).sum(axis=1)
    pc = ((cnt + SEG_ALIGN - 1) // SEG_ALIGN) * SEG_ALIGN
    loff = jnp.cumsum(pc, axis=1) - pc
    tot = pc.sum(axis=0)
    grp = ((tot + MOE_TILE - 1) // MOE_TILE) * MOE_TILE
    gend = jnp.cumsum(grp)
    gstart = gend - grp
    goff = gstart[None, :] + jnp.cumsum(pc, axis=0) - pc
    n_tiles = -(-(TOP_K * T + (SEG_ALIGN - 1) * N_EXPERTS * nt) // MOE_TILE) + N_EXPERTS
    tile_start = jnp.arange(n_tiles, dtype=jnp.int32) * MOE_TILE
    tile_exp = jnp.minimum(jnp.sum((tile_start[:, None] >= gend[None, :]).astype(jnp.int32), axis=1),
                           N_EXPERTS - 1)
    tile_rows = jnp.clip((gstart + tot)[tile_exp] - tile_start, 0, MOE_TILE)
    flat = lambda t: t.reshape(-1).astype(jnp.int32)
    used_tiles = gend[-1] // MOE_TILE
    gaps = jnp.concatenate([gstart + tot, grp - tot, jnp.stack([used_tiles, n_tiles - used_tiles])])
    return (flat(loff), flat(goff), flat(pc), flat(gaps), tile_exp.astype(jnp.int32),
            tile_rows.astype(jnp.int32), n_tiles)


def _local_positions(ridx, loff_ref, j):
    rows = ridx.shape[0]
    lane = lax.broadcasted_iota(jnp.int32, (rows, LANES), 1)
    oh0 = lane == ridx[:, 0:1]
    oh1 = lane == ridx[:, 1:2]
    ti = lax.broadcasted_iota(jnp.int32, (rows, rows), 0)
    tj = lax.broadcasted_iota(jnp.int32, (rows, rows), 1)
    before = jnp.where(tj < ti, 1.0, 0.0).astype(BF16)
    f0 = jnp.where(oh0, 1.0, 0.0)
    f1 = jnp.where(oh1, 1.0, 0.0)
    pre0 = jnp.dot(before, f0.astype(BF16), preferred_element_type=F32)
    pre1 = jnp.dot(before, f1.astype(BF16), preferred_element_type=F32)
    c0 = jnp.sum(f0, axis=0, keepdims=True)
    lane1 = lax.broadcasted_iota(jnp.int32, (1, LANES), 1)
    loff = jnp.zeros((1, LANES), F32)
    for e in range(N_EXPERTS):
        loff = jnp.where(lane1 == e, loff_ref[j * N_EXPERTS + e].astype(F32), loff)
    pos0 = jnp.sum(jnp.where(oh0, loff + pre0, 0.0), axis=1, keepdims=True)
    pos1 = jnp.sum(jnp.where(oh1, loff + c0 + pre1, 0.0), axis=1, keepdims=True)
    return pos0, pos1


def _segment_copies(j, pc_ref, loff_ref, goff_ref, local_ref, hbm_ref, sem, to_hbm):
    out = []
    for e in range(N_EXPERTS):
        n = pc_ref[j * N_EXPERTS + e]
        lo = loff_ref[j * N_EXPERTS + e]
        go = goff_ref[j * N_EXPERTS + e]
        for b in SEG_PIECES:
            done = n & ~(2 * b - 1)
            loc = local_ref.at[pl.ds(pl.multiple_of(lo + done, SEG_ALIGN), b), :]
            glob = hbm_ref.at[pl.ds(pl.multiple_of(go + done, SEG_ALIGN), b), :]
            cp = pltpu.make_async_copy(loc, glob, sem) if to_hbm else pltpu.make_async_copy(glob, loc, sem)
            out.append(((n & b) != 0, cp))
    return out


def _start_segments(*args):
    for cond, cp in _segment_copies(*args):
        pl.when(cond)(cp.start)


def _wait_segments(j, pc_ref, loff_ref, goff_ref, local_ref, hbm_ref, sem, to_hbm):
    del goff_ref
    last = j * N_EXPERTS + N_EXPERTS - 1
    total = loff_ref[last] + pc_ref[last]
    for b in WAIT_PIECES:
        loc = local_ref.at[pl.ds(0, b), :]
        glob = hbm_ref.at[pl.ds(0, b), :]
        cp = pltpu.make_async_copy(loc, glob, sem) if to_hbm else pltpu.make_async_copy(glob, loc, sem)
        pl.when((total & b) != 0)(cp.wait)


def _gap_copies(gaps_ref, zero_ref, xs_hbm, sem):
    out = []
    for e in range(N_EXPERTS):
        start = gaps_ref[e]
        n = gaps_ref[N_EXPERTS + e]
        for b in GAP_PIECES:
            done = n & ~(2 * b - 1)
            dst = xs_hbm.at[pl.ds(pl.multiple_of(start + done, SEG_ALIGN), b), :]
            out.append(((n & b) != 0, pltpu.make_async_copy(zero_ref.at[pl.ds(0, b), :], dst, sem)))
    return out


def _zero_fill_gaps(gaps_ref, zero_ref, xs_hbm, sem):
    zero_ref[...] = jnp.zeros(zero_ref.shape, F32)
    first_tile = gaps_ref[2 * N_EXPERTS]
    n_tail = gaps_ref[2 * N_EXPERTS + 1]

    def tail_copy(i):
        row0 = pl.multiple_of((first_tile + i) * MOE_TILE, MOE_TILE)
        return pltpu.make_async_copy(zero_ref, xs_hbm.at[pl.ds(row0, MOE_TILE), :], sem)

    for cond, cp in _gap_copies(gaps_ref, zero_ref, xs_hbm, sem):
        pl.when(cond)(cp.start)
    lax.fori_loop(0, n_tail, lambda i, c: (tail_copy(i).start(), c)[1], 0)
    for cond, cp in _gap_copies(gaps_ref, zero_ref, xs_hbm, sem):
        pl.when(cond)(cp.wait)
    lax.fori_loop(0, n_tail, lambda i, c: (tail_copy(i).wait(), c)[1], 0)


def _moe_dispatch_kernel(loff_ref, goff_ref, pc_ref, gaps_ref, ridx_ref, h_ref, xs_hbm, local_ref, zero_ref,
                         sem, zsem):
    j = pl.program_id(0)
    n = pl.num_programs(0)
    slot = j % 2

    @pl.when(j == 0)
    def _gaps():
        _zero_fill_gaps(gaps_ref, zero_ref, xs_hbm, zsem.at[0])

    pos0, pos1 = _local_positions(ridx_ref[...], loff_ref, j)
    col = lax.broadcasted_iota(jnp.int32, (MOE_TOK_TILE, MOE_LOCAL_ROWS), 1).astype(F32)
    place = jnp.where((col == pos0) | (col == pos1), 1.0, 0.0).astype(BF16)
    xs = lax.dot_general(place, h_ref[...], (((0,), (0,)), ((), ())), preferred_element_type=F32)

    def seg(jj, s):
        return (jj, pc_ref, loff_ref, goff_ref, local_ref.at[s], xs_hbm, sem.at[s], True)

    @pl.when(j >= 2)
    def _reuse():
        _wait_segments(*seg(j - 2, slot))

    local_ref[slot] = xs
    _start_segments(*seg(j, slot))

    @pl.when(j == n - 1)
    def _drain():
        _wait_segments(*seg(j, slot))

        @pl.when(n >= 2)
        def _():
            _wait_segments(*seg(j - 1, 1 - slot))


def _moe_dispatch(ridx, h, loff, goff, pc, gaps, n_rows):
    T = h.shape[0]
    grid_spec = pltpu.PrefetchScalarGridSpec(
        num_scalar_prefetch=4,
        grid=(T // MOE_TOK_TILE,),
        in_specs=[pl.BlockSpec((MOE_TOK_TILE, LANES), lambda j, *_: (j, 0)),
                  pl.BlockSpec((MOE_TOK_TILE, D_MODEL), lambda j, *_: (j, 0))],
        out_specs=pl.BlockSpec(memory_space=pl.ANY),
        scratch_shapes=[pltpu.VMEM((2, MOE_LOCAL_ROWS, D_MODEL), F32), pltpu.VMEM((MOE_TILE, D_MODEL), F32),
                        pltpu.SemaphoreType.DMA((2,)), pltpu.SemaphoreType.DMA((1,))])
    return pl.pallas_call(
        _moe_dispatch_kernel,
        grid_spec=grid_spec,
        out_shape=jax.ShapeDtypeStruct((n_rows, D_MODEL), F32),
        compiler_params=_cparams(("arbitrary",)),
        name="moe_dispatch",
    )(loff, goff, pc, gaps, ridx, h)


def _moe_expert_kernel(texp_ref, trows_ref, xs_ref, wg_ref, wu_ref, wd_ref, ys_ref):
    del texp_ref
    rows = trows_ref[pl.program_id(0)]

    @pl.when(rows > 0)
    def _compute():
        x = xs_ref[...].astype(BF16)
        gate = jnp.dot(x, wg_ref[0].astype(BF16), preferred_element_type=F32)
        up = jnp.dot(x, wu_ref[0].astype(BF16), preferred_element_type=F32)
        act = (gate * _sigmoid(gate) * up).astype(BF16)
        ys_ref[...] = jnp.dot(act, wd_ref[0].astype(BF16), preferred_element_type=F32)

    @pl.when(rows == 0)
    def _skip():
        ys_ref[...] = jnp.zeros(ys_ref.shape, F32)


def _moe_experts(xs, tile_exp, tile_rows, wg, wu, wd):
    n_tiles = tile_exp.shape[0]
    _, _, F = wg.shape
    grid_spec = pltpu.PrefetchScalarGridSpec(
        num_scalar_prefetch=2,
        grid=(n_tiles,),
        in_specs=[pl.BlockSpec((MOE_TILE, D_MODEL), lambda i, te, tr: (i, 0)),
                  pl.BlockSpec((1, D_MODEL, F), lambda i, te, tr: (te[i], 0, 0)),
                  pl.BlockSpec((1, D_MODEL, F), lambda i, te, tr: (te[i], 0, 0)),
                  pl.BlockSpec((1, F, D_MODEL), lambda i, te, tr: (te[i], 0, 0))],
        out_specs=pl.BlockSpec((MOE_TILE, D_MODEL), lambda i, te, tr: (i, 0)))
    return pl.pallas_call(
        _moe_expert_kernel,
        grid_spec=grid_spec,
        out_shape=jax.ShapeDtypeStruct((n_tiles * MOE_TILE, D_MODEL), F32),
        compiler_params=_cparams(("arbitrary",)),
        name="moe_experts",
    )(tile_exp, tile_rows, xs, wg, wu, wd)


def _moe_combine_kernel(loff_ref, goff_ref, pc_ref, ridx_ref, rgate_ref, x_ref, ys_hbm, *rest, final):
    if final:
        gf_ref, o_ref, local_ref, sem = rest
    else:
        o_ref, local_ref, sem = rest
    j = pl.program_id(0)
    n = pl.num_programs(0)
    slot = j % 2

    def seg(jj, s):
        return (jj, pc_ref, loff_ref, goff_ref, local_ref.at[s], ys_hbm, sem.at[s], False)

    @pl.when(j == 0)
    def _prologue():
        _start_segments(*seg(j, slot))

    @pl.when(j + 1 < n)
    def _prefetch():
        _start_segments(*seg(j + 1, 1 - slot))

    pos0, pos1 = _local_positions(ridx_ref[...], loff_ref, j)
    col = lax.broadcasted_iota(jnp.int32, (MOE_TOK_TILE, MOE_LOCAL_ROWS), 1).astype(F32)
    pick0 = jnp.where(col == pos0, 1.0, 0.0).astype(BF16)
    pick1 = jnp.where(col == pos1, 1.0, 0.0).astype(BF16)
    used = loff_ref[j * N_EXPERTS + N_EXPERTS - 1] + pc_ref[j * N_EXPERTS + N_EXPERTS - 1]
    _wait_segments(*seg(j, slot))
    ri = lax.broadcasted_iota(jnp.int32, (MOE_LOCAL_ROWS, D_MODEL), 0)
    ys = jnp.where(ri < used, local_ref[slot], 0.0).astype(BF16)
    y0 = jnp.dot(pick0, ys, preferred_element_type=F32)
    y1 = jnp.dot(pick1, ys, preferred_element_type=F32)
    gates = rgate_ref[...]
    xo = x_ref[...] + gates[:, 0:1] * y0 + gates[:, 1:2] * y1
    o_ref[...] = _rms(xo, gf_ref[...]) if final else xo


def _moe_combine(x2, ys, ridx, rgate, loff, goff, pc, final_g):
    T = x2.shape[0]
    final = final_g is not None
    in_specs = [pl.BlockSpec((MOE_TOK_TILE, LANES), lambda j, *_: (j, 0)),
                pl.BlockSpec((MOE_TOK_TILE, LANES), lambda j, *_: (j, 0)),
                pl.BlockSpec((MOE_TOK_TILE, D_MODEL), lambda j, *_: (j, 0)),
                pl.BlockSpec(memory_space=pl.ANY)]
    args = [ridx, rgate, x2, ys]
    if final:
        in_specs.append(pl.BlockSpec((1, D_MODEL), lambda j, *_: (0, 0)))
        args.append(final_g)
    grid_spec = pltpu.PrefetchScalarGridSpec(
        num_scalar_prefetch=3,
        grid=(T // MOE_TOK_TILE,),
        in_specs=in_specs,
        out_specs=pl.BlockSpec((MOE_TOK_TILE, D_MODEL), lambda j, *_: (j, 0)),
        scratch_shapes=[pltpu.VMEM((2, MOE_LOCAL_ROWS, D_MODEL), F32), pltpu.SemaphoreType.DMA((2,))])
    return pl.pallas_call(
        functools.partial(_moe_combine_kernel, final=final),
        grid_spec=grid_spec,
        out_shape=jax.ShapeDtypeStruct((T, D_MODEL), F32),
        compiler_params=_cparams(("arbitrary",)),
        name="moe_combine",
    )(loff, goff, pc, *args)


def _moe(h, x2, ridx, rgate, wg, wu, wd, final_g):
    T = x2.shape[0]
    loff, goff, pc, gaps, tile_exp, tile_rows, n_tiles = _moe_offsets(ridx, T)
    xs = _moe_dispatch(ridx, h, loff, goff, pc, gaps, n_tiles * MOE_TILE)
    ys = _moe_experts(xs, tile_exp, tile_rows, wg, wu, wd)
    return _moe_combine(x2, ys, ridx, rgate, loff, goff, pc, final_g)


def kernel(x, norm_mix_g, w_in, shift_mu, rwkv_w0, rwkv_w2, rwkv_a0, rwkv_a2, rwkv_g2, rwkv_k_k, rwkv_k_a, rwkv_r_k, rwkv_ln_w, rwkv_ln_b, attn_rel_bias, attn_norm_g, w_out, norm_ffn_g, ffn_w_gate, ffn_w_up, ffn_w_down, moe_router, moe_w_gate, moe_w_up, moe_w_down, norm_final_g):
    B, S, D = x.shape
    depth = w_in.shape[0]
    T = B * S
    tm = min(512, T)
    row = lambda t: t.reshape(1, -1).astype(F32)

    hi = jnp.arange(LANES) // HEAD_DIM
    bd = (hi[:, None] == hi[None, :]).astype(BF16)
    rbt = jnp.pad(jnp.swapaxes(attn_rel_bias, 1, 2).astype(F32),
                  ((0, 0), (0, 0), (0, REL_ROWS - attn_rel_bias.shape[1])))
    bias_tabs = _bias_tables(rbt)

    x2 = x.reshape(T, D).astype(F32)
    for l in range(depth):
        ps, qkv = _inproj(x2, row(norm_mix_g[l]), w_in.astype(F32), l, tm)

        zeros = jnp.zeros((DECAY_LORA, RWKV_WIDTH), F32)
        wa = jnp.concatenate([jnp.concatenate([rwkv_w2[l], zeros], axis=1),
                              jnp.concatenate([zeros, rwkv_a2[l]], axis=1)], axis=0).astype(BF16)
        vec = jnp.stack([rwkv_w0[l], rwkv_a0[l], rwkv_k_k[l], rwkv_k_a[l], rwkv_r_k[l],
                         rwkv_ln_w[l], rwkv_ln_b[l], jnp.zeros_like(rwkv_w0[l])]).astype(F32)
        y_rwkv = _rwkv(ps.reshape(B, S, SHIFT_COLS), row(shift_mu[l]), vec, wa,
                       rwkv_g2[l].astype(BF16), bd)
        y_attn = _attn(qkv.reshape(B, S, 3 * ATTN_WIDTH), bias_tabs, l, row(attn_norm_g[l]))

        li = l // 2
        final_g = row(norm_final_g) if l == depth - 1 else None
        mixed = (y_rwkv.reshape(T, RWKV_WIDTH), y_attn.reshape(T, ATTN_WIDTH), x2,
                 w_out[l].astype(BF16), row(norm_ffn_g[l]))
        if l % 2 == 1:
            router = jnp.pad(moe_router[li].astype(F32), ((0, 0), (0, LANES - N_EXPERTS)))
            x_mid, h, ridx, rgate = _outproj(*mixed, router, tm)
            x2 = _moe(h, x_mid, ridx, rgate, moe_w_gate[li].astype(F32), moe_w_up[li].astype(F32),
                      moe_w_down[li].astype(F32), final_g)
        else:
            x2 = _ffn(*mixed, ffn_w_gate[li].astype(BF16), ffn_w_up[li].astype(BF16),
                      ffn_w_down[li].astype(BF16), final_g, tm)
    return x2.reshape(B, S, D).astype(x.dtype)
```

```python
import functools

import jax
import jax.numpy as jnp
from jax import lax
from jax.experimental import pallas as pl
from jax.experimental.pallas import tpu as pltpu

F32 = jnp.float32
BF16 = jnp.bfloat16

D_MODEL = 1024
CHUNK_LOG2 = 6
CHUNK = 1 << CHUNK_LOG2
N_LEFT_CHUNKS = 8
HEAD_DIM = 64
RWKV_WIDTH = 512
ATTN_WIDTH = 512
DECAY_LORA = 64
AAA_LORA = 64
GATE_LORA = 128
REL_CLIP = 128
N_EXPERTS = 8
RMS_EPS = 1e-6
GN_EPS = 64e-5
MASK_VALUE = -1e30
SHIFT_COLS = 3 * RWKV_WIDTH + DECAY_LORA + AAA_LORA + GATE_LORA

LANES = 128
SUBLANES = 8
ATTN_HEADS = ATTN_WIDTH // HEAD_DIM
PAIR = 2 * CHUNK
N_PAIRS = RWKV_WIDTH // LANES
ATTN_WINDOW = (N_LEFT_CHUNKS + 2) * CHUNK
N_BIAS_TABLES = N_LEFT_CHUNKS + 2
BIAS_BASE = 768
REL_ROWS = 384
VMEM_LIMIT = 56 * 1024 * 1024


def _cparams(sem):
    return pltpu.CompilerParams(dimension_semantics=sem, vmem_limit_bytes=VMEM_LIMIT)


def _mm(a, b):
    return jnp.dot(a.astype(BF16), b.astype(BF16), preferred_element_type=F32)


def _mm_nt(a, b):
    return lax.dot_general(a.astype(BF16), b.astype(BF16), (((1,), (1,)), ((), ())),
                           preferred_element_type=F32)


def _mm_tn(a, b):
    return lax.dot_general(a.astype(BF16), b.astype(BF16), (((0,), (0,)), ((), ())),
                           preferred_element_type=F32)


def _split_terms(x, n):
    terms, rem = [], x
    for _ in range(n):
        hi = rem.astype(BF16)
        terms.append(hi)
        rem = rem - hi.astype(F32)
    return terms


def _dot_exact_rhs(x, w_bf16, n):
    acc = None
    for t in _split_terms(x, n):
        d = jnp.dot(t, w_bf16, preferred_element_type=F32)
        acc = d if acc is None else acc + d
    return acc


def _dot_exact_lhs(w_bf16, x, n):
    acc = None
    for t in _split_terms(x, n):
        d = jnp.dot(w_bf16, t, preferred_element_type=F32)
        acc = d if acc is None else acc + d
    return acc


def _dot_f32(a, b):
    a1, a2 = _split_terms(a, 2)
    b1, b2 = _split_terms(b, 2)
    n = b.shape[1]
    t = jnp.dot(a1, jnp.concatenate([b1, b2], axis=1), preferred_element_type=F32)
    return t[:, :n] + t[:, n:] + jnp.dot(a2, b1, preferred_element_type=F32)


def _sigmoid(x):
    return 1.0 / (1.0 + jnp.exp(-x))


def _rms(x, g):
    return x * lax.rsqrt(jnp.mean(x * x, axis=-1, keepdims=True) + RMS_EPS) * g


def _inproj_kernel(x_ref, g_ref, w_ref, ps_ref, qkv_ref):
    hb = _rms(x_ref[...], g_ref[...]).astype(BF16)
    ps_ref[...] = jnp.dot(hb, w_ref[0, :, :SHIFT_COLS].astype(BF16), preferred_element_type=F32)
    qkv_ref[...] = jnp.dot(hb, w_ref[0, :, SHIFT_COLS:].astype(BF16), preferred_element_type=F32).astype(BF16)


def _inproj(x2, g, w_in, l, tm):
    T = x2.shape[0]
    n_in = w_in.shape[2]
    na = n_in - SHIFT_COLS
    return pl.pallas_call(
        _inproj_kernel,
        grid=(T // tm,),
        in_specs=[pl.BlockSpec((tm, D_MODEL), lambda i: (i, 0)),
                  pl.BlockSpec((1, D_MODEL), lambda i: (0, 0)),
                  pl.BlockSpec((1, D_MODEL, n_in), lambda i: (l, 0, 0))],
        out_specs=[pl.BlockSpec((tm, SHIFT_COLS), lambda i: (i, 0)),
                   pl.BlockSpec((tm, na), lambda i: (i, 0))],
        out_shape=[jax.ShapeDtypeStruct((T, SHIFT_COLS), F32),
                   jax.ShapeDtypeStruct((T, na), BF16)],
        compiler_params=_cparams(("parallel",)),
        name="inproj",
    )(x2, g, w_in)


def _rwkv_kernel(ps_ref, prev_ref, mu_ref, vec_ref, wa_ref, g2_ref, bd_ref, y_ref, h_ref):
    c = pl.program_id(1)

    @pl.when(c == 0)
    def _init():
        h_ref[...] = jnp.zeros(h_ref.shape, F32)

    p = ps_ref[0]
    last = jnp.where(c > 0, prev_ref[0][SUBLANES - 1:SUBLANES, :], 0.0)
    row = lax.broadcasted_iota(jnp.int32, p.shape, 0)
    prev = jnp.where(row == 0, last, pltpu.roll(p, 1, 0))
    xs = p + mu_ref[...] * (prev - p)

    W = RWKV_WIDTH
    r, k, v = xs[:, 0:W], xs[:, W:2 * W], xs[:, 2 * W:3 * W]
    z0 = xs[:, 3 * W:3 * W + LANES]
    gd = xs[:, 3 * W + LANES:3 * W + 2 * LANES]
    m1 = lax.broadcasted_iota(jnp.int32, (CHUNK, LANES), 1) < HEAD_DIM
    z0 = jnp.where(lax.broadcasted_iota(jnp.int32, z0.shape, 1) < DECAY_LORA, jnp.tanh(z0), z0)
    lora = _mm(z0, wa_ref[...])
    vec = vec_ref[...]
    w0, a0, k_k, k_a, r_k, ln_w, ln_b = (vec[i:i + 1] for i in range(7))
    w = w0 + lora[:, :W]
    a = _sigmoid(a0 + lora[:, W:])
    g = _mm(_sigmoid(gd), g2_ref[...])
    softplus_neg_w = jnp.maximum(-w, 0.0) + jnp.log(1.0 + jnp.exp(-jnp.abs(w)))
    lw = -jnp.exp(-softplus_neg_w - 0.5)

    bd = bd_ref[...]

    def head_sums(x):
        xb = x.astype(BF16)
        return jnp.concatenate(
            [jnp.dot(xb[:, LANES * i:LANES * (i + 1)], bd, preferred_element_type=F32) for i in range(N_PAIRS)],
            axis=1)

    kk = k * k_k
    kk = kk / jnp.maximum(jnp.sqrt(head_sums(kk * kk)), 1e-12)
    k2 = k * (1.0 + (a - 1.0) * k_a)
    kka = kk * a

    rows = p.shape[0]
    n_chunks = rows // CHUNK
    ti = lax.broadcasted_iota(jnp.int32, (rows, rows), 0)
    tj = lax.broadcasted_iota(jnp.int32, (rows, rows), 1)
    tri = jnp.where((ti >= tj) & ((ti >> CHUNK_LOG2) == (tj >> CHUNK_LOG2)), 1.0, 0.0).astype(BF16)
    L = _dot_exact_lhs(tri, lw, 2)

    ri = lax.broadcasted_iota(jnp.int32, (PAIR, PAIR), 0)
    ci = lax.broadcasted_iota(jnp.int32, (PAIR, PAIR), 1)
    same_head = (ri >> CHUNK_LOG2) == (ci >> CHUNK_LOG2)
    strict = same_head & (ri > ci)
    incl = same_head & (ri >= ci)
    eye = ri == ci
    eye_f = jnp.where(eye, 1.0, 0.0)

    units = []
    for j in range(n_chunks):
        rs = slice(CHUNK * j, CHUNK * (j + 1))
        Lj, lwj = L[rs], lw[rs]
        Lc = Lj[CHUNK - 1:CHUNK]
        inv = jnp.exp(-Lj)
        to_end = jnp.exp(Lc - Lj)
        gC = jnp.exp(Lc)
        Rt = r[rs] * jnp.exp(Lj)
        At = -kk[rs] * jnp.exp(Lj - lwj)
        Bt, Kt = kka[rs] * inv, k2[rs] * inv
        Bh, Kh = kka[rs] * to_end, k2[rs] * to_end
        vj = v[rs]
        for pi in range(N_PAIRS):
            sl = slice(LANES * pi, LANES * (pi + 1))

            def stack(x):
                xp = x[:, sl]
                return jnp.concatenate([jnp.where(m1, xp, 0.0), jnp.where(m1, 0.0, xp)], axis=0)

            units.append(dict(
                j=j, pi=pi, gC=gC[:, sl], sRt=stack(Rt),
                sAt=stack(At).astype(BF16), sV=stack(vj).astype(BF16),
                sBt=stack(Bt).astype(BF16), sKt=stack(Kt).astype(BF16),
                sBh=stack(Bh).astype(BF16), sKh=stack(Kh).astype(BF16)))

    for u in units:
        big = _mm_nt(jnp.concatenate([u["sAt"], u["sRt"].astype(BF16)], axis=0),
                     jnp.concatenate([u["sBt"], u["sKt"]], axis=0))
        u["AB"] = jnp.where(strict, big[:PAIR, :PAIR], 0.0)
        u["AK"] = jnp.where(strict, big[:PAIR, PAIR:], 0.0)
        u["RB"] = jnp.where(incl, big[PAIR:, :PAIR], 0.0)
        u["RK"] = jnp.where(incl, big[PAIR:, PAIR:], 0.0)
    for u in units:
        u["X"] = eye_f + u["AB"]
        u["Pw"] = _mm(u["AB"], u["AB"])
        u["W1"] = _mm(u["AK"], u["sV"])
    for _ in range(4):
        for u in units:
            PX = _mm(u["Pw"], jnp.concatenate([u["Pw"], u["X"]], axis=1))
            u["Pw"] = PX[:, :PAIR]
            u["X"] = u["X"] + PX[:, PAIR:]
    for u in units:
        u["Tm"] = u["X"] + _mm(u["Pw"], u["X"])
    for u in units:
        u["PQ"] = _mm(u["Tm"], jnp.concatenate([u["sAt"], u["W1"].astype(BF16)], axis=1)).astype(BF16)
    for u in units:
        PQ = u["PQ"]
        Pm, Q = PQ[:, :PAIR], PQ[:, PAIR:]
        RBPQ = _mm(u["RB"], PQ)
        u["Rp"] = u["sRt"] + RBPQ[:, :PAIR]
        u["Y0"] = RBPQ[:, PAIR:] + _mm(u["RK"], u["sV"])
        u["Mm"] = jnp.where(eye, u["gC"], 0.0) + _mm_tn(u["sBh"], Pm)
        u["G"] = _mm_tn(jnp.concatenate([u["sBh"], u["sKh"]], axis=0),
                        jnp.concatenate([Q, u["sV"]], axis=0))
    H = [h_ref[pi] for pi in range(N_PAIRS)]
    y_rows = []
    for j in range(n_chunks):
        ys = []
        for u in units[j * N_PAIRS:(j + 1) * N_PAIRS]:
            pi = u["pi"]
            YH = _mm(jnp.concatenate([u["Rp"], u["Mm"]], axis=0), H[pi])
            Ysm = YH[:PAIR] + u["Y0"]
            H[pi] = YH[PAIR:] + u["G"]
            ys.append(Ysm[:CHUNK] + Ysm[CHUNK:])
        y_rows.append(jnp.concatenate(ys, axis=1))
    for pi in range(N_PAIRS):
        h_ref[pi] = H[pi]
    y = y_rows[0] if n_chunks == 1 else jnp.concatenate(y_rows, axis=0)

    inv_n = 1.0 / HEAD_DIM
    mean = head_sums(y) * inv_n
    d = y - mean
    var = head_sums(d * d) * inv_n
    yn = d * lax.rsqrt(var + GN_EPS) * ln_w + ln_b
    bonus = head_sums(r * k2 * r_k) * v
    y_ref[0] = ((yn + bonus) * g).astype(BF16)


RWKV_BLOCK_CHUNKS = 4


def _rwkv(ps3, mu, vec, wa, g2, bd):
    B, S, _ = ps3.shape
    rows = RWKV_BLOCK_CHUNKS * CHUNK
    nc = S // rows
    rows8 = rows // SUBLANES
    return pl.pallas_call(
        _rwkv_kernel,
        grid=(B, nc),
        in_specs=[pl.BlockSpec((1, rows, SHIFT_COLS), lambda b, c: (b, c, 0)),
                  pl.BlockSpec((1, SUBLANES, SHIFT_COLS), lambda b, c: (b, jnp.maximum(c * rows8 - 1, 0), 0)),
                  pl.BlockSpec((1, SHIFT_COLS), lambda b, c: (0, 0)),
                  pl.BlockSpec((SUBLANES, RWKV_WIDTH), lambda b, c: (0, 0)),
                  pl.BlockSpec((LANES, 2 * RWKV_WIDTH), lambda b, c: (0, 0)),
                  pl.BlockSpec((GATE_LORA, RWKV_WIDTH), lambda b, c: (0, 0)),
                  pl.BlockSpec((LANES, LANES), lambda b, c: (0, 0))],
        out_specs=pl.BlockSpec((1, rows, RWKV_WIDTH), lambda b, c: (b, c, 0)),
        out_shape=jax.ShapeDtypeStruct((B, S, RWKV_WIDTH), BF16),
        scratch_shapes=[pltpu.VMEM((N_PAIRS, PAIR, LANES), F32)],
        compiler_params=_cparams(("parallel", "arbitrary")),
        name="rwkv7",
    )(ps3, ps3, mu, vec, wa, g2, bd)


def _bias_kernel(rbt_ref, o_ref):
    e = pl.program_id(1)
    xi = lax.broadcasted_iota(jnp.int32, (REL_ROWS, BIAS_BASE), 1)
    ji = lax.broadcasted_iota(jnp.int32, (REL_ROWS, BIAS_BASE), 0)
    off = jnp.where(xi < BIAS_BASE - CHUNK, xi, xi - BIAS_BASE)
    idx = jnp.clip(e * CHUNK - off, -REL_CLIP, REL_CLIP) + REL_CLIP
    onehot = jnp.where(idx == ji, 1.0, 0.0).astype(BF16)
    base = _dot_exact_rhs(rbt_ref[0], onehot, 3)
    kj = lax.broadcasted_iota(jnp.int32, (CHUNK, ATTN_WINDOW), 1)
    kc = kj >> CHUNK_LOG2
    valid = (kc <= e) & (kc >= e - N_LEFT_CHUNKS)
    for h in range(ATTN_HEADS):
        rows = jnp.broadcast_to(base[h:h + 1, :], (CHUNK, BIAS_BASE))
        toep = pltpu.roll(rows, 0, 1, stride=1, stride_axis=0)
        o_ref[0, 0, h * CHUNK:(h + 1) * CHUNK, :] = jnp.where(valid, toep[:, :ATTN_WINDOW], MASK_VALUE)


def _bias_tables(rbt):
    L = rbt.shape[0]
    return pl.pallas_call(
        _bias_kernel,
        grid=(L, N_BIAS_TABLES),
        in_specs=[pl.BlockSpec((1, 8, REL_ROWS), lambda l, e: (l, 0, 0))],
        out_specs=pl.BlockSpec((1, 1, ATTN_HEADS * CHUNK, ATTN_WINDOW), lambda l, e: (l, e, 0, 0)),
        out_shape=jax.ShapeDtypeStruct((L, N_BIAS_TABLES, ATTN_HEADS * CHUNK, ATTN_WINDOW), F32),
        compiler_params=_cparams(("parallel", "parallel")),
        name="bias_tables",
    )(rbt)


ATTN_BLOCK_CHUNKS = 8


def _attn_kernel(q_ref, k_ref, v_ref, *rest):
    bias_refs, g_ref, o_ref = rest[:ATTN_BLOCK_CHUNKS], rest[-2], rest[-1]
    n0 = pl.program_id(1) * ATTN_BLOCK_CHUNKS
    q = q_ref[0] * jnp.asarray(HEAD_DIM ** -0.5, BF16)
    m1 = lax.broadcasted_iota(jnp.int32, (CHUNK, LANES), 1) < HEAD_DIM
    zero = jnp.zeros((), BF16)
    n_pairs = ATTN_WIDTH // LANES
    units = []
    for j in range(ATTN_BLOCK_CHUNKS):
        start = pl.multiple_of(jnp.maximum(n0 + j - (N_LEFT_CHUNKS + 1), 0) * CHUNK, CHUNK)
        kw = k_ref[0, pl.ds(start, ATTN_WINDOW), :]
        vw = v_ref[0, pl.ds(start, ATTN_WINDOW), :]
        qj = q[CHUNK * j:CHUNK * (j + 1)]
        for pi in range(n_pairs):
            sl = slice(LANES * pi, LANES * (pi + 1))
            qp = qj[:, sl]
            qs = jnp.concatenate([jnp.where(m1, qp, zero), jnp.where(m1, zero, qp)], axis=0)
            units.append(dict(qs=qs, k=kw[:, sl], v=vw[:, sl],
                              bias=bias_refs[j][0, 0, PAIR * pi:PAIR * (pi + 1), :]))
    for u in units:
        u["s"] = lax.dot_general(u["qs"], u["k"], (((1,), (1,)), ((), ())),
                                 preferred_element_type=F32) + u["bias"]
    for u in units:
        s = u["s"]
        ex = jnp.exp(s - jnp.max(s, axis=1, keepdims=True))
        u["den"] = jnp.sum(ex, axis=1, keepdims=True)
        u["ex"] = ex.astype(BF16)
    for u in units:
        o = jnp.dot(u["ex"], u["v"], preferred_element_type=F32) / u["den"]
        u["o"] = jnp.where(m1, o[:CHUNK], o[CHUNK:])
    rows = [jnp.concatenate([u["o"] for u in units[j * n_pairs:(j + 1) * n_pairs]], axis=1)
            for j in range(ATTN_BLOCK_CHUNKS)]
    o = jnp.concatenate(rows, axis=0)
    o_ref[0] = _rms(o, g_ref[...]).astype(BF16)


def _attn(qkv3, bias_l, l, g):
    B, S, _ = qkv3.shape
    rows = ATTN_BLOCK_CHUNKS * CHUNK
    nc = S // rows

    def bias_spec(j):
        return pl.BlockSpec(
            (1, 1, ATTN_HEADS * CHUNK, ATTN_WINDOW),
            lambda b, n: (l, jnp.minimum(n * ATTN_BLOCK_CHUNKS + j, N_BIAS_TABLES - 1), 0, 0))

    return pl.pallas_call(
        _attn_kernel,
        grid=(B, nc),
        in_specs=[pl.BlockSpec((1, rows, ATTN_WIDTH), lambda b, n: (b, n, 0)),
                  pl.BlockSpec((1, S, ATTN_WIDTH), lambda b, n: (b, 0, 1)),
                  pl.BlockSpec((1, S, ATTN_WIDTH), lambda b, n: (b, 0, 2))]
                 + [bias_spec(j) for j in range(ATTN_BLOCK_CHUNKS)]
                 + [pl.BlockSpec((1, ATTN_WIDTH), lambda b, n: (0, 0))],
        out_specs=pl.BlockSpec((1, rows, ATTN_WIDTH), lambda b, n: (b, n, 0)),
        out_shape=jax.ShapeDtypeStruct((B, S, ATTN_WIDTH), BF16),
        compiler_params=_cparams(("parallel", "arbitrary")),
        name="chunk_attn",
    )(qkv3, qkv3, qkv3, *([bias_l] * ATTN_BLOCK_CHUNKS), g)


def _outproj_kernel(yr_ref, ya_ref, x_ref, w_ref, g_ref, router_ref, xo_ref, h_ref, ridx_ref, rgate_ref):
    y = jnp.concatenate([yr_ref[...], ya_ref[...]], axis=1)
    xn = x_ref[...] + jnp.dot(y, w_ref[...], preferred_element_type=F32)
    xo_ref[...] = xn
    h = _rms(xn, g_ref[...])
    h_ref[...] = h.astype(BF16)
    lane = lax.broadcasted_iota(jnp.int32, ridx_ref.shape, 1)
    neg = jnp.asarray(-jnp.inf, F32)
    lg = jnp.where(lane < N_EXPERTS, _dot_f32(h, router_ref[...]), neg)
    top1 = jnp.max(lg, axis=1, keepdims=True)
    idx1 = jnp.min(jnp.where(lg == top1, lane, LANES), axis=1, keepdims=True)
    lg2 = jnp.where(lane == idx1, neg, lg)
    top2 = jnp.max(lg2, axis=1, keepdims=True)
    idx2 = jnp.min(jnp.where(lg2 == top2, lane, LANES), axis=1, keepdims=True)
    ex = jnp.exp(top2 - top1)
    ridx_ref[...] = jnp.where(lane == 0, idx1, jnp.where(lane == 1, idx2, 0))
    rgate_ref[...] = jnp.where(lane == 0, 1.0 / (1.0 + ex), jnp.where(lane == 1, ex / (1.0 + ex), 0.0))


def _outproj(yr, ya, x2, w, g, router, tm):
    T = x2.shape[0]
    row_tile = lambda width: pl.BlockSpec((tm, width), lambda i: (i, 0))
    return pl.pallas_call(
        _outproj_kernel,
        grid=(T // tm,),
        in_specs=[row_tile(RWKV_WIDTH), row_tile(ATTN_WIDTH), row_tile(D_MODEL),
                  pl.BlockSpec((D_MODEL, D_MODEL), lambda i: (0, 0)),
                  pl.BlockSpec((1, D_MODEL), lambda i: (0, 0)),
                  pl.BlockSpec((D_MODEL, LANES), lambda i: (0, 0))],
        out_specs=[row_tile(D_MODEL), row_tile(D_MODEL), row_tile(LANES), row_tile(LANES)],
        out_shape=[jax.ShapeDtypeStruct((T, D_MODEL), F32), jax.ShapeDtypeStruct((T, D_MODEL), BF16),
                   jax.ShapeDtypeStruct((T, LANES), jnp.int32), jax.ShapeDtypeStruct((T, LANES), F32)],
        compiler_params=_cparams(("parallel",)),
        name="outproj_router",
    )(yr, ya, x2, w, g, router)


FF_TILE = 256


def _ffn_kernel(yr_ref, ya_ref, x_ref, wo_ref, g_ref, wg_ref, wu_ref, wd_ref, *rest, final):
    if final:
        gf_ref, o_ref = rest
    else:
        (o_ref,) = rest
    y = jnp.concatenate([yr_ref[...], ya_ref[...]], axis=1)
    xn = x_ref[...] + jnp.dot(y, wo_ref[...], preferred_element_type=F32)
    h = _rms(xn, g_ref[...]).astype(BF16)
    acc = None
    for f in range(0, wg_ref.shape[1], FF_TILE):
        gate = jnp.dot(h, wg_ref[:, f:f + FF_TILE], preferred_element_type=F32)
        up = jnp.dot(h, wu_ref[:, f:f + FF_TILE], preferred_element_type=F32)
        act = (gate * _sigmoid(gate) * up).astype(BF16)
        d = jnp.dot(act, wd_ref[f:f + FF_TILE, :], preferred_element_type=F32)
        acc = d if acc is None else acc + d
    xo = xn + acc
    o_ref[...] = _rms(xo, gf_ref[...]) if final else xo


def _ffn(yr, ya, x2, w_out, g, wg, wu, wd, final_g, tm):
    T = x2.shape[0]
    F = wg.shape[1]
    final = final_g is not None
    in_specs = [pl.BlockSpec((tm, RWKV_WIDTH), lambda i: (i, 0)),
                pl.BlockSpec((tm, ATTN_WIDTH), lambda i: (i, 0)),
                pl.BlockSpec((tm, D_MODEL), lambda i: (i, 0)),
                pl.BlockSpec((D_MODEL, D_MODEL), lambda i: (0, 0)),
                pl.BlockSpec((1, D_MODEL), lambda i: (0, 0)),
                pl.BlockSpec((D_MODEL, F), lambda i: (0, 0)),
                pl.BlockSpec((D_MODEL, F), lambda i: (0, 0)),
                pl.BlockSpec((F, D_MODEL), lambda i: (0, 0))]
    args = [yr, ya, x2, w_out, g, wg, wu, wd]
    if final:
        in_specs.append(pl.BlockSpec((1, D_MODEL), lambda i: (0, 0)))
        args.append(final_g)
    return pl.pallas_call(
        functools.partial(_ffn_kernel, final=final),
        grid=(T // tm,),
        in_specs=in_specs,
        out_specs=pl.BlockSpec((tm, D_MODEL), lambda i: (i, 0)),
        out_shape=jax.ShapeDtypeStruct((T, D_MODEL), F32),
        compiler_params=_cparams(("parallel",)),
        name="ffn_dense",
    )(*args)


MOE_TILE = 512
MOE_TOK_TILE = 256
SEG_ALIGN = 8
MOE_LOCAL_ROWS = 640
SEG_PIECES = (256, 128, 64, 32, 16, 8)
GAP_PIECES = tuple(b for b in SEG_PIECES if b < MOE_TILE)
WAIT_PIECES = (2 * MOE_TOK_TILE,) + SEG_PIECES
TOP_K = 2


def _moe_offsets(ridx, T):
    nt = T // MOE_TOK_TILE
    e12 = ridx[:, :TOP_K]
    onehot = (e12[:, :, None] == jnp.arange(N_EXPERTS, dtype=jnp.int32)[None, None, :]).astype(jnp.int32)
    cnt = onehot.sum(axis=1).reshape(nt, MOE_TOK_TILE, N_EXPERTS).sum(axis=1)
    pc = ((cnt + SEG_ALIGN - 1) // SEG_ALIGN) * SEG_ALIGN
    loff = jnp.cumsum(pc, axis=1) - pc
    tot = pc.sum(axis=0)
    grp = ((tot + MOE_TILE - 1) // MOE_TILE) * MOE_TILE
    gend = jnp.cumsum(grp)
    gstart = gend - grp
    goff = gstart[None, :] + jnp.cumsum(pc, axis=0) - pc
    n_tiles = -(-(TOP_K * T + (SEG_ALIGN - 1) * N_EXPERTS * nt) // MOE_TILE) + N_EXPERTS
    tile_start = jnp.arange(n_tiles, dtype=jnp.int32) * MOE_TILE
    tile_exp = jnp.minimum(jnp.sum((tile_start[:, None] >= gend[None, :]).astype(jnp.int32), axis=1),
                           N_EXPERTS - 1)
    tile_rows = jnp.clip((gstart + tot)[tile_exp] - tile_start, 0, MOE_TILE)
    flat = lambda t: t.reshape(-1).astype(jnp.int32)
    used_tiles = gend[-1] // MOE_TILE
    gaps = jnp.concatenate([gstart + tot, grp - tot, jnp.stack([used_tiles, n_tiles - used_tiles])])
    return (flat(loff), flat(goff), flat(pc), flat(gaps), tile_exp.astype(jnp.int32),
            tile_rows.astype(jnp.int32), n_tiles)


def _local_positions(ridx, loff_ref, j):
    rows = ridx.shape[0]
    lane = lax.broadcasted_iota(jnp.int32, (rows, LANES), 1)
    oh0 = lane == ridx[:, 0:1]
    oh1 = lane == ridx[:, 1:2]
    ti = lax.broadcasted_iota(jnp.int32, (rows, rows), 0)
    tj = lax.broadcasted_iota(jnp.int32, (rows, rows), 1)
    before = jnp.where(tj < ti, 1.0, 0.0).astype(BF16)
    f0 = jnp.where(oh0, 1.0, 0.0)
    f1 = jnp.where(oh1, 1.0, 0.0)
    pre = jnp.dot(before, jnp.concatenate([f0, f1], axis=1).astype(BF16), preferred_element_type=F32)
    pre0, pre1 = pre[:, :LANES], pre[:, LANES:]
    c0 = jnp.sum(f0, axis=0, keepdims=True)
    lane1 = lax.broadcasted_iota(jnp.int32, (1, LANES), 1)
    loff = jnp.zeros((1, LANES), F32)
    for e in range(N_EXPERTS):
        loff = jnp.where(lane1 == e, loff_ref[j * N_EXPERTS + e].astype(F32), loff)
    pos0 = jnp.sum(jnp.where(oh0, loff + pre0, 0.0), axis=1, keepdims=True)
    pos1 = jnp.sum(jnp.where(oh1, loff + c0 + pre1, 0.0), axis=1, keepdims=True)
    return pos0, pos1


def _segment_copies(j, pc_ref, loff_ref, goff_ref, local_ref, hbm_ref, sem, to_hbm):
    out = []
    for e in range(N_EXPERTS):
        n = pc_ref[j * N_EXPERTS + e]
        lo = loff_ref[j * N_EXPERTS + e]
        go = goff_ref[j * N_EXPERTS + e]
        for b in SEG_PIECES:
            done = n & ~(2 * b - 1)
            loc = local_ref.at[pl.ds(pl.multiple_of(lo + done, SEG_ALIGN), b), :]
            glob = hbm_ref.at[pl.ds(pl.multiple_of(go + done, SEG_ALIGN), b), :]
            cp = pltpu.make_async_copy(loc, glob, sem) if to_hbm else pltpu.make_async_copy(glob, loc, sem)
            out.append(((n & b) != 0, cp))
    return out


def _start_segments(*args):
    for cond, cp in _segment_copies(*args):
        pl.when(cond)(cp.start)


def _wait_segments(j, pc_ref, loff_ref, goff_ref, local_ref, hbm_ref, sem, to_hbm):
    del goff_ref
    last = j * N_EXPERTS + N_EXPERTS - 1
    total = loff_ref[last] + pc_ref[last]
    for b in WAIT_PIECES:
        loc = local_ref.at[pl.ds(0, b), :]
        glob = hbm_ref.at[pl.ds(0, b), :]
        cp = pltpu.make_async_copy(loc, glob, sem) if to_hbm else pltpu.make_async_copy(glob, loc, sem)
        pl.when((total & b) != 0)(cp.wait)


def _gap_copies(gaps_ref, zero_ref, xs_hbm, sem):
    out = []
    for e in range(N_EXPERTS):
        start = gaps_ref[e]
        n = gaps_ref[N_EXPERTS + e]
        for b in GAP_PIECES:
            done = n & ~(2 * b - 1)
            dst = xs_hbm.at[pl.ds(pl.multiple_of(start + done, SEG_ALIGN), b), :]
            out.append(((n & b) != 0, pltpu.make_async_copy(zero_ref.at[pl.ds(0, b), :], dst, sem)))
    return out


def _zero_fill_gaps(gaps_ref, zero_ref, xs_hbm, sem):
    zero_ref[...] = jnp.zeros(zero_ref.shape, F32)
    first_tile = gaps_ref[2 * N_EXPERTS]
    n_tail = gaps_ref[2 * N_EXPERTS + 1]

    def tail_copy(i):
        row0 = pl.multiple_of((first_tile + i) * MOE_TILE, MOE_TILE)
        return pltpu.make_async_copy(zero_ref, xs_hbm.at[pl.ds(row0, MOE_TILE), :], sem)

    for cond, cp in _gap_copies(gaps_ref, zero_ref, xs_hbm, sem):
        pl.when(cond)(cp.start)
    lax.fori_loop(0, n_tail, lambda i, c: (tail_copy(i).start(), c)[1], 0)
    for cond, cp in _gap_copies(gaps_ref, zero_ref, xs_hbm, sem):
        pl.when(cond)(cp.wait)
    lax.fori_loop(0, n_tail, lambda i, c: (tail_copy(i).wait(), c)[1], 0)


def _moe_dispatch_kernel(loff_ref, goff_ref, pc_ref, gaps_ref, ridx_ref, h_ref, xs_hbm, local_ref, zero_ref,
                         sem, zsem):
    j = pl.program_id(0)
    n = pl.num_programs(0)
    slot = j % 2

    @pl.when(j == 0)
    def _gaps():
        _zero_fill_gaps(gaps_ref, zero_ref, xs_hbm, zsem.at[0])

    pos0, pos1 = _local_positions(ridx_ref[...], loff_ref, j)
    col = lax.broadcasted_iota(jnp.int32, (MOE_TOK_TILE, MOE_LOCAL_ROWS), 1).astype(F32)
    place = jnp.where((col == pos0) | (col == pos1), 1.0, 0.0).astype(BF16)
    xs = lax.dot_general(place, h_ref[...], (((0,), (0,)), ((), ())), preferred_element_type=F32)

    def seg(jj, s):
        return (jj, pc_ref, loff_ref, goff_ref, local_ref.at[s], xs_hbm, sem.at[s], True)

    @pl.when(j >= 2)
    def _reuse():
        _wait_segments(*seg(j - 2, slot))

    local_ref[slot] = xs
    _start_segments(*seg(j, slot))

    @pl.when(j == n - 1)
    def _drain():
        _wait_segments(*seg(j, slot))

        @pl.when(n >= 2)
        def _():
            _wait_segments(*seg(j - 1, 1 - slot))


def _moe_dispatch(ridx, h, loff, goff, pc, gaps, n_rows):
    T = h.shape[0]
    grid_spec = pltpu.PrefetchScalarGridSpec(
        num_scalar_prefetch=4,
        grid=(T // MOE_TOK_TILE,),
        in_specs=[pl.BlockSpec((MOE_TOK_TILE, LANES), lambda j, *_: (j, 0)),
                  pl.BlockSpec((MOE_TOK_TILE, D_MODEL), lambda j, *_: (j, 0))],
        out_specs=pl.BlockSpec(memory_space=pl.ANY),
        scratch_shapes=[pltpu.VMEM((2, MOE_LOCAL_ROWS, D_MODEL), F32), pltpu.VMEM((MOE_TILE, D_MODEL), F32),
                        pltpu.SemaphoreType.DMA((2,)), pltpu.SemaphoreType.DMA((1,))])
    return pl.pallas_call(
        _moe_dispatch_kernel,
        grid_spec=grid_spec,
        out_shape=jax.ShapeDtypeStruct((n_rows, D_MODEL), F32),
        compiler_params=_cparams(("arbitrary",)),
        name="moe_dispatch",
    )(loff, goff, pc, gaps, ridx, h)


def _moe_expert_kernel(texp_ref, trows_ref, xs_ref, wg_ref, wu_ref, wd_ref, ys_ref, wgb_ref, wub_ref, wdb_ref):
    i = pl.program_id(0)
    rows = trows_ref[i]

    @pl.when((i == 0) | (texp_ref[i] != texp_ref[jnp.maximum(i - 1, 0)]))
    def _new_expert():
        wgb_ref[...] = wg_ref[0].astype(BF16)
        wub_ref[...] = wu_ref[0].astype(BF16)
        wdb_ref[...] = wd_ref[0].astype(BF16)

    @pl.when(rows > 0)
    def _compute():
        x = xs_ref[...].astype(BF16)
        gate = jnp.dot(x, wgb_ref[...], preferred_element_type=F32)
        up = jnp.dot(x, wub_ref[...], preferred_element_type=F32)
        act = (gate * _sigmoid(gate) * up).astype(BF16)
        ys_ref[...] = jnp.dot(act, wdb_ref[...], preferred_element_type=F32)

    @pl.when(rows == 0)
    def _skip():
        ys_ref[...] = jnp.zeros(ys_ref.shape, F32)


def _moe_experts(xs, tile_exp, tile_rows, wg, wu, wd):
    n_tiles = tile_exp.shape[0]
    _, _, F = wg.shape
    grid_spec = pltpu.PrefetchScalarGridSpec(
        num_scalar_prefetch=2,
        grid=(n_tiles,),
        in_specs=[pl.BlockSpec((MOE_TILE, D_MODEL), lambda i, te, tr: (i, 0)),
                  pl.BlockSpec((1, D_MODEL, F), lambda i, te, tr: (te[i], 0, 0)),
                  pl.BlockSpec((1, D_MODEL, F), lambda i, te, tr: (te[i], 0, 0)),
                  pl.BlockSpec((1, F, D_MODEL), lambda i, te, tr: (te[i], 0, 0))],
        out_specs=pl.BlockSpec((MOE_TILE, D_MODEL), lambda i, te, tr: (i, 0)),
        scratch_shapes=[pltpu.VMEM((D_MODEL, F), BF16), pltpu.VMEM((D_MODEL, F), BF16),
                        pltpu.VMEM((F, D_MODEL), BF16)])
    return pl.pallas_call(
        _moe_expert_kernel,
        grid_spec=grid_spec,
        out_shape=jax.ShapeDtypeStruct((n_tiles * MOE_TILE, D_MODEL), F32),
        compiler_params=_cparams(("arbitrary",)),
        name="moe_experts",
    )(tile_exp, tile_rows, xs, wg, wu, wd)


def _moe_combine_kernel(loff_ref, goff_ref, pc_ref, ridx_ref, rgate_ref, x_ref, ys_hbm, *rest, final):
    if final:
        gf_ref, o_ref, local_ref, sem = rest
    else:
        o_ref, local_ref, sem = rest
    j = pl.program_id(0)
    n = pl.num_programs(0)
    slot = j % 2

    def seg(jj, s):
        return (jj, pc_ref, loff_ref, goff_ref, local_ref.at[s], ys_hbm, sem.at[s], False)

    @pl.when(j == 0)
    def _prologue():
        _start_segments(*seg(j, slot))

    @pl.when(j + 1 < n)
    def _prefetch():
        _start_segments(*seg(j + 1, 1 - slot))

    pos0, pos1 = _local_positions(ridx_ref[...], loff_ref, j)
    col = lax.broadcasted_iota(jnp.int32, (MOE_TOK_TILE, MOE_LOCAL_ROWS), 1).astype(F32)
    pick0 = jnp.where(col == pos0, 1.0, 0.0).astype(BF16)
    pick1 = jnp.where(col == pos1, 1.0, 0.0).astype(BF16)
    used = loff_ref[j * N_EXPERTS + N_EXPERTS - 1] + pc_ref[j * N_EXPERTS + N_EXPERTS - 1]
    _wait_segments(*seg(j, slot))
    ri = lax.broadcasted_iota(jnp.int32, (MOE_LOCAL_ROWS, D_MODEL), 0)
    ys = jnp.where(ri < used, local_ref[slot], 0.0).astype(BF16)
    y01 = jnp.dot(jnp.concatenate([pick0, pick1], axis=0), ys, preferred_element_type=F32)
    y0, y1 = y01[:MOE_TOK_TILE], y01[MOE_TOK_TILE:]
    gates = rgate_ref[...]
    xo = x_ref[...] + gates[:, 0:1] * y0 + gates[:, 1:2] * y1
    o_ref[...] = _rms(xo, gf_ref[...]) if final else xo


def _moe_combine(x2, ys, ridx, rgate, loff, goff, pc, final_g):
    T = x2.shape[0]
    final = final_g is not None
    in_specs = [pl.BlockSpec((MOE_TOK_TILE, LANES), lambda j, *_: (j, 0)),
                pl.BlockSpec((MOE_TOK_TILE, LANES), lambda j, *_: (j, 0)),
                pl.BlockSpec((MOE_TOK_TILE, D_MODEL), lambda j, *_: (j, 0)),
                pl.BlockSpec(memory_space=pl.ANY)]
    args = [ridx, rgate, x2, ys]
    if final:
        in_specs.append(pl.BlockSpec((1, D_MODEL), lambda j, *_: (0, 0)))
        args.append(final_g)
    grid_spec = pltpu.PrefetchScalarGridSpec(
        num_scalar_prefetch=3,
        grid=(T // MOE_TOK_TILE,),
        in_specs=in_specs,
        out_specs=pl.BlockSpec((MOE_TOK_TILE, D_MODEL), lambda j, *_: (j, 0)),
        scratch_shapes=[pltpu.VMEM((2, MOE_LOCAL_ROWS, D_MODEL), F32), pltpu.SemaphoreType.DMA((2,))])
    return pl.pallas_call(
        functools.partial(_moe_combine_kernel, final=final),
        grid_spec=grid_spec,
        out_shape=jax.ShapeDtypeStruct((T, D_MODEL), F32),
        compiler_params=_cparams(("arbitrary",)),
        name="moe_combine",
    )(loff, goff, pc, *args)


def _moe(h, x2, ridx, rgate, wg, wu, wd, final_g):
    T = x2.shape[0]
    loff, goff, pc, gaps, tile_exp, tile_rows, n_tiles = _moe_offsets(ridx, T)
    xs = _moe_dispatch(ridx, h, loff, goff, pc, gaps, n_tiles * MOE_TILE)
    ys = _moe_experts(xs, tile_exp, tile_rows, wg, wu, wd)
    return _moe_combine(x2, ys, ridx, rgate, loff, goff, pc, final_g)


def kernel(x, norm_mix_g, w_in, shift_mu, rwkv_w0, rwkv_w2, rwkv_a0, rwkv_a2, rwkv_g2, rwkv_k_k, rwkv_k_a, rwkv_r_k, rwkv_ln_w, rwkv_ln_b, attn_rel_bias, attn_norm_g, w_out, norm_ffn_g, ffn_w_gate, ffn_w_up, ffn_w_down, moe_router, moe_w_gate, moe_w_up, moe_w_down, norm_final_g):
    B, S, D = x.shape
    depth = w_in.shape[0]
    T = B * S
    tm = min(512, T)
    row = lambda t: t.reshape(1, -1).astype(F32)

    hi = jnp.arange(LANES) // HEAD_DIM
    bd = (hi[:, None] == hi[None, :]).astype(BF16)
    rbt = jnp.pad(jnp.swapaxes(attn_rel_bias, 1, 2).astype(F32),
                  ((0, 0), (0, 0), (0, REL_ROWS - attn_rel_bias.shape[1])))
    bias_tabs = _bias_tables(rbt)

    x2 = x.reshape(T, D).astype(F32)
    for l in range(depth):
        ps, qkv = _inproj(x2, row(norm_mix_g[l]), w_in.astype(F32), l, tm)

        zeros = jnp.zeros((DECAY_LORA, RWKV_WIDTH), F32)
        wa = jnp.concatenate([jnp.concatenate([rwkv_w2[l], zeros], axis=1),
                              jnp.concatenate([zeros, rwkv_a2[l]], axis=1)], axis=0).astype(BF16)
        vec = jnp.stack([rwkv_w0[l], rwkv_a0[l], rwkv_k_k[l], rwkv_k_a[l], rwkv_r_k[l],
                         rwkv_ln_w[l], rwkv_ln_b[l], jnp.zeros_like(rwkv_w0[l])]).astype(F32)
        y_rwkv = _rwkv(ps.reshape(B, S, SHIFT_COLS), row(shift_mu[l]), vec, wa,
                       rwkv_g2[l].astype(BF16), bd)
        y_attn = _attn(qkv.reshape(B, S, 3 * ATTN_WIDTH), bias_tabs, l, row(attn_norm_g[l]))

        li = l // 2
        final_g = row(norm_final_g) if l == depth - 1 else None
        mixed = (y_rwkv.reshape(T, RWKV_WIDTH), y_attn.reshape(T, ATTN_WIDTH), x2,
                 w_out[l].astype(BF16), row(norm_ffn_g[l]))
        if l % 2 == 1:
            router = jnp.pad(moe_router[li].astype(F32), ((0, 0), (0, LANES - N_EXPERTS)))
            x_mid, h, ridx, rgate = _outproj(*mixed, router, tm)
            x2 = _moe(h, x_mid, ridx, rgate, moe_w_gate[li].astype(F32), moe_w_up[li].astype(F32),
                      moe_w_down[li].astype(F32), final_g)
        else:
            x2 = _ffn(*mixed, ffn_w_gate[li].astype(BF16), ffn_w_up[li].astype(BF16),
                      ffn_w_down[li].astype(BF16), final_g, tm)
    return x2.reshape(B, S, D).astype(x.dtype)
```

```python
import functools

import jax
import jax.numpy as jnp
from jax import lax
from jax.experimental import pallas as pl
from jax.experimental.pallas import tpu as pltpu

F32 = jnp.float32
BF16 = jnp.bfloat16

D_MODEL = 1024
CHUNK_LOG2 = 6
CHUNK = 1 << CHUNK_LOG2
N_LEFT_CHUNKS = 8
HEAD_DIM = 64
RWKV_WIDTH = 512
ATTN_WIDTH = 512
DECAY_LORA = 64
AAA_LORA = 64
GATE_LORA = 128
REL_CLIP = 128
N_EXPERTS = 8
RMS_EPS = 1e-6
GN_EPS = 64e-5
MASK_VALUE = -1e30
SHIFT_COLS = 3 * RWKV_WIDTH + DECAY_LORA + AAA_LORA + GATE_LORA

LANES = 128
SUBLANES = 8
ATTN_HEADS = ATTN_WIDTH // HEAD_DIM
PAIR = 2 * CHUNK
N_PAIRS = RWKV_WIDTH // LANES
ATTN_WINDOW = (N_LEFT_CHUNKS + 2) * CHUNK
N_BIAS_TABLES = N_LEFT_CHUNKS + 2
BIAS_BASE = 768
REL_ROWS = 384
VMEM_LIMIT = 56 * 1024 * 1024


def _cparams(sem):
    return pltpu.CompilerParams(dimension_semantics=sem, vmem_limit_bytes=VMEM_LIMIT)


def _mm(a, b):
    return jnp.dot(a.astype(BF16), b.astype(BF16), preferred_element_type=F32)


def _mm_nt(a, b):
    return lax.dot_general(a.astype(BF16), b.astype(BF16), (((1,), (1,)), ((), ())),
                           preferred_element_type=F32)


def _mm_tn(a, b):
    return lax.dot_general(a.astype(BF16), b.astype(BF16), (((0,), (0,)), ((), ())),
                           preferred_element_type=F32)


def _split_terms(x, n):
    terms, rem = [], x
    for _ in range(n):
        hi = rem.astype(BF16)
        terms.append(hi)
        rem = rem - hi.astype(F32)
    return terms


def _dot_exact_rhs(x, w_bf16, n):
    acc = None
    for t in _split_terms(x, n):
        d = jnp.dot(t, w_bf16, preferred_element_type=F32)
        acc = d if acc is None else acc + d
    return acc


def _dot_exact_lhs(w_bf16, x, n):
    acc = None
    for t in _split_terms(x, n):
        d = jnp.dot(w_bf16, t, preferred_element_type=F32)
        acc = d if acc is None else acc + d
    return acc


def _dot_f32(a, b):
    a1, a2 = _split_terms(a, 2)
    b1, b2 = _split_terms(b, 2)
    n = b.shape[1]
    t = jnp.dot(a1, jnp.concatenate([b1, b2], axis=1), preferred_element_type=F32)
    return t[:, :n] + t[:, n:] + jnp.dot(a2, b1, preferred_element_type=F32)


def _sigmoid(x):
    return 1.0 / (1.0 + jnp.exp(-x))


def _rms(x, g):
    return x * lax.rsqrt(jnp.mean(x * x, axis=-1, keepdims=True) + RMS_EPS) * g


def _inproj_kernel(x_ref, g_ref, w_ref, ps_ref, qkv_ref):
    hb = _rms(x_ref[...], g_ref[...]).astype(BF16)
    ps_ref[...] = jnp.dot(hb, w_ref[0, :, :SHIFT_COLS].astype(BF16), preferred_element_type=F32)
    qkv_ref[...] = jnp.dot(hb, w_ref[0, :, SHIFT_COLS:].astype(BF16), preferred_element_type=F32).astype(BF16)


def _inproj(x2, g, w_in, l, tm):
    T = x2.shape[0]
    n_in = w_in.shape[2]
    na = n_in - SHIFT_COLS
    return pl.pallas_call(
        _inproj_kernel,
        grid=(T // tm,),
        in_specs=[pl.BlockSpec((tm, D_MODEL), lambda i: (i, 0)),
                  pl.BlockSpec((1, D_MODEL), lambda i: (0, 0)),
                  pl.BlockSpec((1, D_MODEL, n_in), lambda i: (l, 0, 0))],
        out_specs=[pl.BlockSpec((tm, SHIFT_COLS), lambda i: (i, 0)),
                   pl.BlockSpec((tm, na), lambda i: (i, 0))],
        out_shape=[jax.ShapeDtypeStruct((T, SHIFT_COLS), F32),
                   jax.ShapeDtypeStruct((T, na), BF16)],
        compiler_params=_cparams(("parallel",)),
        name="inproj",
    )(x2, g, w_in)


def _rwkv_kernel(ps_ref, prev_ref, mu_ref, vec_ref, wa_ref, g2_ref, bd_ref, y_ref, h_ref):
    c = pl.program_id(1)

    @pl.when(c == 0)
    def _init():
        h_ref[...] = jnp.zeros(h_ref.shape, F32)

    p = ps_ref[0]
    last = jnp.where(c > 0, prev_ref[0][SUBLANES - 1:SUBLANES, :], 0.0)
    row = lax.broadcasted_iota(jnp.int32, p.shape, 0)
    prev = jnp.where(row == 0, last, pltpu.roll(p, 1, 0))
    xs = p + mu_ref[...] * (prev - p)

    W = RWKV_WIDTH
    r, k, v = xs[:, 0:W], xs[:, W:2 * W], xs[:, 2 * W:3 * W]
    z0 = xs[:, 3 * W:3 * W + LANES]
    gd = xs[:, 3 * W + LANES:3 * W + 2 * LANES]
    m1 = lax.broadcasted_iota(jnp.int32, (CHUNK, LANES), 1) < HEAD_DIM
    z0 = jnp.where(lax.broadcasted_iota(jnp.int32, z0.shape, 1) < DECAY_LORA, jnp.tanh(z0), z0)
    lora = _mm(z0, wa_ref[...])
    vec = vec_ref[...]
    w0, a0, k_k, k_a, r_k, ln_w, ln_b = (vec[i:i + 1] for i in range(7))
    w = w0 + lora[:, :W]
    a = _sigmoid(a0 + lora[:, W:])
    g = _mm(_sigmoid(gd), g2_ref[...])
    softplus_neg_w = jnp.maximum(-w, 0.0) + jnp.log(1.0 + jnp.exp(-jnp.abs(w)))
    lw = -jnp.exp(-softplus_neg_w - 0.5)

    bd = bd_ref[...]

    def head_sums(x):
        xb = x.astype(BF16)
        return jnp.concatenate(
            [jnp.dot(xb[:, LANES * i:LANES * (i + 1)], bd, preferred_element_type=F32) for i in range(N_PAIRS)],
            axis=1)

    kk = k * k_k
    kk = kk / jnp.maximum(jnp.sqrt(head_sums(kk * kk)), 1e-12)
    k2 = k * (1.0 + (a - 1.0) * k_a)
    kka = kk * a

    rows = p.shape[0]
    n_chunks = rows // CHUNK
    ti = lax.broadcasted_iota(jnp.int32, (rows, rows), 0)
    tj = lax.broadcasted_iota(jnp.int32, (rows, rows), 1)
    tri = jnp.where((ti >= tj) & ((ti >> CHUNK_LOG2) == (tj >> CHUNK_LOG2)), 1.0, 0.0).astype(BF16)
    L = _dot_exact_lhs(tri, lw, 2)

    ri = lax.broadcasted_iota(jnp.int32, (PAIR, PAIR), 0)
    ci = lax.broadcasted_iota(jnp.int32, (PAIR, PAIR), 1)
    same_head = (ri >> CHUNK_LOG2) == (ci >> CHUNK_LOG2)
    strict = same_head & (ri > ci)
    incl = same_head & (ri >= ci)
    eye = ri == ci
    eye_f = jnp.where(eye, 1.0, 0.0)

    units = []
    for j in range(n_chunks):
        rs = slice(CHUNK * j, CHUNK * (j + 1))
        Lj, lwj = L[rs], lw[rs]
        Lc = Lj[CHUNK - 1:CHUNK]
        inv = jnp.exp(-Lj)
        gC = jnp.exp(Lc)
        to_end = gC * inv
        Rt = r[rs] * jnp.exp(Lj)
        At = -kk[rs] * jnp.exp(Lj - lwj)
        Bt, Kt = kka[rs] * inv, k2[rs] * inv
        Bh, Kh = kka[rs] * to_end, k2[rs] * to_end
        vj = v[rs]
        for pi in range(N_PAIRS):
            sl = slice(LANES * pi, LANES * (pi + 1))

            def stack(x):
                xp = x[:, sl]
                return jnp.concatenate([jnp.where(m1, xp, 0.0), jnp.where(m1, 0.0, xp)], axis=0)

            def twice(x):
                xp = x[:, sl].astype(BF16)
                return jnp.concatenate([xp, xp], axis=0)

            units.append(dict(
                j=j, pi=pi, gC=gC[:, sl], sRt=stack(Rt),
                sAt=stack(At).astype(BF16), sV=stack(vj).astype(BF16),
                sBt=twice(Bt), sKt=twice(Kt), sBh=twice(Bh), sKh=twice(Kh)))

    for u in units:
        big = _mm_nt(jnp.concatenate([u["sAt"], u["sRt"].astype(BF16)], axis=0),
                     jnp.concatenate([u["sBt"], u["sKt"]], axis=0))
        u["AB"] = jnp.where(strict, big[:PAIR, :PAIR], 0.0)
        u["AK"] = jnp.where(strict, big[:PAIR, PAIR:], 0.0)
        u["RB"] = jnp.where(incl, big[PAIR:, :PAIR], 0.0)
        u["RK"] = jnp.where(incl, big[PAIR:, PAIR:], 0.0)
    for u in units:
        u["X"] = eye_f + u["AB"]
        u["Pw"] = _mm(u["AB"], u["AB"])
        u["W1"] = _mm(u["AK"], u["sV"])
    for _ in range(4):
        for u in units:
            PX = _mm(u["Pw"], jnp.concatenate([u["Pw"], u["X"]], axis=1))
            u["Pw"] = PX[:, :PAIR]
            u["X"] = u["X"] + PX[:, PAIR:]
    for u in units:
        u["Tm"] = u["X"] + _mm(u["Pw"], u["X"])
    for u in units:
        u["PQ"] = _mm(u["Tm"], jnp.concatenate([u["sAt"], u["W1"].astype(BF16)], axis=1)).astype(BF16)
    for u in units:
        PQ = u["PQ"]
        Pm, Q = PQ[:, :PAIR], PQ[:, PAIR:]
        RBPQ = _mm(u["RB"], PQ)
        u["Rp"] = u["sRt"] + RBPQ[:, :PAIR]
        u["Y0"] = RBPQ[:, PAIR:] + _mm(u["RK"], u["sV"])
        u["Mm"] = jnp.where(eye, u["gC"], 0.0) + jnp.where(same_head, _mm_tn(u["sBh"], Pm), 0.0)
        u["G"] = jnp.where(same_head, _mm_tn(jnp.concatenate([u["sBh"], u["sKh"]], axis=0),
                                             jnp.concatenate([Q, u["sV"]], axis=0)), 0.0)
    H = [h_ref[pi] for pi in range(N_PAIRS)]
    y_rows = []
    for j in range(n_chunks):
        ys = []
        for u in units[j * N_PAIRS:(j + 1) * N_PAIRS]:
            pi = u["pi"]
            YH = _mm(jnp.concatenate([u["Rp"], u["Mm"]], axis=0), H[pi])
            Ysm = YH[:PAIR] + u["Y0"]
            H[pi] = YH[PAIR:] + u["G"]
            ys.append(Ysm[:CHUNK] + Ysm[CHUNK:])
        y_rows.append(jnp.concatenate(ys, axis=1))
    for pi in range(N_PAIRS):
        h_ref[pi] = H[pi]
    y = y_rows[0] if n_chunks == 1 else jnp.concatenate(y_rows, axis=0)

    inv_n = 1.0 / HEAD_DIM
    mean = head_sums(y) * inv_n
    d = y - mean
    var = head_sums(d * d) * inv_n
    yn = d * lax.rsqrt(var + GN_EPS) * ln_w + ln_b
    bonus = head_sums(r * k2 * r_k) * v
    y_ref[0] = ((yn + bonus) * g).astype(BF16)


RWKV_BLOCK_CHUNKS = 4


def _rwkv(ps3, mu, vec, wa, g2, bd):
    B, S, _ = ps3.shape
    rows = RWKV_BLOCK_CHUNKS * CHUNK
    nc = S // rows
    rows8 = rows // SUBLANES
    return pl.pallas_call(
        _rwkv_kernel,
        grid=(B, nc),
        in_specs=[pl.BlockSpec((1, rows, SHIFT_COLS), lambda b, c: (b, c, 0)),
                  pl.BlockSpec((1, SUBLANES, SHIFT_COLS), lambda b, c: (b, jnp.maximum(c * rows8 - 1, 0), 0)),
                  pl.BlockSpec((1, SHIFT_COLS), lambda b, c: (0, 0)),
                  pl.BlockSpec((SUBLANES, RWKV_WIDTH), lambda b, c: (0, 0)),
                  pl.BlockSpec((LANES, 2 * RWKV_WIDTH), lambda b, c: (0, 0)),
                  pl.BlockSpec((GATE_LORA, RWKV_WIDTH), lambda b, c: (0, 0)),
                  pl.BlockSpec((LANES, LANES), lambda b, c: (0, 0))],
        out_specs=pl.BlockSpec((1, rows, RWKV_WIDTH), lambda b, c: (b, c, 0)),
        out_shape=jax.ShapeDtypeStruct((B, S, RWKV_WIDTH), BF16),
        scratch_shapes=[pltpu.VMEM((N_PAIRS, PAIR, LANES), F32)],
        compiler_params=_cparams(("parallel", "arbitrary")),
        name="rwkv7",
    )(ps3, ps3, mu, vec, wa, g2, bd)


def _bias_kernel(rbt_ref, o_ref):
    e = pl.program_id(1)
    xi = lax.broadcasted_iota(jnp.int32, (REL_ROWS, BIAS_BASE), 1)
    ji = lax.broadcasted_iota(jnp.int32, (REL_ROWS, BIAS_BASE), 0)
    off = jnp.where(xi < BIAS_BASE - CHUNK, xi, xi - BIAS_BASE)
    idx = jnp.clip(e * CHUNK - off, -REL_CLIP, REL_CLIP) + REL_CLIP
    onehot = jnp.where(idx == ji, 1.0, 0.0).astype(BF16)
    base = _dot_exact_rhs(rbt_ref[0], onehot, 3)
    kj = lax.broadcasted_iota(jnp.int32, (CHUNK, ATTN_WINDOW), 1)
    kc = kj >> CHUNK_LOG2
    valid = (kc <= e) & (kc >= e - N_LEFT_CHUNKS)
    for h in range(ATTN_HEADS):
        rows = jnp.broadcast_to(base[h:h + 1, :], (CHUNK, BIAS_BASE))
        toep = pltpu.roll(rows, 0, 1, stride=1, stride_axis=0)
        o_ref[0, 0, h * CHUNK:(h + 1) * CHUNK, :] = jnp.where(valid, toep[:, :ATTN_WINDOW], MASK_VALUE)


def _bias_tables(rbt):
    L = rbt.shape[0]
    return pl.pallas_call(
        _bias_kernel,
        grid=(L, N_BIAS_TABLES),
        in_specs=[pl.BlockSpec((1, 8, REL_ROWS), lambda l, e: (l, 0, 0))],
        out_specs=pl.BlockSpec((1, 1, ATTN_HEADS * CHUNK, ATTN_WINDOW), lambda l, e: (l, e, 0, 0)),
        out_shape=jax.ShapeDtypeStruct((L, N_BIAS_TABLES, ATTN_HEADS * CHUNK, ATTN_WINDOW), F32),
        compiler_params=_cparams(("parallel", "parallel")),
        name="bias_tables",
    )(rbt)


ATTN_BLOCK_CHUNKS = 8


def _attn_kernel(q_ref, k_ref, v_ref, *rest):
    bias_refs, g_ref, o_ref = rest[:ATTN_BLOCK_CHUNKS], rest[-2], rest[-1]
    n0 = pl.program_id(1) * ATTN_BLOCK_CHUNKS
    q = q_ref[0] * jnp.asarray(HEAD_DIM ** -0.5, BF16)
    m1 = lax.broadcasted_iota(jnp.int32, (CHUNK, LANES), 1) < HEAD_DIM
    zero = jnp.zeros((), BF16)
    n_pairs = ATTN_WIDTH // LANES
    units = []
    for j in range(ATTN_BLOCK_CHUNKS):
        start = pl.multiple_of(jnp.maximum(n0 + j - (N_LEFT_CHUNKS + 1), 0) * CHUNK, CHUNK)
        kw = k_ref[0, pl.ds(start, ATTN_WINDOW), :]
        vw = v_ref[0, pl.ds(start, ATTN_WINDOW), :]
        qj = q[CHUNK * j:CHUNK * (j + 1)]
        for pi in range(n_pairs):
            sl = slice(LANES * pi, LANES * (pi + 1))
            qp = qj[:, sl]
            qs = jnp.concatenate([jnp.where(m1, qp, zero), jnp.where(m1, zero, qp)], axis=0)
            units.append(dict(qs=qs, k=kw[:, sl], v=vw[:, sl],
                              bias=bias_refs[j][0, 0, PAIR * pi:PAIR * (pi + 1), :]))
    for u in units:
        u["s"] = lax.dot_general(u["qs"], u["k"], (((1,), (1,)), ((), ())),
                                 preferred_element_type=F32) + u["bias"]
    for u in units:
        s = u["s"]
        ex = jnp.exp(s - jnp.max(s, axis=1, keepdims=True))
        u["den"] = jnp.sum(ex, axis=1, keepdims=True)
        u["ex"] = ex.astype(BF16)
    for u in units:
        o = jnp.dot(u["ex"], u["v"], preferred_element_type=F32) / u["den"]
        u["o"] = jnp.where(m1, o[:CHUNK], o[CHUNK:])
    rows = [jnp.concatenate([u["o"] for u in units[j * n_pairs:(j + 1) * n_pairs]], axis=1)
            for j in range(ATTN_BLOCK_CHUNKS)]
    o = jnp.concatenate(rows, axis=0)
    o_ref[0] = _rms(o, g_ref[...]).astype(BF16)


def _attn(qkv3, bias_l, l, g):
    B, S, _ = qkv3.shape
    rows = ATTN_BLOCK_CHUNKS * CHUNK
    nc = S // rows

    def bias_spec(j):
        return pl.BlockSpec(
            (1, 1, ATTN_HEADS * CHUNK, ATTN_WINDOW),
            lambda b, n: (l, jnp.minimum(n * ATTN_BLOCK_CHUNKS + j, N_BIAS_TABLES - 1), 0, 0))

    return pl.pallas_call(
        _attn_kernel,
        grid=(B, nc),
        in_specs=[pl.BlockSpec((1, rows, ATTN_WIDTH), lambda b, n: (b, n, 0)),
                  pl.BlockSpec((1, S, ATTN_WIDTH), lambda b, n: (b, 0, 1)),
                  pl.BlockSpec((1, S, ATTN_WIDTH), lambda b, n: (b, 0, 2))]
                 + [bias_spec(j) for j in range(ATTN_BLOCK_CHUNKS)]
                 + [pl.BlockSpec((1, ATTN_WIDTH), lambda b, n: (0, 0))],
        out_specs=pl.BlockSpec((1, rows, ATTN_WIDTH), lambda b, n: (b, n, 0)),
        out_shape=jax.ShapeDtypeStruct((B, S, ATTN_WIDTH), BF16),
        compiler_params=_cparams(("parallel", "arbitrary")),
        name="chunk_attn",
    )(qkv3, qkv3, qkv3, *([bias_l] * ATTN_BLOCK_CHUNKS), g)


def _outproj_kernel(yr_ref, ya_ref, x_ref, w_ref, g_ref, router_ref, xo_ref, h_ref, ridx_ref, rgate_ref):
    y = jnp.concatenate([yr_ref[...], ya_ref[...]], axis=1)
    xn = x_ref[...] + jnp.dot(y, w_ref[...], preferred_element_type=F32)
    xo_ref[...] = xn
    h = _rms(xn, g_ref[...])
    h_ref[...] = h.astype(BF16)
    lane = lax.broadcasted_iota(jnp.int32, ridx_ref.shape, 1)
    neg = jnp.asarray(-jnp.inf, F32)
    lg = jnp.where(lane < N_EXPERTS, _dot_f32(h, router_ref[...]), neg)
    top1 = jnp.max(lg, axis=1, keepdims=True)
    idx1 = jnp.min(jnp.where(lg == top1, lane, LANES), axis=1, keepdims=True)
    lg2 = jnp.where(lane == idx1, neg, lg)
    top2 = jnp.max(lg2, axis=1, keepdims=True)
    idx2 = jnp.min(jnp.where(lg2 == top2, lane, LANES), axis=1, keepdims=True)
    ex = jnp.exp(top2 - top1)
    ridx_ref[...] = jnp.where(lane == 0, idx1, jnp.where(lane == 1, idx2, 0))
    rgate_ref[...] = jnp.where(lane == 0, 1.0 / (1.0 + ex), jnp.where(lane == 1, ex / (1.0 + ex), 0.0))


def _outproj(yr, ya, x2, w, g, router, tm):
    T = x2.shape[0]
    row_tile = lambda width: pl.BlockSpec((tm, width), lambda i: (i, 0))
    return pl.pallas_call(
        _outproj_kernel,
        grid=(T // tm,),
        in_specs=[row_tile(RWKV_WIDTH), row_tile(ATTN_WIDTH), row_tile(D_MODEL),
                  pl.BlockSpec((D_MODEL, D_MODEL), lambda i: (0, 0)),
                  pl.BlockSpec((1, D_MODEL), lambda i: (0, 0)),
                  pl.BlockSpec((D_MODEL, LANES), lambda i: (0, 0))],
        out_specs=[row_tile(D_MODEL), row_tile(D_MODEL), row_tile(LANES), row_tile(LANES)],
        out_shape=[jax.ShapeDtypeStruct((T, D_MODEL), F32), jax.ShapeDtypeStruct((T, D_MODEL), BF16),
                   jax.ShapeDtypeStruct((T, LANES), jnp.int32), jax.ShapeDtypeStruct((T, LANES), F32)],
        compiler_params=_cparams(("parallel",)),
        name="outproj_router",
    )(yr, ya, x2, w, g, router)


FF_TILE = 256


def _ffn_kernel(yr_ref, ya_ref, x_ref, wo_ref, g_ref, wg_ref, wu_ref, wd_ref, *rest, final):
    if final:
        gf_ref, o_ref = rest
    else:
        (o_ref,) = rest
    y = jnp.concatenate([yr_ref[...], ya_ref[...]], axis=1)
    xn = x_ref[...] + jnp.dot(y, wo_ref[...], preferred_element_type=F32)
    h = _rms(xn, g_ref[...]).astype(BF16)
    acc = None
    for f in range(0, wg_ref.shape[1], FF_TILE):
        gate = jnp.dot(h, wg_ref[:, f:f + FF_TILE], preferred_element_type=F32)
        up = jnp.dot(h, wu_ref[:, f:f + FF_TILE], preferred_element_type=F32)
        act = (gate * _sigmoid(gate) * up).astype(BF16)
        d = jnp.dot(act, wd_ref[f:f + FF_TILE, :], preferred_element_type=F32)
        acc = d if acc is None else acc + d
    xo = xn + acc
    o_ref[...] = _rms(xo, gf_ref[...]) if final else xo


def _ffn(yr, ya, x2, w_out, g, wg, wu, wd, final_g, tm):
    T = x2.shape[0]
    F = wg.shape[1]
    final = final_g is not None
    in_specs = [pl.BlockSpec((tm, RWKV_WIDTH), lambda i: (i, 0)),
                pl.BlockSpec((tm, ATTN_WIDTH), lambda i: (i, 0)),
                pl.BlockSpec((tm, D_MODEL), lambda i: (i, 0)),
                pl.BlockSpec((D_MODEL, D_MODEL), lambda i: (0, 0)),
                pl.BlockSpec((1, D_MODEL), lambda i: (0, 0)),
                pl.BlockSpec((D_MODEL, F), lambda i: (0, 0)),
                pl.BlockSpec((D_MODEL, F), lambda i: (0, 0)),
                pl.BlockSpec((F, D_MODEL), lambda i: (0, 0))]
    args = [yr, ya, x2, w_out, g, wg, wu, wd]
    if final:
        in_specs.append(pl.BlockSpec((1, D_MODEL), lambda i: (0, 0)))
        args.append(final_g)
    return pl.pallas_call(
        functools.partial(_ffn_kernel, final=final),
        grid=(T // tm,),
        in_specs=in_specs,
        out_specs=pl.BlockSpec((tm, D_MODEL), lambda i: (i, 0)),
        out_shape=jax.ShapeDtypeStruct((T, D_MODEL), F32),
        compiler_params=_cparams(("parallel",)),
        name="ffn_dense",
    )(*args)


MOE_TILE = 512
MOE_TOK_TILE = 256
SEG_ALIGN = 8
MOE_LOCAL_ROWS = 640
SEG_PIECES = (256, 128, 64, 32, 16, 8)
GAP_PIECES = tuple(b for b in SEG_PIECES if b < MOE_TILE)
WAIT_PIECES = (2 * MOE_TOK_TILE,) + SEG_PIECES
TOP_K = 2


def _moe_offsets(ridx, T):
    nt = T // MOE_TOK_TILE
    e12 = ridx[:, :TOP_K]
    onehot = (e12[:, :, None] == jnp.arange(N_EXPERTS, dtype=jnp.int32)[None, None, :]).astype(jnp.int32)
    cnt = onehot.sum(axis=1).reshape(nt, MOE_TOK_TILE, N_EXPERTS).sum(axis=1)
    pc = ((cnt + SEG_ALIGN - 1) // SEG_ALIGN) * SEG_ALIGN
    loff = jnp.cumsum(pc, axis=1) - pc
    tot = pc.sum(axis=0)
    grp = ((tot + MOE_TILE - 1) // MOE_TILE) * MOE_TILE
    gend = jnp.cumsum(grp)
    gstart = gend - grp
    goff = gstart[None, :] + jnp.cumsum(pc, axis=0) - pc
    n_tiles = -(-(TOP_K * T + (SEG_ALIGN - 1) * N_EXPERTS * nt) // MOE_TILE) + N_EXPERTS
    tile_start = jnp.arange(n_tiles, dtype=jnp.int32) * MOE_TILE
    tile_exp = jnp.minimum(jnp.sum((tile_start[:, None] >= gend[None, :]).astype(jnp.int32), axis=1),
                           N_EXPERTS - 1)
    tile_rows = jnp.clip((gstart + tot)[tile_exp] - tile_start, 0, MOE_TILE)
    flat = lambda t: t.reshape(-1).astype(jnp.int32)
    used_tiles = gend[-1] // MOE_TILE
    gaps = jnp.concatenate([gstart + tot, grp - tot, jnp.stack([used_tiles, n_tiles - used_tiles])])
    return (flat(loff), flat(goff), flat(pc), flat(gaps), tile_exp.astype(jnp.int32),
            tile_rows.astype(jnp.int32), n_tiles)


def _local_positions(ridx, loff_ref, j):
    rows = ridx.shape[0]
    lane = lax.broadcasted_iota(jnp.int32, (rows, LANES), 1)
    oh0 = lane == ridx[:, 0:1]
    oh1 = lane == ridx[:, 1:2]
    ti = lax.broadcasted_iota(jnp.int32, (rows, rows), 0)
    tj = lax.broadcasted_iota(jnp.int32, (rows, rows), 1)
    before = jnp.where(tj < ti, 1.0, 0.0).astype(BF16)
    f0 = jnp.where(oh0, 1.0, 0.0)
    f1 = jnp.where(oh1, 1.0, 0.0)
    pre0 = jnp.dot(before, f0.astype(BF16), preferred_element_type=F32)
    pre1 = jnp.dot(before, f1.astype(BF16), preferred_element_type=F32)
    c0 = jnp.sum(f0, axis=0, keepdims=True)
    lane1 = lax.broadcasted_iota(jnp.int32, (1, LANES), 1)
    loff = jnp.zeros((1, LANES), F32)
    for e in range(N_EXPERTS):
        loff = jnp.where(lane1 == e, loff_ref[j * N_EXPERTS + e].astype(F32), loff)
    pos0 = jnp.sum(jnp.where(oh0, loff + pre0, 0.0), axis=1, keepdims=True)
    pos1 = jnp.sum(jnp.where(oh1, loff + c0 + pre1, 0.0), axis=1, keepdims=True)
    return pos0, pos1


def _segment_copies(j, pc_ref, loff_ref, goff_ref, local_ref, hbm_ref, sem, to_hbm):
    out = []
    for e in range(N_EXPERTS):
        n = pc_ref[j * N_EXPERTS + e]
        lo = loff_ref[j * N_EXPERTS + e]
        go = goff_ref[j * N_EXPERTS + e]
        for b in SEG_PIECES:
            done = n & ~(2 * b - 1)
            loc = local_ref.at[pl.ds(pl.multiple_of(lo + done, SEG_ALIGN), b), :]
            glob = hbm_ref.at[pl.ds(pl.multiple_of(go + done, SEG_ALIGN), b), :]
            cp = pltpu.make_async_copy(loc, glob, sem) if to_hbm else pltpu.make_async_copy(glob, loc, sem)
            out.append(((n & b) != 0, cp))
    return out


def _start_segments(*args):
    for cond, cp in _segment_copies(*args):
        pl.when(cond)(cp.start)


def _wait_segments(j, pc_ref, loff_ref, goff_ref, local_ref, hbm_ref, sem, to_hbm):
    del goff_ref
    last = j * N_EXPERTS + N_EXPERTS - 1
    total = loff_ref[last] + pc_ref[last]
    for b in WAIT_PIECES:
        loc = local_ref.at[pl.ds(0, b), :]
        glob = hbm_ref.at[pl.ds(0, b), :]
        cp = pltpu.make_async_copy(loc, glob, sem) if to_hbm else pltpu.make_async_copy(glob, loc, sem)
        pl.when((total & b) != 0)(cp.wait)


def _gap_copies(gaps_ref, zero_ref, xs_hbm, sem):
    out = []
    for e in range(N_EXPERTS):
        start = gaps_ref[e]
        n = gaps_ref[N_EXPERTS + e]
        for b in GAP_PIECES:
            done = n & ~(2 * b - 1)
            dst = xs_hbm.at[pl.ds(pl.multiple_of(start + done, SEG_ALIGN), b), :]
            out.append(((n & b) != 0, pltpu.make_async_copy(zero_ref.at[pl.ds(0, b), :], dst, sem)))
    return out


def _zero_fill_gaps(gaps_ref, zero_ref, xs_hbm, sem):
    zero_ref[...] = jnp.zeros(zero_ref.shape, F32)
    first_tile = gaps_ref[2 * N_EXPERTS]
    n_tail = gaps_ref[2 * N_EXPERTS + 1]

    def tail_copy(i):
        row0 = pl.multiple_of((first_tile + i) * MOE_TILE, MOE_TILE)
        return pltpu.make_async_copy(zero_ref, xs_hbm.at[pl.ds(row0, MOE_TILE), :], sem)

    for cond, cp in _gap_copies(gaps_ref, zero_ref, xs_hbm, sem):
        pl.when(cond)(cp.start)
    lax.fori_loop(0, n_tail, lambda i, c: (tail_copy(i).start(), c)[1], 0)
    for cond, cp in _gap_copies(gaps_ref, zero_ref, xs_hbm, sem):
        pl.when(cond)(cp.wait)
    lax.fori_loop(0, n_tail, lambda i, c: (tail_copy(i).wait(), c)[1], 0)


def _moe_dispatch_kernel(loff_ref, goff_ref, pc_ref, gaps_ref, ridx_ref, h_ref, xs_hbm, local_ref, zero_ref,
                         sem, zsem):
    j = pl.program_id(0)
    n = pl.num_programs(0)
    slot = j % 2

    @pl.when(j == 0)
    def _gaps():
        _zero_fill_gaps(gaps_ref, zero_ref, xs_hbm, zsem.at[0])

    pos0, pos1 = _local_positions(ridx_ref[...], loff_ref, j)
    col = lax.broadcasted_iota(jnp.int32, (MOE_TOK_TILE, MOE_LOCAL_ROWS), 1).astype(F32)
    place = jnp.where((col == pos0) | (col == pos1), 1.0, 0.0).astype(BF16)
    xs = lax.dot_general(place, h_ref[...], (((0,), (0,)), ((), ())), preferred_element_type=F32)

    def seg(jj, s):
        return (jj, pc_ref, loff_ref, goff_ref, local_ref.at[s], xs_hbm, sem.at[s], True)

    @pl.when(j >= 2)
    def _reuse():
        _wait_segments(*seg(j - 2, slot))

    local_ref[slot] = xs
    _start_segments(*seg(j, slot))

    @pl.when(j == n - 1)
    def _drain():
        _wait_segments(*seg(j, slot))

        @pl.when(n >= 2)
        def _():
            _wait_segments(*seg(j - 1, 1 - slot))


def _moe_dispatch(ridx, h, loff, goff, pc, gaps, n_rows):
    T = h.shape[0]
    grid_spec = pltpu.PrefetchScalarGridSpec(
        num_scalar_prefetch=4,
        grid=(T // MOE_TOK_TILE,),
        in_specs=[pl.BlockSpec((MOE_TOK_TILE, LANES), lambda j, *_: (j, 0)),
                  pl.BlockSpec((MOE_TOK_TILE, D_MODEL), lambda j, *_: (j, 0))],
        out_specs=pl.BlockSpec(memory_space=pl.ANY),
        scratch_shapes=[pltpu.VMEM((2, MOE_LOCAL_ROWS, D_MODEL), F32), pltpu.VMEM((MOE_TILE, D_MODEL), F32),
                        pltpu.SemaphoreType.DMA((2,)), pltpu.SemaphoreType.DMA((1,))])
    return pl.pallas_call(
        _moe_dispatch_kernel,
        grid_spec=grid_spec,
        out_shape=jax.ShapeDtypeStruct((n_rows, D_MODEL), F32),
        compiler_params=_cparams(("arbitrary",)),
        name="moe_dispatch",
    )(loff, goff, pc, gaps, ridx, h)


def _moe_expert_kernel(texp_ref, trows_ref, xs_ref, wg_ref, wu_ref, wd_ref, ys_ref):
    del texp_ref
    rows = trows_ref[pl.program_id(0)]

    @pl.when(rows > 0)
    def _compute():
        x = xs_ref[...].astype(BF16)
        gate = jnp.dot(x, wg_ref[0].astype(BF16), preferred_element_type=F32)
        up = jnp.dot(x, wu_ref[0].astype(BF16), preferred_element_type=F32)
        act = (gate * _sigmoid(gate) * up).astype(BF16)
        ys_ref[...] = jnp.dot(act, wd_ref[0].astype(BF16), preferred_element_type=F32)

    @pl.when(rows == 0)
    def _skip():
        ys_ref[...] = jnp.zeros(ys_ref.shape, F32)


def _moe_experts(xs, tile_exp, tile_rows, wg, wu, wd):
    n_tiles = tile_exp.shape[0]
    _, _, F = wg.shape
    grid_spec = pltpu.PrefetchScalarGridSpec(
        num_scalar_prefetch=2,
        grid=(n_tiles,),
        in_specs=[pl.BlockSpec((MOE_TILE, D_MODEL), lambda i, te, tr: (i, 0)),
                  pl.BlockSpec((1, D_MODEL, F), lambda i, te, tr: (te[i], 0, 0)),
                  pl.BlockSpec((1, D_MODEL, F), lambda i, te, tr: (te[i], 0, 0)),
                  pl.BlockSpec((1, F, D_MODEL), lambda i, te, tr: (te[i], 0, 0))],
        out_specs=pl.BlockSpec((MOE_TILE, D_MODEL), lambda i, te, tr: (i, 0)))
    return pl.pallas_call(
        _moe_expert_kernel,
        grid_spec=grid_spec,
        out_shape=jax.ShapeDtypeStruct((n_tiles * MOE_TILE, D_MODEL), F32),
        compiler_params=_cparams(("arbitrary",)),
        name="moe_experts",
    )(tile_exp, tile_rows, xs, wg, wu, wd)


def _moe_combine_kernel(loff_ref, goff_ref, pc_ref, ridx_ref, rgate_ref, x_ref, ys_hbm, *rest, final):
    if final:
        gf_ref, o_ref, local_ref, sem = rest
    else:
        o_ref, local_ref, sem = rest
    j = pl.program_id(0)
    n = pl.num_programs(0)
    slot = j % 2

    def seg(jj, s):
        return (jj, pc_ref, loff_ref, goff_ref, local_ref.at[s], ys_hbm, sem.at[s], False)

    @pl.when(j == 0)
    def _prologue():
        _start_segments(*seg(j, slot))

    @pl.when(j + 1 < n)
    def _prefetch():
        _start_segments(*seg(j + 1, 1 - slot))

    pos0, pos1 = _local_positions(ridx_ref[...], loff_ref, j)
    col = lax.broadcasted_iota(jnp.int32, (MOE_TOK_TILE, MOE_LOCAL_ROWS), 1).astype(F32)
    pick0 = jnp.where(col == pos0, 1.0, 0.0).astype(BF16)
    pick1 = jnp.where(col == pos1, 1.0, 0.0).astype(BF16)
    used = loff_ref[j * N_EXPERTS + N_EXPERTS - 1] + pc_ref[j * N_EXPERTS + N_EXPERTS - 1]
    _wait_segments(*seg(j, slot))
    ri = lax.broadcasted_iota(jnp.int32, (MOE_LOCAL_ROWS, D_MODEL), 0)
    ys = jnp.where(ri < used, local_ref[slot], 0.0).astype(BF16)
    y0 = jnp.dot(pick0, ys, preferred_element_type=F32)
    y1 = jnp.dot(pick1, ys, preferred_element_type=F32)
    gates = rgate_ref[...]
    xo = x_ref[...] + gates[:, 0:1] * y0 + gates[:, 1:2] * y1
    o_ref[...] = _rms(xo, gf_ref[...]) if final else xo


def _moe_combine(x2, ys, ridx, rgate, loff, goff, pc, final_g):
    T = x2.shape[0]
    final = final_g is not None
    in_specs = [pl.BlockSpec((MOE_TOK_TILE, LANES), lambda j, *_: (j, 0)),
                pl.BlockSpec((MOE_TOK_TILE, LANES), lambda j, *_: (j, 0)),
                pl.BlockSpec((MOE_TOK_TILE, D_MODEL), lambda j, *_: (j, 0)),
                pl.BlockSpec(memory_space=pl.ANY)]
    args = [ridx, rgate, x2, ys]
    if final:
        in_specs.append(pl.BlockSpec((1, D_MODEL), lambda j, *_: (0, 0)))
        args.append(final_g)
    grid_spec = pltpu.PrefetchScalarGridSpec(
        num_scalar_prefetch=3,
        grid=(T // MOE_TOK_TILE,),
        in_specs=in_specs,
        out_specs=pl.BlockSpec((MOE_TOK_TILE, D_MODEL), lambda j, *_: (j, 0)),
        scratch_shapes=[pltpu.VMEM((2, MOE_LOCAL_ROWS, D_MODEL), F32), pltpu.SemaphoreType.DMA((2,))])
    return pl.pallas_call(
        functools.partial(_moe_combine_kernel, final=final),
        grid_spec=grid_spec,
        out_shape=jax.ShapeDtypeStruct((T, D_MODEL), F32),
        compiler_params=_cparams(("arbitrary",)),
        name="moe_combine",
    )(loff, goff, pc, *args)


def _moe(h, x2, ridx, rgate, wg, wu, wd, final_g):
    T = x2.shape[0]
    loff, goff, pc, gaps, tile_exp, tile_rows, n_tiles = _moe_offsets(ridx, T)
    xs = _moe_dispatch(ridx, h, loff, goff, pc, gaps, n_tiles * MOE_TILE)
    ys = _moe_experts(xs, tile_exp, tile_rows, wg, wu, wd)
    return _moe_combine(x2, ys, ridx, rgate, loff, goff, pc, final_g)


def kernel(x, norm_mix_g, w_in, shift_mu, rwkv_w0, rwkv_w2, rwkv_a0, rwkv_a2, rwkv_g2, rwkv_k_k, rwkv_k_a, rwkv_r_k, rwkv_ln_w, rwkv_ln_b, attn_rel_bias, attn_norm_g, w_out, norm_ffn_g, ffn_w_gate, ffn_w_up, ffn_w_down, moe_router, moe_w_gate, moe_w_up, moe_w_down, norm_final_g):
    B, S, D = x.shape
    depth = w_in.shape[0]
    T = B * S
    tm = min(512, T)
    row = lambda t: t.reshape(1, -1).astype(F32)

    hi = jnp.arange(LANES) // HEAD_DIM
    bd = (hi[:, None] == hi[None, :]).astype(BF16)
    rbt = jnp.pad(jnp.swapaxes(attn_rel_bias, 1, 2).astype(F32),
                  ((0, 0), (0, 0), (0, REL_ROWS - attn_rel_bias.shape[1])))
    bias_tabs = _bias_tables(rbt)

    x2 = x.reshape(T, D).astype(F32)
    for l in range(depth):
        ps, qkv = _inproj(x2, row(norm_mix_g[l]), w_in.astype(F32), l, tm)

        zeros = jnp.zeros((DECAY_LORA, RWKV_WIDTH), F32)
        wa = jnp.concatenate([jnp.concatenate([rwkv_w2[l], zeros], axis=1),
                              jnp.concatenate([zeros, rwkv_a2[l]], axis=1)], axis=0).astype(BF16)
        vec = jnp.stack([rwkv_w0[l], rwkv_a0[l], rwkv_k_k[l], rwkv_k_a[l], rwkv_r_k[l],
                         rwkv_ln_w[l], rwkv_ln_b[l], jnp.zeros_like(rwkv_w0[l])]).astype(F32)
        y_rwkv = _rwkv(ps.reshape(B, S, SHIFT_COLS), row(shift_mu[l]), vec, wa,
                       rwkv_g2[l].astype(BF16), bd)
        y_attn = _attn(qkv.reshape(B, S, 3 * ATTN_WIDTH), bias_tabs, l, row(attn_norm_g[l]))

        li = l // 2
        final_g = row(norm_final_g) if l == depth - 1 else None
        mixed = (y_rwkv.reshape(T, RWKV_WIDTH), y_attn.reshape(T, ATTN_WIDTH), x2,
                 w_out[l].astype(BF16), row(norm_ffn_g[l]))
        if l % 2 == 1:
            router = jnp.pad(moe_router[li].astype(F32), ((0, 0), (0, LANES - N_EXPERTS)))
            x_mid, h, ridx, rgate = _outproj(*mixed, router, tm)
            x2 = _moe(h, x_mid, ridx, rgate, moe_w_gate[li].astype(F32), moe_w_up[li].astype(F32),
                      moe_w_down[li].astype(F32), final_g)
        else:
            x2 = _ffn(*mixed, ffn_w_gate[li].astype(BF16), ffn_w_up[li].astype(BF16),
                      ffn_w_down[li].astype(BF16), final_g, tm)
    return x2.reshape(B, S, D).astype(x.dtype)
```

```python
import functools

import jax
import jax.numpy as jnp
from jax import lax
from jax.experimental import pallas as pl
from jax.experimental.pallas import tpu as pltpu

F32 = jnp.float32
BF16 = jnp.bfloat16

D_MODEL = 1024
CHUNK_LOG2 = 6
CHUNK = 1 << CHUNK_LOG2
N_LEFT_CHUNKS = 8
HEAD_DIM = 64
RWKV_WIDTH = 512
ATTN_WIDTH = 512
DECAY_LORA = 64
AAA_LORA = 64
GATE_LORA = 128
REL_CLIP = 128
N_EXPERTS = 8
RMS_EPS = 1e-6
GN_EPS = 64e-5
MASK_VALUE = -1e30
DECAY_SCALE = 0.6065306597126334
SHIFT_COLS = 3 * RWKV_WIDTH + DECAY_LORA + AAA_LORA + GATE_LORA

LANES = 128
SUBLANES = 8
ATTN_HEADS = ATTN_WIDTH // HEAD_DIM
PAIR = 2 * CHUNK
N_PAIRS = RWKV_WIDTH // LANES
ATTN_WINDOW = (N_LEFT_CHUNKS + 2) * CHUNK
N_BIAS_TABLES = N_LEFT_CHUNKS + 2
BIAS_BASE = 768
REL_ROWS = 384
VMEM_LIMIT = 56 * 1024 * 1024


def _cparams(sem):
    return pltpu.CompilerParams(dimension_semantics=sem, vmem_limit_bytes=VMEM_LIMIT)


def _mm(a, b):
    return jnp.dot(a.astype(BF16), b.astype(BF16), preferred_element_type=F32)


def _mm_nt(a, b):
    return lax.dot_general(a.astype(BF16), b.astype(BF16), (((1,), (1,)), ((), ())),
                           preferred_element_type=F32)


def _mm_tn(a, b):
    return lax.dot_general(a.astype(BF16), b.astype(BF16), (((0,), (0,)), ((), ())),
                           preferred_element_type=F32)


def _split_terms(x, n):
    terms, rem = [], x
    for _ in range(n):
        hi = rem.astype(BF16)
        terms.append(hi)
        rem = rem - hi.astype(F32)
    return terms


def _dot_exact_rhs(x, w_bf16, n):
    acc = None
    for t in _split_terms(x, n):
        d = jnp.dot(t, w_bf16, preferred_element_type=F32)
        acc = d if acc is None else acc + d
    return acc


def _dot_exact_lhs(w_bf16, x, n):
    acc = None
    for t in _split_terms(x, n):
        d = jnp.dot(w_bf16, t, preferred_element_type=F32)
        acc = d if acc is None else acc + d
    return acc


def _dot_f32(a, b):
    a1, a2 = _split_terms(a, 2)
    b1, b2 = _split_terms(b, 2)
    n = b.shape[1]
    t = jnp.dot(a1, jnp.concatenate([b1, b2], axis=1), preferred_element_type=F32)
    return t[:, :n] + t[:, n:] + jnp.dot(a2, b1, preferred_element_type=F32)


def _sigmoid(x):
    return 1.0 / (1.0 + jnp.exp(-x))


def _rms(x, g):
    return x * lax.rsqrt(jnp.mean(x * x, axis=-1, keepdims=True) + RMS_EPS) * g


def _inproj_kernel(x_ref, g_ref, w_ref, ps_ref, qkv_ref):
    hb = _rms(x_ref[...], g_ref[...]).astype(BF16)
    ps_ref[...] = jnp.dot(hb, w_ref[0, :, :SHIFT_COLS].astype(BF16), preferred_element_type=F32)
    qkv_ref[...] = jnp.dot(hb, w_ref[0, :, SHIFT_COLS:].astype(BF16), preferred_element_type=F32).astype(BF16)


def _inproj(x2, g, w_in, l, tm):
    T = x2.shape[0]
    n_in = w_in.shape[2]
    na = n_in - SHIFT_COLS
    return pl.pallas_call(
        _inproj_kernel,
        grid=(T // tm,),
        in_specs=[pl.BlockSpec((tm, D_MODEL), lambda i: (i, 0)),
                  pl.BlockSpec((1, D_MODEL), lambda i: (0, 0)),
                  pl.BlockSpec((1, D_MODEL, n_in), lambda i: (l, 0, 0))],
        out_specs=[pl.BlockSpec((tm, SHIFT_COLS), lambda i: (i, 0)),
                   pl.BlockSpec((tm, na), lambda i: (i, 0))],
        out_shape=[jax.ShapeDtypeStruct((T, SHIFT_COLS), F32),
                   jax.ShapeDtypeStruct((T, na), BF16)],
        compiler_params=_cparams(("parallel",)),
        name="inproj",
    )(x2, g, w_in)


def _rwkv_kernel(ps_ref, prev_ref, mu_ref, vec_ref, wa_ref, g2_ref, bd_ref, y_ref, h_ref):
    c = pl.program_id(1)

    @pl.when(c == 0)
    def _init():
        h_ref[...] = jnp.zeros(h_ref.shape, F32)

    p = ps_ref[0]
    last = jnp.where(c > 0, prev_ref[0][SUBLANES - 1:SUBLANES, :], 0.0)
    row = lax.broadcasted_iota(jnp.int32, p.shape, 0)
    prev = jnp.where(row == 0, last, pltpu.roll(p, 1, 0))
    xs = p + mu_ref[...] * (prev - p)

    W = RWKV_WIDTH
    r, k, v = xs[:, 0:W], xs[:, W:2 * W], xs[:, 2 * W:3 * W]
    z0 = xs[:, 3 * W:3 * W + LANES]
    gd = xs[:, 3 * W + LANES:3 * W + 2 * LANES]
    m1 = lax.broadcasted_iota(jnp.int32, (CHUNK, LANES), 1) < HEAD_DIM
    z0 = jnp.where(lax.broadcasted_iota(jnp.int32, z0.shape, 1) < DECAY_LORA, jnp.tanh(z0), z0)
    lora = _mm(z0, wa_ref[...])
    vec = vec_ref[...]
    w0, a0, k_k, k_a, r_k, ln_w, ln_b = (vec[i:i + 1] for i in range(7))
    w = w0 + lora[:, :W]
    a = _sigmoid(a0 + lora[:, W:])
    g = _mm(_sigmoid(gd), g2_ref[...])
    lw = -DECAY_SCALE * _sigmoid(w)

    bd = bd_ref[...]

    def head_sums(x):
        xb = x.astype(BF16)
        return jnp.concatenate(
            [jnp.dot(xb[:, LANES * i:LANES * (i + 1)], bd, preferred_element_type=F32) for i in range(N_PAIRS)],
            axis=1)

    kk = k * k_k
    kk = kk / jnp.maximum(jnp.sqrt(head_sums(kk * kk)), 1e-12)
    k2 = k * (1.0 + (a - 1.0) * k_a)
    kka = kk * a

    rows = p.shape[0]
    n_chunks = rows // CHUNK
    ti = lax.broadcasted_iota(jnp.int32, (rows, rows), 0)
    tj = lax.broadcasted_iota(jnp.int32, (rows, rows), 1)
    tri = jnp.where((ti >= tj) & ((ti >> CHUNK_LOG2) == (tj >> CHUNK_LOG2)), 1.0, 0.0).astype(BF16)
    L = _dot_exact_lhs(tri, lw, 2)

    ri = lax.broadcasted_iota(jnp.int32, (PAIR, PAIR), 0)
    ci = lax.broadcasted_iota(jnp.int32, (PAIR, PAIR), 1)
    same_head = (ri >> CHUNK_LOG2) == (ci >> CHUNK_LOG2)
    strict = same_head & (ri > ci)
    incl = same_head & (ri >= ci)
    eye = ri == ci
    eye_f = jnp.where(eye, 1.0, 0.0)

    units = []
    for j in range(n_chunks):
        rs = slice(CHUNK * j, CHUNK * (j + 1))
        Lj, lwj = L[rs], lw[rs]
        Lc = Lj[CHUNK - 1:CHUNK]
        inv = jnp.exp(-Lj)
        gC = jnp.exp(Lc)
        to_end = gC * inv
        Rt = r[rs] * jnp.exp(Lj)
        At = -kk[rs] * jnp.exp(Lj - lwj)
        Bt, Kt = kka[rs] * inv, k2[rs] * inv
        Bh, Kh = kka[rs] * to_end, k2[rs] * to_end
        vj = v[rs]
        for pi in range(N_PAIRS):
            sl = slice(LANES * pi, LANES * (pi + 1))

            def stack(x):
                xp = x[:, sl]
                return jnp.concatenate([jnp.where(m1, xp, 0.0), jnp.where(m1, 0.0, xp)], axis=0)

            def twice(x):
                xp = x[:, sl].astype(BF16)
                return jnp.concatenate([xp, xp], axis=0)

            units.append(dict(
                j=j, pi=pi, gC=gC[:, sl], sRt=stack(Rt),
                sAt=stack(At).astype(BF16), sV=stack(vj).astype(BF16),
                sBt=twice(Bt), sKt=twice(Kt), sBh=twice(Bh), sKh=twice(Kh)))

    for u in units:
        big = _mm_nt(jnp.concatenate([u["sAt"], u["sRt"].astype(BF16)], axis=0),
                     jnp.concatenate([u["sBt"], u["sKt"]], axis=0))
        u["AB"] = jnp.where(strict, big[:PAIR, :PAIR], 0.0)
        u["AK"] = jnp.where(strict, big[:PAIR, PAIR:], 0.0)
        u["RB"] = jnp.where(incl, big[PAIR:, :PAIR], 0.0)
        u["RK"] = jnp.where(incl, big[PAIR:, PAIR:], 0.0)
    for u in units:
        u["X"] = eye_f + u["AB"]
        u["Pw"] = _mm(u["AB"], u["AB"])
        u["W1"] = _mm(u["AK"], u["sV"])
    for _ in range(4):
        for u in units:
            PX = _mm(u["Pw"], jnp.concatenate([u["Pw"], u["X"]], axis=1))
            u["Pw"] = PX[:, :PAIR]
            u["X"] = u["X"] + PX[:, PAIR:]
    for u in units:
        u["Tm"] = u["X"] + _mm(u["Pw"], u["X"])
    for u in units:
        u["PQ"] = _mm(u["Tm"], jnp.concatenate([u["sAt"], u["W1"].astype(BF16)], axis=1)).astype(BF16)
    for u in units:
        PQ = u["PQ"]
        Pm, Q = PQ[:, :PAIR], PQ[:, PAIR:]
        RBPQ = _mm(u["RB"], PQ)
        u["Rp"] = u["sRt"] + RBPQ[:, :PAIR]
        u["Y0"] = RBPQ[:, PAIR:] + _mm(u["RK"], u["sV"])
        u["Mm"] = jnp.where(eye, u["gC"], 0.0) + jnp.where(same_head, _mm_tn(u["sBh"], Pm), 0.0)
        u["G"] = jnp.where(same_head, _mm_tn(jnp.concatenate([u["sBh"], u["sKh"]], axis=0),
                                             jnp.concatenate([Q, u["sV"]], axis=0)), 0.0)
    H = [h_ref[pi] for pi in range(N_PAIRS)]
    y_rows = []
    for j in range(n_chunks):
        ys = []
        for u in units[j * N_PAIRS:(j + 1) * N_PAIRS]:
            pi = u["pi"]
            YH = _mm(jnp.concatenate([u["Rp"], u["Mm"]], axis=0), H[pi])
            Ysm = YH[:PAIR] + u["Y0"]
            H[pi] = YH[PAIR:] + u["G"]
            ys.append(Ysm[:CHUNK] + Ysm[CHUNK:])
        y_rows.append(jnp.concatenate(ys, axis=1))
    for pi in range(N_PAIRS):
        h_ref[pi] = H[pi]
    y = y_rows[0] if n_chunks == 1 else jnp.concatenate(y_rows, axis=0)

    inv_n = 1.0 / HEAD_DIM
    mean = head_sums(y) * inv_n
    d = y - mean
    var = head_sums(d * d) * inv_n
    yn = d * lax.rsqrt(var + GN_EPS) * ln_w + ln_b
    bonus = head_sums(r * k2 * r_k) * v
    y_ref[0] = ((yn + bonus) * g).astype(BF16)


RWKV_BLOCK_CHUNKS = 4


def _rwkv(ps3, mu, vec, wa, g2, bd):
    B, S, _ = ps3.shape
    rows = RWKV_BLOCK_CHUNKS * CHUNK
    nc = S // rows
    rows8 = rows // SUBLANES
    return pl.pallas_call(
        _rwkv_kernel,
        grid=(B, nc),
        in_specs=[pl.BlockSpec((1, rows, SHIFT_COLS), lambda b, c: (b, c, 0)),
                  pl.BlockSpec((1, SUBLANES, SHIFT_COLS), lambda b, c: (b, jnp.maximum(c * rows8 - 1, 0), 0)),
                  pl.BlockSpec((1, SHIFT_COLS), lambda b, c: (0, 0)),
                  pl.BlockSpec((SUBLANES, RWKV_WIDTH), lambda b, c: (0, 0)),
                  pl.BlockSpec((LANES, 2 * RWKV_WIDTH), lambda b, c: (0, 0)),
                  pl.BlockSpec((GATE_LORA, RWKV_WIDTH), lambda b, c: (0, 0)),
                  pl.BlockSpec((LANES, LANES), lambda b, c: (0, 0))],
        out_specs=pl.BlockSpec((1, rows, RWKV_WIDTH), lambda b, c: (b, c, 0)),
        out_shape=jax.ShapeDtypeStruct((B, S, RWKV_WIDTH), BF16),
        scratch_shapes=[pltpu.VMEM((N_PAIRS, PAIR, LANES), F32)],
        compiler_params=_cparams(("parallel", "arbitrary")),
        name="rwkv7",
    )(ps3, ps3, mu, vec, wa, g2, bd)


def _bias_kernel(rbt_ref, o_ref):
    e = pl.program_id(1)
    xi = lax.broadcasted_iota(jnp.int32, (REL_ROWS, BIAS_BASE), 1)
    ji = lax.broadcasted_iota(jnp.int32, (REL_ROWS, BIAS_BASE), 0)
    off = jnp.where(xi < BIAS_BASE - CHUNK, xi, xi - BIAS_BASE)
    idx = jnp.clip(e * CHUNK - off, -REL_CLIP, REL_CLIP) + REL_CLIP
    onehot = jnp.where(idx == ji, 1.0, 0.0).astype(BF16)
    base = _dot_exact_rhs(rbt_ref[0], onehot, 3)
    kj = lax.broadcasted_iota(jnp.int32, (CHUNK, ATTN_WINDOW), 1)
    kc = kj >> CHUNK_LOG2
    valid = (kc <= e) & (kc >= e - N_LEFT_CHUNKS)
    for h in range(ATTN_HEADS):
        rows = jnp.broadcast_to(base[h:h + 1, :], (CHUNK, BIAS_BASE))
        toep = pltpu.roll(rows, 0, 1, stride=1, stride_axis=0)
        o_ref[0, 0, h * CHUNK:(h + 1) * CHUNK, :] = jnp.where(valid, toep[:, :ATTN_WINDOW], MASK_VALUE)


def _bias_tables(rbt):
    L = rbt.shape[0]
    return pl.pallas_call(
        _bias_kernel,
        grid=(L, N_BIAS_TABLES),
        in_specs=[pl.BlockSpec((1, 8, REL_ROWS), lambda l, e: (l, 0, 0))],
        out_specs=pl.BlockSpec((1, 1, ATTN_HEADS * CHUNK, ATTN_WINDOW), lambda l, e: (l, e, 0, 0)),
        out_shape=jax.ShapeDtypeStruct((L, N_BIAS_TABLES, ATTN_HEADS * CHUNK, ATTN_WINDOW), F32),
        compiler_params=_cparams(("parallel", "parallel")),
        name="bias_tables",
    )(rbt)


ATTN_BLOCK_CHUNKS = 8


def _attn_kernel(q_ref, k_ref, v_ref, *rest):
    bias_refs, g_ref, o_ref = rest[:ATTN_BLOCK_CHUNKS], rest[-2], rest[-1]
    n0 = pl.program_id(1) * ATTN_BLOCK_CHUNKS
    q = q_ref[0] * jnp.asarray(HEAD_DIM ** -0.5, BF16)
    m1 = lax.broadcasted_iota(jnp.int32, (CHUNK, LANES), 1) < HEAD_DIM
    zero = jnp.zeros((), BF16)
    n_pairs = ATTN_WIDTH // LANES
    units = []
    for j in range(ATTN_BLOCK_CHUNKS):
        start = pl.multiple_of(jnp.maximum(n0 + j - (N_LEFT_CHUNKS + 1), 0) * CHUNK, CHUNK)
        kw = k_ref[0, pl.ds(start, ATTN_WINDOW), :]
        vw = v_ref[0, pl.ds(start, ATTN_WINDOW), :]
        qj = q[CHUNK * j:CHUNK * (j + 1)]
        for pi in range(n_pairs):
            sl = slice(LANES * pi, LANES * (pi + 1))
            qp = qj[:, sl]
            qs = jnp.concatenate([jnp.where(m1, qp, zero), jnp.where(m1, zero, qp)], axis=0)
            units.append(dict(qs=qs, k=kw[:, sl], v=vw[:, sl],
                              bias=bias_refs[j][0, 0, PAIR * pi:PAIR * (pi + 1), :]))
    for u in units:
        u["s"] = lax.dot_general(u["qs"], u["k"], (((1,), (1,)), ((), ())),
                                 preferred_element_type=F32) + u["bias"]
    for u in units:
        s = u["s"]
        ex = jnp.exp(s - jnp.max(s, axis=1, keepdims=True))
        u["den"] = jnp.sum(ex, axis=1, keepdims=True)
        u["ex"] = ex.astype(BF16)
    for u in units:
        o = jnp.dot(u["ex"], u["v"], preferred_element_type=F32) / u["den"]
        u["o"] = jnp.where(m1, o[:CHUNK], o[CHUNK:])
    rows = [jnp.concatenate([u["o"] for u in units[j * n_pairs:(j + 1) * n_pairs]], axis=1)
            for j in range(ATTN_BLOCK_CHUNKS)]
    o = jnp.concatenate(rows, axis=0)
    o_ref[0] = _rms(o, g_ref[...]).astype(BF16)


def _attn(qkv3, bias_l, l, g):
    B, S, _ = qkv3.shape
    rows = ATTN_BLOCK_CHUNKS * CHUNK
    nc = S // rows

    def bias_spec(j):
        return pl.BlockSpec(
            (1, 1, ATTN_HEADS * CHUNK, ATTN_WINDOW),
            lambda b, n: (l, jnp.minimum(n * ATTN_BLOCK_CHUNKS + j, N_BIAS_TABLES - 1), 0, 0))

    return pl.pallas_call(
        _attn_kernel,
        grid=(B, nc),
        in_specs=[pl.BlockSpec((1, rows, ATTN_WIDTH), lambda b, n: (b, n, 0)),
                  pl.BlockSpec((1, S, ATTN_WIDTH), lambda b, n: (b, 0, 1)),
                  pl.BlockSpec((1, S, ATTN_WIDTH), lambda b, n: (b, 0, 2))]
                 + [bias_spec(j) for j in range(ATTN_BLOCK_CHUNKS)]
                 + [pl.BlockSpec((1, ATTN_WIDTH), lambda b, n: (0, 0))],
        out_specs=pl.BlockSpec((1, rows, ATTN_WIDTH), lambda b, n: (b, n, 0)),
        out_shape=jax.ShapeDtypeStruct((B, S, ATTN_WIDTH), BF16),
        compiler_params=_cparams(("parallel", "arbitrary")),
        name="chunk_attn",
    )(qkv3, qkv3, qkv3, *([bias_l] * ATTN_BLOCK_CHUNKS), g)


def _outproj_kernel(yr_ref, ya_ref, x_ref, w_ref, g_ref, router_ref, xo_ref, h_ref, ridx_ref, rgate_ref):
    y = jnp.concatenate([yr_ref[...], ya_ref[...]], axis=1)
    xn = x_ref[...] + jnp.dot(y, w_ref[...], preferred_element_type=F32)
    xo_ref[...] = xn
    h = _rms(xn, g_ref[...])
    h_ref[...] = h.astype(BF16)
    lane = lax.broadcasted_iota(jnp.int32, ridx_ref.shape, 1)
    neg = jnp.asarray(-jnp.inf, F32)
    lg = jnp.where(lane < N_EXPERTS, _dot_f32(h, router_ref[...]), neg)
    top1 = jnp.max(lg, axis=1, keepdims=True)
    idx1 = jnp.min(jnp.where(lg == top1, lane, LANES), axis=1, keepdims=True)
    lg2 = jnp.where(lane == idx1, neg, lg)
    top2 = jnp.max(lg2, axis=1, keepdims=True)
    idx2 = jnp.min(jnp.where(lg2 == top2, lane, LANES), axis=1, keepdims=True)
    ex = jnp.exp(top2 - top1)
    ridx_ref[...] = jnp.where(lane == 0, idx1, jnp.where(lane == 1, idx2, 0))
    rgate_ref[...] = jnp.where(lane == 0, 1.0 / (1.0 + ex), jnp.where(lane == 1, ex / (1.0 + ex), 0.0))


def _outproj(yr, ya, x2, w, g, router, tm):
    T = x2.shape[0]
    row_tile = lambda width: pl.BlockSpec((tm, width), lambda i: (i, 0))
    return pl.pallas_call(
        _outproj_kernel,
        grid=(T // tm,),
        in_specs=[row_tile(RWKV_WIDTH), row_tile(ATTN_WIDTH), row_tile(D_MODEL),
                  pl.BlockSpec((D_MODEL, D_MODEL), lambda i: (0, 0)),
                  pl.BlockSpec((1, D_MODEL), lambda i: (0, 0)),
                  pl.BlockSpec((D_MODEL, LANES), lambda i: (0, 0))],
        out_specs=[row_tile(D_MODEL), row_tile(D_MODEL), row_tile(LANES), row_tile(LANES)],
        out_shape=[jax.ShapeDtypeStruct((T, D_MODEL), F32), jax.ShapeDtypeStruct((T, D_MODEL), BF16),
                   jax.ShapeDtypeStruct((T, LANES), jnp.int32), jax.ShapeDtypeStruct((T, LANES), F32)],
        compiler_params=_cparams(("parallel",)),
        name="outproj_router",
    )(yr, ya, x2, w, g, router)


FF_TILE = 256


def _ffn_kernel(yr_ref, ya_ref, x_ref, wo_ref, g_ref, wg_ref, wu_ref, wd_ref, *rest, final):
    if final:
        gf_ref, o_ref = rest
    else:
        (o_ref,) = rest
    y = jnp.concatenate([yr_ref[...], ya_ref[...]], axis=1)
    xn = x_ref[...] + jnp.dot(y, wo_ref[...], preferred_element_type=F32)
    h = _rms(xn, g_ref[...]).astype(BF16)
    acc = None
    for f in range(0, wg_ref.shape[1], FF_TILE):
        gate = jnp.dot(h, wg_ref[:, f:f + FF_TILE], preferred_element_type=F32)
        up = jnp.dot(h, wu_ref[:, f:f + FF_TILE], preferred_element_type=F32)
        act = (gate * _sigmoid(gate) * up).astype(BF16)
        d = jnp.dot(act, wd_ref[f:f + FF_TILE, :], preferred_element_type=F32)
        acc = d if acc is None else acc + d
    xo = xn + acc
    o_ref[...] = _rms(xo, gf_ref[...]) if final else xo


def _ffn(yr, ya, x2, w_out, g, wg, wu, wd, final_g, tm):
    T = x2.shape[0]
    F = wg.shape[1]
    final = final_g is not None
    in_specs = [pl.BlockSpec((tm, RWKV_WIDTH), lambda i: (i, 0)),
                pl.BlockSpec((tm, ATTN_WIDTH), lambda i: (i, 0)),
                pl.BlockSpec((tm, D_MODEL), lambda i: (i, 0)),
                pl.BlockSpec((D_MODEL, D_MODEL), lambda i: (0, 0)),
                pl.BlockSpec((1, D_MODEL), lambda i: (0, 0)),
                pl.BlockSpec((D_MODEL, F), lambda i: (0, 0)),
                pl.BlockSpec((D_MODEL, F), lambda i: (0, 0)),
                pl.BlockSpec((F, D_MODEL), lambda i: (0, 0))]
    args = [yr, ya, x2, w_out, g, wg, wu, wd]
    if final:
        in_specs.append(pl.BlockSpec((1, D_MODEL), lambda i: (0, 0)))
        args.append(final_g)
    return pl.pallas_call(
        functools.partial(_ffn_kernel, final=final),
        grid=(T // tm,),
        in_specs=in_specs,
        out_specs=pl.BlockSpec((tm, D_MODEL), lambda i: (i, 0)),
        out_shape=jax.ShapeDtypeStruct((T, D_MODEL), F32),
        compiler_params=_cparams(("parallel",)),
        name="ffn_dense",
    )(*args)


MOE_TILE = 512
MOE_TOK_TILE = 256
SEG_ALIGN = 8
MOE_LOCAL_ROWS = 640
SEG_PIECES = (256, 128, 64, 32, 16, 8)
GAP_PIECES = tuple(b for b in SEG_PIECES if b < MOE_TILE)
WAIT_PIECES = (2 * MOE_TOK_TILE,) + SEG_PIECES
TOP_K = 2


def _moe_offsets(ridx, T):
    nt = T // MOE_TOK_TILE
    e12 = ridx[:, :TOP_K]
    onehot = (e12[:, :, None] == jnp.arange(N_EXPERTS, dtype=jnp.int32)[None, None, :]).astype(jnp.int32)
    cnt = onehot.sum(axis=1).reshape(nt, MOE_TOK_TILE, N_EXPERTS).sum(axis=1)
    pc = ((cnt + SEG_ALIGN - 1) // SEG_ALIGN) * SEG_ALIGN
    loff = jnp.cumsum(pc, axis=1) - pc
    tot = pc.sum(axis=0)
    grp = ((tot + MOE_TILE - 1) // MOE_TILE) * MOE_TILE
    gend = jnp.cumsum(grp)
    gstart = gend - grp
    goff = gstart[None, :] + jnp.cumsum(pc, axis=0) - pc
    n_tiles = -(-(TOP_K * T + (SEG_ALIGN - 1) * N_EXPERTS * nt) // MOE_TILE) + N_EXPERTS
    tile_start = jnp.arange(n_tiles, dtype=jnp.int32) * MOE_TILE
    tile_exp = jnp.minimum(jnp.sum((tile_start[:, None] >= gend[None, :]).astype(jnp.int32), axis=1),
                           N_EXPERTS - 1)
    tile_rows = jnp.clip((gstart + tot)[tile_exp] - tile_start, 0, MOE_TILE)
    flat = lambda t: t.reshape(-1).astype(jnp.int32)
    used_tiles = gend[-1] // MOE_TILE
    gaps = jnp.concatenate([gstart + tot, grp - tot, jnp.stack([used_tiles, n_tiles - used_tiles])])
    return (flat(loff), flat(goff), flat(pc), flat(gaps), tile_exp.astype(jnp.int32),
            tile_rows.astype(jnp.int32), n_tiles)


def _local_positions(ridx, loff_ref, j):
    rows = ridx.shape[0]
    lane = lax.broadcasted_iota(jnp.int32, (rows, LANES), 1)
    oh0 = lane == ridx[:, 0:1]
    oh1 = lane == ridx[:, 1:2]
    ti = lax.broadcasted_iota(jnp.int32, (rows, rows), 0)
    tj = lax.broadcasted_iota(jnp.int32, (rows, rows), 1)
    before = jnp.where(tj < ti, 1.0, 0.0).astype(BF16)
    f0 = jnp.where(oh0, 1.0, 0.0)
    f1 = jnp.where(oh1, 1.0, 0.0)
    pre0 = jnp.dot(before, f0.astype(BF16), preferred_element_type=F32)
    pre1 = jnp.dot(before, f1.astype(BF16), preferred_element_type=F32)
    c0 = jnp.sum(f0, axis=0, keepdims=True)
    lane1 = lax.broadcasted_iota(jnp.int32, (1, LANES), 1)
    loff = jnp.zeros((1, LANES), F32)
    for e in range(N_EXPERTS):
        loff = jnp.where(lane1 == e, loff_ref[j * N_EXPERTS + e].astype(F32), loff)
    pos0 = jnp.sum(jnp.where(oh0, loff + pre0, 0.0), axis=1, keepdims=True)
    pos1 = jnp.sum(jnp.where(oh1, loff + c0 + pre1, 0.0), axis=1, keepdims=True)
    return pos0, pos1


def _segment_copies(j, pc_ref, loff_ref, goff_ref, local_ref, hbm_ref, sem, to_hbm):
    out = []
    for e in range(N_EXPERTS):
        n = pc_ref[j * N_EXPERTS + e]
        lo = loff_ref[j * N_EXPERTS + e]
        go = goff_ref[j * N_EXPERTS + e]
        for b in SEG_PIECES:
            done = n & ~(2 * b - 1)
            loc = local_ref.at[pl.ds(pl.multiple_of(lo + done, SEG_ALIGN), b), :]
            glob = hbm_ref.at[pl.ds(pl.multiple_of(go + done, SEG_ALIGN), b), :]
            cp = pltpu.make_async_copy(loc, glob, sem) if to_hbm else pltpu.make_async_copy(glob, loc, sem)
            out.append(((n & b) != 0, cp))
    return out


def _start_segments(*args):
    for cond, cp in _segment_copies(*args):
        pl.when(cond)(cp.start)


def _wait_segments(j, pc_ref, loff_ref, goff_ref, local_ref, hbm_ref, sem, to_hbm):
    del goff_ref
    last = j * N_EXPERTS + N_EXPERTS - 1
    total = loff_ref[last] + pc_ref[last]
    for b in WAIT_PIECES:
        loc = local_ref.at[pl.ds(0, b), :]
        glob = hbm_ref.at[pl.ds(0, b), :]
        cp = pltpu.make_async_copy(loc, glob, sem) if to_hbm else pltpu.make_async_copy(glob, loc, sem)
        pl.when((total & b) != 0)(cp.wait)


def _gap_copies(gaps_ref, zero_ref, xs_hbm, sem):
    out = []
    for e in range(N_EXPERTS):
        start = gaps_ref[e]
        n = gaps_ref[N_EXPERTS + e]
        for b in GAP_PIECES:
            done = n & ~(2 * b - 1)
            dst = xs_hbm.at[pl.ds(pl.multiple_of(start + done, SEG_ALIGN), b), :]
            out.append(((n & b) != 0, pltpu.make_async_copy(zero_ref.at[pl.ds(0, b), :], dst, sem)))
    return out


def _zero_fill_gaps(gaps_ref, zero_ref, xs_hbm, sem):
    zero_ref[...] = jnp.zeros(zero_ref.shape, F32)
    first_tile = gaps_ref[2 * N_EXPERTS]
    n_tail = gaps_ref[2 * N_EXPERTS + 1]

    def tail_copy(i):
        row0 = pl.multiple_of((first_tile + i) * MOE_TILE, MOE_TILE)
        return pltpu.make_async_copy(zero_ref, xs_hbm.at[pl.ds(row0, MOE_TILE), :], sem)

    for cond, cp in _gap_copies(gaps_ref, zero_ref, xs_hbm, sem):
        pl.when(cond)(cp.start)
    lax.fori_loop(0, n_tail, lambda i, c: (tail_copy(i).start(), c)[1], 0)
    for cond, cp in _gap_copies(gaps_ref, zero_ref, xs_hbm, sem):
        pl.when(cond)(cp.wait)
    lax.fori_loop(0, n_tail, lambda i, c: (tail_copy(i).wait(), c)[1], 0)


def _moe_dispatch_kernel(loff_ref, goff_ref, pc_ref, gaps_ref, ridx_ref, h_ref, xs_hbm, local_ref, zero_ref,
                         sem, zsem):
    j = pl.program_id(0)
    n = pl.num_programs(0)
    slot = j % 2

    @pl.when(j == 0)
    def _gaps():
        _zero_fill_gaps(gaps_ref, zero_ref, xs_hbm, zsem.at[0])

    pos0, pos1 = _local_positions(ridx_ref[...], loff_ref, j)
    col = lax.broadcasted_iota(jnp.int32, (MOE_TOK_TILE, MOE_LOCAL_ROWS), 1).astype(F32)
    place = jnp.where((col == pos0) | (col == pos1), 1.0, 0.0).astype(BF16)
    xs = lax.dot_general(place, h_ref[...], (((0,), (0,)), ((), ())), preferred_element_type=F32)

    def seg(jj, s):
        return (jj, pc_ref, loff_ref, goff_ref, local_ref.at[s], xs_hbm, sem.at[s], True)

    @pl.when(j >= 2)
    def _reuse():
        _wait_segments(*seg(j - 2, slot))

    local_ref[slot] = xs
    _start_segments(*seg(j, slot))

    @pl.when(j == n - 1)
    def _drain():
        _wait_segments(*seg(j, slot))

        @pl.when(n >= 2)
        def _():
            _wait_segments(*seg(j - 1, 1 - slot))


def _moe_dispatch(ridx, h, loff, goff, pc, gaps, n_rows):
    T = h.shape[0]
    grid_spec = pltpu.PrefetchScalarGridSpec(
        num_scalar_prefetch=4,
        grid=(T // MOE_TOK_TILE,),
        in_specs=[pl.BlockSpec((MOE_TOK_TILE, LANES), lambda j, *_: (j, 0)),
                  pl.BlockSpec((MOE_TOK_TILE, D_MODEL), lambda j, *_: (j, 0))],
        out_specs=pl.BlockSpec(memory_space=pl.ANY),
        scratch_shapes=[pltpu.VMEM((2, MOE_LOCAL_ROWS, D_MODEL), F32), pltpu.VMEM((MOE_TILE, D_MODEL), F32),
                        pltpu.SemaphoreType.DMA((2,)), pltpu.SemaphoreType.DMA((1,))])
    return pl.pallas_call(
        _moe_dispatch_kernel,
        grid_spec=grid_spec,
        out_shape=jax.ShapeDtypeStruct((n_rows, D_MODEL), F32),
        compiler_params=_cparams(("arbitrary",)),
        name="moe_dispatch",
    )(loff, goff, pc, gaps, ridx, h)


def _moe_expert_kernel(texp_ref, trows_ref, xs_ref, wg_ref, wu_ref, wd_ref, ys_ref):
    del texp_ref
    rows = trows_ref[pl.program_id(0)]

    @pl.when(rows > 0)
    def _compute():
        x = xs_ref[...].astype(BF16)
        gate = jnp.dot(x, wg_ref[0].astype(BF16), preferred_element_type=F32)
        up = jnp.dot(x, wu_ref[0].astype(BF16), preferred_element_type=F32)
        act = (gate * _sigmoid(gate) * up).astype(BF16)
        ys_ref[...] = jnp.dot(act, wd_ref[0].astype(BF16), preferred_element_type=F32)

    @pl.when(rows == 0)
    def _skip():
        ys_ref[...] = jnp.zeros(ys_ref.shape, F32)


def _moe_experts(xs, tile_exp, tile_rows, wg, wu, wd):
    n_tiles = tile_exp.shape[0]
    _, _, F = wg.shape
    grid_spec = pltpu.PrefetchScalarGridSpec(
        num_scalar_prefetch=2,
        grid=(n_tiles,),
        in_specs=[pl.BlockSpec((MOE_TILE, D_MODEL), lambda i, te, tr: (i, 0)),
                  pl.BlockSpec((1, D_MODEL, F), lambda i, te, tr: (te[i], 0, 0)),
                  pl.BlockSpec((1, D_MODEL, F), lambda i, te, tr: (te[i], 0, 0)),
                  pl.BlockSpec((1, F, D_MODEL), lambda i, te, tr: (te[i], 0, 0))],
        out_specs=pl.BlockSpec((MOE_TILE, D_MODEL), lambda i, te, tr: (i, 0)))
    return pl.pallas_call(
        _moe_expert_kernel,
        grid_spec=grid_spec,
        out_shape=jax.ShapeDtypeStruct((n_tiles * MOE_TILE, D_MODEL), F32),
        compiler_params=_cparams(("arbitrary",)),
        name="moe_experts",
    )(tile_exp, tile_rows, xs, wg, wu, wd)


def _moe_combine_kernel(loff_ref, goff_ref, pc_ref, ridx_ref, rgate_ref, x_ref, ys_hbm, *rest, final):
    if final:
        gf_ref, o_ref, local_ref, sem = rest
    else:
        o_ref, local_ref, sem = rest
    j = pl.program_id(0)
    n = pl.num_programs(0)
    slot = j % 2

    def seg(jj, s):
        return (jj, pc_ref, loff_ref, goff_ref, local_ref.at[s], ys_hbm, sem.at[s], False)

    @pl.when(j == 0)
    def _prologue():
        _start_segments(*seg(j, slot))

    @pl.when(j + 1 < n)
    def _prefetch():
        _start_segments(*seg(j + 1, 1 - slot))

    pos0, pos1 = _local_positions(ridx_ref[...], loff_ref, j)
    col = lax.broadcasted_iota(jnp.int32, (MOE_TOK_TILE, MOE_LOCAL_ROWS), 1).astype(F32)
    pick0 = jnp.where(col == pos0, 1.0, 0.0).astype(BF16)
    pick1 = jnp.where(col == pos1, 1.0, 0.0).astype(BF16)
    used = loff_ref[j * N_EXPERTS + N_EXPERTS - 1] + pc_ref[j * N_EXPERTS + N_EXPERTS - 1]
    _wait_segments(*seg(j, slot))
    ri = lax.broadcasted_iota(jnp.int32, (MOE_LOCAL_ROWS, D_MODEL), 0)
    ys = jnp.where(ri < used, local_ref[slot], 0.0).astype(BF16)
    y0 = jnp.dot(pick0, ys, preferred_element_type=F32)
    y1 = jnp.dot(pick1, ys, preferred_element_type=F32)
    gates = rgate_ref[...]
    xo = x_ref[...] + gates[:, 0:1] * y0 + gates[:, 1:2] * y1
    o_ref[...] = _rms(xo, gf_ref[...]) if final else xo


def _moe_combine(x2, ys, ridx, rgate, loff, goff, pc, final_g):
    T = x2.shape[0]
    final = final_g is not None
    in_specs = [pl.BlockSpec((MOE_TOK_TILE, LANES), lambda j, *_: (j, 0)),
                pl.BlockSpec((MOE_TOK_TILE, LANES), lambda j, *_: (j, 0)),
                pl.BlockSpec((MOE_TOK_TILE, D_MODEL), lambda j, *_: (j, 0)),
                pl.BlockSpec(memory_space=pl.ANY)]
    args = [ridx, rgate, x2, ys]
    if final:
        in_specs.append(pl.BlockSpec((1, D_MODEL), lambda j, *_: (0, 0)))
        args.append(final_g)
    grid_spec = pltpu.PrefetchScalarGridSpec(
        num_scalar_prefetch=3,
        grid=(T // MOE_TOK_TILE,),
        in_specs=in_specs,
        out_specs=pl.BlockSpec((MOE_TOK_TILE, D_MODEL), lambda j, *_: (j, 0)),
        scratch_shapes=[pltpu.VMEM((2, MOE_LOCAL_ROWS, D_MODEL), F32), pltpu.SemaphoreType.DMA((2,))])
    return pl.pallas_call(
        functools.partial(_moe_combine_kernel, final=final),
        grid_spec=grid_spec,
        out_shape=jax.ShapeDtypeStruct((T, D_MODEL), F32),
        compiler_params=_cparams(("arbitrary",)),
        name="moe_combine",
    )(loff, goff, pc, *args)


def _moe(h, x2, ridx, rgate, wg, wu, wd, final_g):
    T = x2.shape[0]
    loff, goff, pc, gaps, tile_exp, tile_rows, n_tiles = _moe_offsets(ridx, T)
    xs = _moe_dispatch(ridx, h, loff, goff, pc, gaps, n_tiles * MOE_TILE)
    ys = _moe_experts(xs, tile_exp, tile_rows, wg, wu, wd)
    return _moe_combine(x2, ys, ridx, rgate, loff, goff, pc, final_g)


def kernel(x, norm_mix_g, w_in, shift_mu, rwkv_w0, rwkv_w2, rwkv_a0, rwkv_a2, rwkv_g2, rwkv_k_k, rwkv_k_a, rwkv_r_k, rwkv_ln_w, rwkv_ln_b, attn_rel_bias, attn_norm_g, w_out, norm_ffn_g, ffn_w_gate, ffn_w_up, ffn_w_down, moe_router, moe_w_gate, moe_w_up, moe_w_down, norm_final_g):
    B, S, D = x.shape
    depth = w_in.shape[0]
    T = B * S
    tm = min(512, T)
    row = lambda t: t.reshape(1, -1).astype(F32)

    hi = jnp.arange(LANES) // HEAD_DIM
    bd = (hi[:, None] == hi[None, :]).astype(BF16)
    rbt = jnp.pad(jnp.swapaxes(attn_rel_bias, 1, 2).astype(F32),
                  ((0, 0), (0, 0), (0, REL_ROWS - attn_rel_bias.shape[1])))
    bias_tabs = _bias_tables(rbt)

    x2 = x.reshape(T, D).astype(F32)
    for l in range(depth):
        ps, qkv = _inproj(x2, row(norm_mix_g[l]), w_in.astype(F32), l, tm)

        zeros = jnp.zeros((DECAY_LORA, RWKV_WIDTH), F32)
        wa = jnp.concatenate([jnp.concatenate([rwkv_w2[l], zeros], axis=1),
                              jnp.concatenate([zeros, rwkv_a2[l]], axis=1)], axis=0).astype(BF16)
        vec = jnp.stack([rwkv_w0[l], rwkv_a0[l], rwkv_k_k[l], rwkv_k_a[l], rwkv_r_k[l],
                         rwkv_ln_w[l], rwkv_ln_b[l], jnp.zeros_like(rwkv_w0[l])]).astype(F32)
        y_rwkv = _rwkv(ps.reshape(B, S, SHIFT_COLS), row(shift_mu[l]), vec, wa,
                       rwkv_g2[l].astype(BF16), bd)
        y_attn = _attn(qkv.reshape(B, S, 3 * ATTN_WIDTH), bias_tabs, l, row(attn_norm_g[l]))

        li = l // 2
        final_g = row(norm_final_g) if l == depth - 1 else None
        mixed = (y_rwkv.reshape(T, RWKV_WIDTH), y_attn.reshape(T, ATTN_WIDTH), x2,
                 w_out[l].astype(BF16), row(norm_ffn_g[l]))
        if l % 2 == 1:
            router = jnp.pad(moe_router[li].astype(F32), ((0, 0), (0, LANES - N_EXPERTS)))
            x_mid, h, ridx, rgate = _outproj(*mixed, router, tm)
            x2 = _moe(h, x_mid, ridx, rgate, moe_w_gate[li].astype(F32), moe_w_up[li].astype(F32),
                      moe_w_down[li].astype(F32), final_g)
        else:
            x2 = _ffn(*mixed, ffn_w_gate[li].astype(BF16), ffn_w_up[li].astype(BF16),
                      ffn_w_down[li].astype(BF16), final_g, tm)
    return x2.reshape(B, S, D).astype(x.dtype)
```

```python
import functools

import jax
import jax.numpy as jnp
from jax import lax
from jax.experimental import pallas as pl
from jax.experimental.pallas import tpu as pltpu

F32 = jnp.float32
BF16 = jnp.bfloat16

D_MODEL = 1024
CHUNK_LOG2 = 6
CHUNK = 1 << CHUNK_LOG2
N_LEFT_CHUNKS = 8
HEAD_DIM = 64
RWKV_WIDTH = 512
ATTN_WIDTH = 512
DECAY_LORA = 64
AAA_LORA = 64
GATE_LORA = 128
REL_CLIP = 128
N_EXPERTS = 8
RMS_EPS = 1e-6
GN_EPS = 64e-5
MASK_VALUE = -1e30
DECAY_SCALE = 0.6065306597126334
SHIFT_COLS = 3 * RWKV_WIDTH + DECAY_LORA + AAA_LORA + GATE_LORA

LANES = 128
SUBLANES = 8
ATTN_HEADS = ATTN_WIDTH // HEAD_DIM
PAIR = 2 * CHUNK
N_PAIRS = RWKV_WIDTH // LANES
ATTN_WINDOW = (N_LEFT_CHUNKS + 2) * CHUNK
N_BIAS_TABLES = N_LEFT_CHUNKS + 2
BIAS_BASE = 768
REL_ROWS = 384
VMEM_LIMIT = 56 * 1024 * 1024


def _cparams(sem):
    return pltpu.CompilerParams(dimension_semantics=sem, vmem_limit_bytes=VMEM_LIMIT)


def _mm(a, b):
    return jnp.dot(a.astype(BF16), b.astype(BF16), preferred_element_type=F32)


def _mm_nt(a, b):
    return lax.dot_general(a.astype(BF16), b.astype(BF16), (((1,), (1,)), ((), ())),
                           preferred_element_type=F32)


def _mm_tn(a, b):
    return lax.dot_general(a.astype(BF16), b.astype(BF16), (((0,), (0,)), ((), ())),
                           preferred_element_type=F32)


def _split_terms(x, n):
    terms, rem = [], x
    for _ in range(n):
        hi = rem.astype(BF16)
        terms.append(hi)
        rem = rem - hi.astype(F32)
    return terms


def _dot_exact_rhs(x, w_bf16, n):
    acc = None
    for t in _split_terms(x, n):
        d = jnp.dot(t, w_bf16, preferred_element_type=F32)
        acc = d if acc is None else acc + d
    return acc


def _dot_exact_lhs(w_bf16, x, n):
    acc = None
    for t in _split_terms(x, n):
        d = jnp.dot(w_bf16, t, preferred_element_type=F32)
        acc = d if acc is None else acc + d
    return acc


def _dot_f32(a, b):
    a1, a2 = _split_terms(a, 2)
    b1, b2 = _split_terms(b, 2)
    n = b.shape[1]
    t = jnp.dot(a1, jnp.concatenate([b1, b2], axis=1), preferred_element_type=F32)
    return t[:, :n] + t[:, n:] + jnp.dot(a2, b1, preferred_element_type=F32)


def _sigmoid(x):
    return 1.0 / (1.0 + jnp.exp(-x))


def _rms(x, g):
    return x * lax.rsqrt(jnp.mean(x * x, axis=-1, keepdims=True) + RMS_EPS) * g


INPROJ_ROWS = 1024


def _inproj_kernel(x_ref, g_ref, w_ref, ps_ref, qkv_ref):
    hb = _rms(x_ref[...], g_ref[...]).astype(BF16)
    ps_ref[...] = jnp.dot(hb, w_ref[0, :, :SHIFT_COLS].astype(BF16), preferred_element_type=F32)
    qkv_ref[...] = jnp.dot(hb, w_ref[0, :, SHIFT_COLS:].astype(BF16), preferred_element_type=F32).astype(BF16)


def _inproj(x2, g, w_in, l, tm):
    T = x2.shape[0]
    n_in = w_in.shape[2]
    na = n_in - SHIFT_COLS
    return pl.pallas_call(
        _inproj_kernel,
        grid=(T // tm,),
        in_specs=[pl.BlockSpec((tm, D_MODEL), lambda i: (i, 0)),
                  pl.BlockSpec((1, D_MODEL), lambda i: (0, 0)),
                  pl.BlockSpec((1, D_MODEL, n_in), lambda i: (l, 0, 0), pipeline_mode=pl.Buffered(1))],
        out_specs=[pl.BlockSpec((tm, SHIFT_COLS), lambda i: (i, 0)),
                   pl.BlockSpec((tm, na), lambda i: (i, 0))],
        out_shape=[jax.ShapeDtypeStruct((T, SHIFT_COLS), F32),
                   jax.ShapeDtypeStruct((T, na), BF16)],
        compiler_params=_cparams(("parallel",)),
        name="inproj",
    )(x2, g, w_in)


def _rwkv_kernel(ps_ref, prev_ref, mu_ref, vec_ref, wa_ref, g2_ref, bd_ref, y_ref, h_ref):
    c = pl.program_id(1)

    @pl.when(c == 0)
    def _init():
        h_ref[...] = jnp.zeros(h_ref.shape, F32)

    p = ps_ref[0]
    last = jnp.where(c > 0, prev_ref[0][SUBLANES - 1:SUBLANES, :], 0.0)
    row = lax.broadcasted_iota(jnp.int32, p.shape, 0)
    prev = jnp.where(row == 0, last, pltpu.roll(p, 1, 0))
    xs = p + mu_ref[...] * (prev - p)

    W = RWKV_WIDTH
    r, k, v = xs[:, 0:W], xs[:, W:2 * W], xs[:, 2 * W:3 * W]
    z0 = xs[:, 3 * W:3 * W + LANES]
    gd = xs[:, 3 * W + LANES:3 * W + 2 * LANES]
    m1 = lax.broadcasted_iota(jnp.int32, (CHUNK, LANES), 1) < HEAD_DIM
    z0 = jnp.where(lax.broadcasted_iota(jnp.int32, z0.shape, 1) < DECAY_LORA, jnp.tanh(z0), z0)
    lora = _mm(z0, wa_ref[...])
    vec = vec_ref[...]
    w0, a0, k_k, k_a, r_k, ln_w, ln_b = (vec[i:i + 1] for i in range(7))
    w = w0 + lora[:, :W]
    a = _sigmoid(a0 + lora[:, W:])
    g = _mm(_sigmoid(gd), g2_ref[...])
    lw = -DECAY_SCALE * _sigmoid(w)

    bd = bd_ref[...]

    def head_sums(x):
        xb = x.astype(BF16)
        return jnp.concatenate(
            [jnp.dot(xb[:, LANES * i:LANES * (i + 1)], bd, preferred_element_type=F32) for i in range(N_PAIRS)],
            axis=1)

    kk = k * k_k
    kk = kk / jnp.maximum(jnp.sqrt(head_sums(kk * kk)), 1e-12)
    k2 = k * (1.0 + (a - 1.0) * k_a)
    kka = kk * a

    rows = p.shape[0]
    n_chunks = rows // CHUNK
    ti = lax.broadcasted_iota(jnp.int32, (rows, rows), 0)
    tj = lax.broadcasted_iota(jnp.int32, (rows, rows), 1)
    tri = jnp.where((ti >= tj) & ((ti >> CHUNK_LOG2) == (tj >> CHUNK_LOG2)), 1.0, 0.0).astype(BF16)
    L = _dot_exact_lhs(tri, lw, 2)

    ri = lax.broadcasted_iota(jnp.int32, (PAIR, PAIR), 0)
    ci = lax.broadcasted_iota(jnp.int32, (PAIR, PAIR), 1)
    same_head = (ri >> CHUNK_LOG2) == (ci >> CHUNK_LOG2)
    strict = same_head & (ri > ci)
    incl = same_head & (ri >= ci)
    eye = ri == ci
    eye_f = jnp.where(eye, 1.0, 0.0)

    units = []
    for j in range(n_chunks):
        rs = slice(CHUNK * j, CHUNK * (j + 1))
        Lj, lwj = L[rs], lw[rs]
        Lc = Lj[CHUNK - 1:CHUNK]
        inv = jnp.exp(-Lj)
        gC = jnp.exp(Lc)
        to_end = gC * inv
        Rt = r[rs] * jnp.exp(Lj)
        At = -kk[rs] * jnp.exp(Lj - lwj)
        Bt, Kt = kka[rs] * inv, k2[rs] * inv
        Bh, Kh = kka[rs] * to_end, k2[rs] * to_end
        vj = v[rs]
        for pi in range(N_PAIRS):
            sl = slice(LANES * pi, LANES * (pi + 1))

            def stack(x):
                xp = x[:, sl]
                return jnp.concatenate([jnp.where(m1, xp, 0.0), jnp.where(m1, 0.0, xp)], axis=0)

            def twice(x):
                xp = x[:, sl].astype(BF16)
                return jnp.concatenate([xp, xp], axis=0)

            units.append(dict(
                j=j, pi=pi, gC=gC[:, sl], sRt=stack(Rt),
                sAt=stack(At).astype(BF16), sV=stack(vj).astype(BF16),
                sBt=twice(Bt), sKt=twice(Kt), sBh=twice(Bh), sKh=twice(Kh)))

    for u in units:
        big = _mm_nt(jnp.concatenate([u["sAt"], u["sRt"].astype(BF16)], axis=0),
                     jnp.concatenate([u["sBt"], u["sKt"]], axis=0))
        u["AB"] = jnp.where(strict, big[:PAIR, :PAIR], 0.0)
        u["AK"] = jnp.where(strict, big[:PAIR, PAIR:], 0.0)
        u["RB"] = jnp.where(incl, big[PAIR:, :PAIR], 0.0)
        u["RK"] = jnp.where(incl, big[PAIR:, PAIR:], 0.0)
    for u in units:
        u["X"] = eye_f + u["AB"]
        u["Pw"] = _mm(u["AB"], u["AB"])
        u["W1"] = _mm(u["AK"], u["sV"])
    for _ in range(4):
        for u in units:
            PX = _mm(u["Pw"], jnp.concatenate([u["Pw"], u["X"]], axis=1))
            u["Pw"] = PX[:, :PAIR]
            u["X"] = u["X"] + PX[:, PAIR:]
    for u in units:
        u["Tm"] = u["X"] + _mm(u["Pw"], u["X"])
    for u in units:
        u["PQ"] = _mm(u["Tm"], jnp.concatenate([u["sAt"], u["W1"].astype(BF16)], axis=1)).astype(BF16)
    for u in units:
        PQ = u["PQ"]
        Pm, Q = PQ[:, :PAIR], PQ[:, PAIR:]
        RBPQ = _mm(u["RB"], PQ)
        u["Rp"] = u["sRt"] + RBPQ[:, :PAIR]
        u["Y0"] = RBPQ[:, PAIR:] + _mm(u["RK"], u["sV"])
        u["Mm"] = jnp.where(eye, u["gC"], 0.0) + jnp.where(same_head, _mm_tn(u["sBh"], Pm), 0.0)
        u["G"] = jnp.where(same_head, _mm_tn(jnp.concatenate([u["sBh"], u["sKh"]], axis=0),
                                             jnp.concatenate([Q, u["sV"]], axis=0)), 0.0)
    H = [h_ref[pi] for pi in range(N_PAIRS)]
    y_rows = []
    for j in range(n_chunks):
        ys = []
        for u in units[j * N_PAIRS:(j + 1) * N_PAIRS]:
            pi = u["pi"]
            YH = _mm(jnp.concatenate([u["Rp"], u["Mm"]], axis=0), H[pi])
            Ysm = YH[:PAIR] + u["Y0"]
            H[pi] = YH[PAIR:] + u["G"]
            ys.append(Ysm[:CHUNK] + Ysm[CHUNK:])
        y_rows.append(jnp.concatenate(ys, axis=1))
    for pi in range(N_PAIRS):
        h_ref[pi] = H[pi]
    y = y_rows[0] if n_chunks == 1 else jnp.concatenate(y_rows, axis=0)

    inv_n = 1.0 / HEAD_DIM
    mean = head_sums(y) * inv_n
    d = y - mean
    var = head_sums(d * d) * inv_n
    yn = d * lax.rsqrt(var + GN_EPS) * ln_w + ln_b
    bonus = head_sums(r * k2 * r_k) * v
    y_ref[0] = ((yn + bonus) * g).astype(BF16)


RWKV_BLOCK_CHUNKS = 4


def _rwkv(ps3, mu, vec, wa, g2, bd):
    B, S, _ = ps3.shape
    rows = RWKV_BLOCK_CHUNKS * CHUNK
    nc = S // rows
    rows8 = rows // SUBLANES
    return pl.pallas_call(
        _rwkv_kernel,
        grid=(B, nc),
        in_specs=[pl.BlockSpec((1, rows, SHIFT_COLS), lambda b, c: (b, c, 0)),
                  pl.BlockSpec((1, SUBLANES, SHIFT_COLS), lambda b, c: (b, jnp.maximum(c * rows8 - 1, 0), 0)),
                  pl.BlockSpec((1, SHIFT_COLS), lambda b, c: (0, 0)),
                  pl.BlockSpec((SUBLANES, RWKV_WIDTH), lambda b, c: (0, 0)),
                  pl.BlockSpec((LANES, 2 * RWKV_WIDTH), lambda b, c: (0, 0)),
                  pl.BlockSpec((GATE_LORA, RWKV_WIDTH), lambda b, c: (0, 0)),
                  pl.BlockSpec((LANES, LANES), lambda b, c: (0, 0))],
        out_specs=pl.BlockSpec((1, rows, RWKV_WIDTH), lambda b, c: (b, c, 0)),
        out_shape=jax.ShapeDtypeStruct((B, S, RWKV_WIDTH), BF16),
        scratch_shapes=[pltpu.VMEM((N_PAIRS, PAIR, LANES), F32)],
        compiler_params=_cparams(("parallel", "arbitrary")),
        name="rwkv7",
    )(ps3, ps3, mu, vec, wa, g2, bd)


def _bias_kernel(rbt_ref, o_ref):
    e = pl.program_id(1)
    xi = lax.broadcasted_iota(jnp.int32, (REL_ROWS, BIAS_BASE), 1)
    ji = lax.broadcasted_iota(jnp.int32, (REL_ROWS, BIAS_BASE), 0)
    off = jnp.where(xi < BIAS_BASE - CHUNK, xi, xi - BIAS_BASE)
    idx = jnp.clip(e * CHUNK - off, -REL_CLIP, REL_CLIP) + REL_CLIP
    onehot = jnp.where(idx == ji, 1.0, 0.0).astype(BF16)
    base = _dot_exact_rhs(rbt_ref[0], onehot, 3)
    kj = lax.broadcasted_iota(jnp.int32, (CHUNK, ATTN_WINDOW), 1)
    kc = kj >> CHUNK_LOG2
    valid = (kc <= e) & (kc >= e - N_LEFT_CHUNKS)
    for h in range(ATTN_HEADS):
        rows = jnp.broadcast_to(base[h:h + 1, :], (CHUNK, BIAS_BASE))
        toep = pltpu.roll(rows, 0, 1, stride=1, stride_axis=0)
        o_ref[0, 0, h * CHUNK:(h + 1) * CHUNK, :] = jnp.where(valid, toep[:, :ATTN_WINDOW], MASK_VALUE)


def _bias_tables(rbt):
    L = rbt.shape[0]
    return pl.pallas_call(
        _bias_kernel,
        grid=(L, N_BIAS_TABLES),
        in_specs=[pl.BlockSpec((1, 8, REL_ROWS), lambda l, e: (l, 0, 0))],
        out_specs=pl.BlockSpec((1, 1, ATTN_HEADS * CHUNK, ATTN_WINDOW), lambda l, e: (l, e, 0, 0)),
        out_shape=jax.ShapeDtypeStruct((L, N_BIAS_TABLES, ATTN_HEADS * CHUNK, ATTN_WINDOW), F32),
        compiler_params=_cparams(("parallel", "parallel")),
        name="bias_tables",
    )(rbt)


ATTN_BLOCK_CHUNKS = 8


def _attn_kernel(q_ref, k_ref, v_ref, *rest):
    bias_refs, g_ref, o_ref = rest[:ATTN_BLOCK_CHUNKS], rest[-2], rest[-1]
    n0 = pl.program_id(1) * ATTN_BLOCK_CHUNKS
    q = q_ref[0] * jnp.asarray(HEAD_DIM ** -0.5, BF16)
    m1 = lax.broadcasted_iota(jnp.int32, (CHUNK, LANES), 1) < HEAD_DIM
    zero = jnp.zeros((), BF16)
    n_pairs = ATTN_WIDTH // LANES
    units = []
    for j in range(ATTN_BLOCK_CHUNKS):
        start = pl.multiple_of(jnp.maximum(n0 + j - (N_LEFT_CHUNKS + 1), 0) * CHUNK, CHUNK)
        kw = k_ref[0, pl.ds(start, ATTN_WINDOW), :]
        vw = v_ref[0, pl.ds(start, ATTN_WINDOW), :]
        qj = q[CHUNK * j:CHUNK * (j + 1)]
        for pi in range(n_pairs):
            sl = slice(LANES * pi, LANES * (pi + 1))
            qp = qj[:, sl]
            qs = jnp.concatenate([jnp.where(m1, qp, zero), jnp.where(m1, zero, qp)], axis=0)
            units.append(dict(qs=qs, k=kw[:, sl], v=vw[:, sl],
                              bias=bias_refs[j][0, 0, PAIR * pi:PAIR * (pi + 1), :]))
    for u in units:
        u["s"] = lax.dot_general(u["qs"], u["k"], (((1,), (1,)), ((), ())),
                                 preferred_element_type=F32) + u["bias"]
    for u in units:
        s = u["s"]
        ex = jnp.exp(s - jnp.max(s, axis=1, keepdims=True))
        u["den"] = jnp.sum(ex, axis=1, keepdims=True)
        u["ex"] = ex.astype(BF16)
    for u in units:
        o = jnp.dot(u["ex"], u["v"], preferred_element_type=F32) / u["den"]
        u["o"] = jnp.where(m1, o[:CHUNK], o[CHUNK:])
    rows = [jnp.concatenate([u["o"] for u in units[j * n_pairs:(j + 1) * n_pairs]], axis=1)
            for j in range(ATTN_BLOCK_CHUNKS)]
    o = jnp.concatenate(rows, axis=0)
    o_ref[0] = _rms(o, g_ref[...]).astype(BF16)


def _attn(qkv3, bias_l, l, g):
    B, S, _ = qkv3.shape
    rows = ATTN_BLOCK_CHUNKS * CHUNK
    nc = S // rows

    def bias_spec(j):
        return pl.BlockSpec(
            (1, 1, ATTN_HEADS * CHUNK, ATTN_WINDOW),
            lambda b, n: (l, jnp.minimum(n * ATTN_BLOCK_CHUNKS + j, N_BIAS_TABLES - 1), 0, 0))

    return pl.pallas_call(
        _attn_kernel,
        grid=(B, nc),
        in_specs=[pl.BlockSpec((1, rows, ATTN_WIDTH), lambda b, n: (b, n, 0)),
                  pl.BlockSpec((1, S, ATTN_WIDTH), lambda b, n: (b, 0, 1)),
                  pl.BlockSpec((1, S, ATTN_WIDTH), lambda b, n: (b, 0, 2))]
                 + [bias_spec(j) for j in range(ATTN_BLOCK_CHUNKS)]
                 + [pl.BlockSpec((1, ATTN_WIDTH), lambda b, n: (0, 0))],
        out_specs=pl.BlockSpec((1, rows, ATTN_WIDTH), lambda b, n: (b, n, 0)),
        out_shape=jax.ShapeDtypeStruct((B, S, ATTN_WIDTH), BF16),
        compiler_params=_cparams(("parallel", "arbitrary")),
        name="chunk_attn",
    )(qkv3, qkv3, qkv3, *([bias_l] * ATTN_BLOCK_CHUNKS), g)


def _outproj_kernel(yr_ref, ya_ref, x_ref, w_ref, g_ref, router_ref, xo_ref, h_ref, ridx_ref, rgate_ref):
    y = jnp.concatenate([yr_ref[...], ya_ref[...]], axis=1)
    xn = x_ref[...] + jnp.dot(y, w_ref[...], preferred_element_type=F32)
    xo_ref[...] = xn
    h = _rms(xn, g_ref[...])
    h_ref[...] = h.astype(BF16)
    lane = lax.broadcasted_iota(jnp.int32, ridx_ref.shape, 1)
    neg = jnp.asarray(-jnp.inf, F32)
    lg = jnp.where(lane < N_EXPERTS, _dot_f32(h, router_ref[...]), neg)
    top1 = jnp.max(lg, axis=1, keepdims=True)
    idx1 = jnp.min(jnp.where(lg == top1, lane, LANES), axis=1, keepdims=True)
    lg2 = jnp.where(lane == idx1, neg, lg)
    top2 = jnp.max(lg2, axis=1, keepdims=True)
    idx2 = jnp.min(jnp.where(lg2 == top2, lane, LANES), axis=1, keepdims=True)
    ex = jnp.exp(top2 - top1)
    ridx_ref[...] = jnp.where(lane == 0, idx1, jnp.where(lane == 1, idx2, 0))
    rgate_ref[...] = jnp.where(lane == 0, 1.0 / (1.0 + ex), jnp.where(lane == 1, ex / (1.0 + ex), 0.0))


def _outproj(yr, ya, x2, w, g, router, tm):
    T = x2.shape[0]
    row_tile = lambda width: pl.BlockSpec((tm, width), lambda i: (i, 0))
    return pl.pallas_call(
        _outproj_kernel,
        grid=(T // tm,),
        in_specs=[row_tile(RWKV_WIDTH), row_tile(ATTN_WIDTH), row_tile(D_MODEL),
                  pl.BlockSpec((D_MODEL, D_MODEL), lambda i: (0, 0)),
                  pl.BlockSpec((1, D_MODEL), lambda i: (0, 0)),
                  pl.BlockSpec((D_MODEL, LANES), lambda i: (0, 0))],
        out_specs=[row_tile(D_MODEL), row_tile(D_MODEL), row_tile(LANES), row_tile(LANES)],
        out_shape=[jax.ShapeDtypeStruct((T, D_MODEL), F32), jax.ShapeDtypeStruct((T, D_MODEL), BF16),
                   jax.ShapeDtypeStruct((T, LANES), jnp.int32), jax.ShapeDtypeStruct((T, LANES), F32)],
        compiler_params=_cparams(("parallel",)),
        name="outproj_router",
    )(yr, ya, x2, w, g, router)


FF_TILE = 256


def _ffn_kernel(yr_ref, ya_ref, x_ref, wo_ref, g_ref, wg_ref, wu_ref, wd_ref, *rest, final):
    if final:
        gf_ref, o_ref = rest
    else:
        (o_ref,) = rest
    y = jnp.concatenate([yr_ref[...], ya_ref[...]], axis=1)
    xn = x_ref[...] + jnp.dot(y, wo_ref[...], preferred_element_type=F32)
    h = _rms(xn, g_ref[...]).astype(BF16)
    acc = None
    for f in range(0, wg_ref.shape[1], FF_TILE):
        gate = jnp.dot(h, wg_ref[:, f:f + FF_TILE], preferred_element_type=F32)
        up = jnp.dot(h, wu_ref[:, f:f + FF_TILE], preferred_element_type=F32)
        act = (gate * _sigmoid(gate) * up).astype(BF16)
        d = jnp.dot(act, wd_ref[f:f + FF_TILE, :], preferred_element_type=F32)
        acc = d if acc is None else acc + d
    xo = xn + acc
    o_ref[...] = _rms(xo, gf_ref[...]) if final else xo


def _ffn(yr, ya, x2, w_out, g, wg, wu, wd, final_g, tm):
    T = x2.shape[0]
    F = wg.shape[1]
    final = final_g is not None
    in_specs = [pl.BlockSpec((tm, RWKV_WIDTH), lambda i: (i, 0)),
                pl.BlockSpec((tm, ATTN_WIDTH), lambda i: (i, 0)),
                pl.BlockSpec((tm, D_MODEL), lambda i: (i, 0)),
                pl.BlockSpec((D_MODEL, D_MODEL), lambda i: (0, 0)),
                pl.BlockSpec((1, D_MODEL), lambda i: (0, 0)),
                pl.BlockSpec((D_MODEL, F), lambda i: (0, 0)),
                pl.BlockSpec((D_MODEL, F), lambda i: (0, 0)),
                pl.BlockSpec((F, D_MODEL), lambda i: (0, 0))]
    args = [yr, ya, x2, w_out, g, wg, wu, wd]
    if final:
        in_specs.append(pl.BlockSpec((1, D_MODEL), lambda i: (0, 0)))
        args.append(final_g)
    return pl.pallas_call(
        functools.partial(_ffn_kernel, final=final),
        grid=(T // tm,),
        in_specs=in_specs,
        out_specs=pl.BlockSpec((tm, D_MODEL), lambda i: (i, 0)),
        out_shape=jax.ShapeDtypeStruct((T, D_MODEL), F32),
        compiler_params=_cparams(("parallel",)),
        name="ffn_dense",
    )(*args)


MOE_TILE = 512
MOE_TOK_TILE = 256
SEG_ALIGN = 8
MOE_LOCAL_ROWS = 640
SEG_PIECES = (256, 128, 64, 32, 16, 8)
GAP_PIECES = tuple(b for b in SEG_PIECES if b < MOE_TILE)
WAIT_PIECES = (2 * MOE_TOK_TILE,) + SEG_PIECES
TOP_K = 2


def _moe_offsets(ridx, T):
    nt = T // MOE_TOK_TILE
    e12 = ridx[:, :TOP_K]
    onehot = (e12[:, :, None] == jnp.arange(N_EXPERTS, dtype=jnp.int32)[None, None, :]).astype(jnp.int32)
    cnt = onehot.sum(axis=1).reshape(nt, MOE_TOK_TILE, N_EXPERTS).sum(axis=1)
    pc = ((cnt + SEG_ALIGN - 1) // SEG_ALIGN) * SEG_ALIGN
    loff = jnp.cumsum(pc, axis=1) - pc
    tot = pc.sum(axis=0)
    grp = ((tot + MOE_TILE - 1) // MOE_TILE) * MOE_TILE
    gend = jnp.cumsum(grp)
    gstart = gend - grp
    goff = gstart[None, :] + jnp.cumsum(pc, axis=0) - pc
    n_tiles = -(-(TOP_K * T + (SEG_ALIGN - 1) * N_EXPERTS * nt) // MOE_TILE) + N_EXPERTS
    tile_start = jnp.arange(n_tiles, dtype=jnp.int32) * MOE_TILE
    tile_exp = jnp.minimum(jnp.sum((tile_start[:, None] >= gend[None, :]).astype(jnp.int32), axis=1),
                           N_EXPERTS - 1)
    tile_rows = jnp.clip((gstart + tot)[tile_exp] - tile_start, 0, MOE_TILE)
    flat = lambda t: t.reshape(-1).astype(jnp.int32)
    used_tiles = gend[-1] // MOE_TILE
    gaps = jnp.concatenate([gstart + tot, grp - tot, jnp.stack([used_tiles, n_tiles - used_tiles])])
    return (flat(loff), flat(goff), flat(pc), flat(gaps), tile_exp.astype(jnp.int32),
            tile_rows.astype(jnp.int32), n_tiles)


def _local_positions(ridx, loff_ref, j):
    rows = ridx.shape[0]
    lane = lax.broadcasted_iota(jnp.int32, (rows, LANES), 1)
    oh0 = lane == ridx[:, 0:1]
    oh1 = lane == ridx[:, 1:2]
    ti = lax.broadcasted_iota(jnp.int32, (rows, rows), 0)
    tj = lax.broadcasted_iota(jnp.int32, (rows, rows), 1)
    before = jnp.where(tj < ti, 1.0, 0.0).astype(BF16)
    f0 = jnp.where(oh0, 1.0, 0.0)
    f1 = jnp.where(oh1, 1.0, 0.0)
    pre0 = jnp.dot(before, f0.astype(BF16), preferred_element_type=F32)
    pre1 = jnp.dot(before, f1.astype(BF16), preferred_element_type=F32)
    c0 = jnp.sum(f0, axis=0, keepdims=True)
    lane1 = lax.broadcasted_iota(jnp.int32, (1, LANES), 1)
    loff = jnp.zeros((1, LANES), F32)
    for e in range(N_EXPERTS):
        loff = jnp.where(lane1 == e, loff_ref[j * N_EXPERTS + e].astype(F32), loff)
    pos0 = jnp.sum(jnp.where(oh0, loff + pre0, 0.0), axis=1, keepdims=True)
    pos1 = jnp.sum(jnp.where(oh1, loff + c0 + pre1, 0.0), axis=1, keepdims=True)
    return pos0, pos1


def _segment_copies(j, pc_ref, loff_ref, goff_ref, local_ref, hbm_ref, sem, to_hbm):
    out = []
    for e in range(N_EXPERTS):
        n = pc_ref[j * N_EXPERTS + e]
        lo = loff_ref[j * N_EXPERTS + e]
        go = goff_ref[j * N_EXPERTS + e]
        for b in SEG_PIECES:
            done = n & ~(2 * b - 1)
            loc = local_ref.at[pl.ds(pl.multiple_of(lo + done, SEG_ALIGN), b), :]
            glob = hbm_ref.at[pl.ds(pl.multiple_of(go + done, SEG_ALIGN), b), :]
            cp = pltpu.make_async_copy(loc, glob, sem) if to_hbm else pltpu.make_async_copy(glob, loc, sem)
            out.append(((n & b) != 0, cp))
    return out


def _start_segments(*args):
    for cond, cp in _segment_copies(*args):
        pl.when(cond)(cp.start)


def _wait_segments(j, pc_ref, loff_ref, goff_ref, local_ref, hbm_ref, sem, to_hbm):
    del goff_ref
    last = j * N_EXPERTS + N_EXPERTS - 1
    total = loff_ref[last] + pc_ref[last]
    for b in WAIT_PIECES:
        loc = local_ref.at[pl.ds(0, b), :]
        glob = hbm_ref.at[pl.ds(0, b), :]
        cp = pltpu.make_async_copy(loc, glob, sem) if to_hbm else pltpu.make_async_copy(glob, loc, sem)
        pl.when((total & b) != 0)(cp.wait)


def _gap_copies(gaps_ref, zero_ref, xs_hbm, sem):
    out = []
    for e in range(N_EXPERTS):
        start = gaps_ref[e]
        n = gaps_ref[N_EXPERTS + e]
        for b in GAP_PIECES:
            done = n & ~(2 * b - 1)
            dst = xs_hbm.at[pl.ds(pl.multiple_of(start + done, SEG_ALIGN), b), :]
            out.append(((n & b) != 0, pltpu.make_async_copy(zero_ref.at[pl.ds(0, b), :], dst, sem)))
    return out


def _zero_fill_gaps(gaps_ref, zero_ref, xs_hbm, sem):
    zero_ref[...] = jnp.zeros(zero_ref.shape, F32)
    first_tile = gaps_ref[2 * N_EXPERTS]
    n_tail = gaps_ref[2 * N_EXPERTS + 1]

    def tail_copy(i):
        row0 = pl.multiple_of((first_tile + i) * MOE_TILE, MOE_TILE)
        return pltpu.make_async_copy(zero_ref, xs_hbm.at[pl.ds(row0, MOE_TILE), :], sem)

    for cond, cp in _gap_copies(gaps_ref, zero_ref, xs_hbm, sem):
        pl.when(cond)(cp.start)
    lax.fori_loop(0, n_tail, lambda i, c: (tail_copy(i).start(), c)[1], 0)
    for cond, cp in _gap_copies(gaps_ref, zero_ref, xs_hbm, sem):
        pl.when(cond)(cp.wait)
    lax.fori_loop(0, n_tail, lambda i, c: (tail_copy(i).wait(), c)[1], 0)


def _moe_dispatch_kernel(loff_ref, goff_ref, pc_ref, gaps_ref, ridx_ref, h_ref, xs_hbm, local_ref, zero_ref,
                         sem, zsem):
    j = pl.program_id(0)
    n = pl.num_programs(0)
    slot = j % 2

    @pl.when(j == 0)
    def _gaps():
        _zero_fill_gaps(gaps_ref, zero_ref, xs_hbm, zsem.at[0])

    pos0, pos1 = _local_positions(ridx_ref[...], loff_ref, j)
    col = lax.broadcasted_iota(jnp.int32, (MOE_TOK_TILE, MOE_LOCAL_ROWS), 1).astype(F32)
    place = jnp.where((col == pos0) | (col == pos1), 1.0, 0.0).astype(BF16)
    xs = lax.dot_general(place, h_ref[...], (((0,), (0,)), ((), ())), preferred_element_type=F32)

    def seg(jj, s):
        return (jj, pc_ref, loff_ref, goff_ref, local_ref.at[s], xs_hbm, sem.at[s], True)

    @pl.when(j >= 2)
    def _reuse():
        _wait_segments(*seg(j - 2, slot))

    local_ref[slot] = xs
    _start_segments(*seg(j, slot))

    @pl.when(j == n - 1)
    def _drain():
        _wait_segments(*seg(j, slot))

        @pl.when(n >= 2)
        def _():
            _wait_segments(*seg(j - 1, 1 - slot))


def _moe_dispatch(ridx, h, loff, goff, pc, gaps, n_rows):
    T = h.shape[0]
    grid_spec = pltpu.PrefetchScalarGridSpec(
        num_scalar_prefetch=4,
        grid=(T // MOE_TOK_TILE,),
        in_specs=[pl.BlockSpec((MOE_TOK_TILE, LANES), lambda j, *_: (j, 0)),
                  pl.BlockSpec((MOE_TOK_TILE, D_MODEL), lambda j, *_: (j, 0))],
        out_specs=pl.BlockSpec(memory_space=pl.ANY),
        scratch_shapes=[pltpu.VMEM((2, MOE_LOCAL_ROWS, D_MODEL), F32), pltpu.VMEM((MOE_TILE, D_MODEL), F32),
                        pltpu.SemaphoreType.DMA((2,)), pltpu.SemaphoreType.DMA((1,))])
    return pl.pallas_call(
        _moe_dispatch_kernel,
        grid_spec=grid_spec,
        out_shape=jax.ShapeDtypeStruct((n_rows, D_MODEL), F32),
        compiler_params=_cparams(("arbitrary",)),
        name="moe_dispatch",
    )(loff, goff, pc, gaps, ridx, h)


def _moe_expert_kernel(texp_ref, trows_ref, xs_ref, wg_ref, wu_ref, wd_ref, ys_ref):
    del texp_ref
    rows = trows_ref[pl.program_id(0)]

    @pl.when(rows > 0)
    def _compute():
        x = xs_ref[...].astype(BF16)
        gate = jnp.dot(x, wg_ref[0].astype(BF16), preferred_element_type=F32)
        up = jnp.dot(x, wu_ref[0].astype(BF16), preferred_element_type=F32)
        act = (gate * _sigmoid(gate) * up).astype(BF16)
        ys_ref[...] = jnp.dot(act, wd_ref[0].astype(BF16), preferred_element_type=F32)

    @pl.when(rows == 0)
    def _skip():
        ys_ref[...] = jnp.zeros(ys_ref.shape, F32)


def _moe_experts(xs, tile_exp, tile_rows, wg, wu, wd):
    n_tiles = tile_exp.shape[0]
    _, _, F = wg.shape
    grid_spec = pltpu.PrefetchScalarGridSpec(
        num_scalar_prefetch=2,
        grid=(n_tiles,),
        in_specs=[pl.BlockSpec((MOE_TILE, D_MODEL), lambda i, te, tr: (i, 0)),
                  pl.BlockSpec((1, D_MODEL, F), lambda i, te, tr: (te[i], 0, 0)),
                  pl.BlockSpec((1, D_MODEL, F), lambda i, te, tr: (te[i], 0, 0)),
                  pl.BlockSpec((1, F, D_MODEL), lambda i, te, tr: (te[i], 0, 0))],
        out_specs=pl.BlockSpec((MOE_TILE, D_MODEL), lambda i, te, tr: (i, 0)))
    return pl.pallas_call(
        _moe_expert_kernel,
        grid_spec=grid_spec,
        out_shape=jax.ShapeDtypeStruct((n_tiles * MOE_TILE, D_MODEL), F32),
        compiler_params=_cparams(("arbitrary",)),
        name="moe_experts",
    )(tile_exp, tile_rows, xs, wg, wu, wd)


def _moe_combine_kernel(loff_ref, goff_ref, pc_ref, ridx_ref, rgate_ref, x_ref, ys_hbm, *rest, final):
    if final:
        gf_ref, o_ref, local_ref, sem = rest
    else:
        o_ref, local_ref, sem = rest
    j = pl.program_id(0)
    n = pl.num_programs(0)
    slot = j % 2

    def seg(jj, s):
        return (jj, pc_ref, loff_ref, goff_ref, local_ref.at[s], ys_hbm, sem.at[s], False)

    @pl.when(j == 0)
    def _prologue():
        _start_segments(*seg(j, slot))

    @pl.when(j + 1 < n)
    def _prefetch():
        _start_segments(*seg(j + 1, 1 - slot))

    pos0, pos1 = _local_positions(ridx_ref[...], loff_ref, j)
    col = lax.broadcasted_iota(jnp.int32, (MOE_TOK_TILE, MOE_LOCAL_ROWS), 1).astype(F32)
    pick0 = jnp.where(col == pos0, 1.0, 0.0).astype(BF16)
    pick1 = jnp.where(col == pos1, 1.0, 0.0).astype(BF16)
    used = loff_ref[j * N_EXPERTS + N_EXPERTS - 1] + pc_ref[j * N_EXPERTS + N_EXPERTS - 1]
    _wait_segments(*seg(j, slot))
    ri = lax.broadcasted_iota(jnp.int32, (MOE_LOCAL_ROWS, D_MODEL), 0)
    ys = jnp.where(ri < used, local_ref[slot], 0.0).astype(BF16)
    y0 = jnp.dot(pick0, ys, preferred_element_type=F32)
    y1 = jnp.dot(pick1, ys, preferred_element_type=F32)
    gates = rgate_ref[...]
    xo = x_ref[...] + gates[:, 0:1] * y0 + gates[:, 1:2] * y1
    o_ref[...] = _rms(xo, gf_ref[...]) if final else xo


def _moe_combine(x2, ys, ridx, rgate, loff, goff, pc, final_g):
    T = x2.shape[0]
    final = final_g is not None
    in_specs = [pl.BlockSpec((MOE_TOK_TILE, LANES), lambda j, *_: (j, 0)),
                pl.BlockSpec((MOE_TOK_TILE, LANES), lambda j, *_: (j, 0)),
                pl.BlockSpec((MOE_TOK_TILE, D_MODEL), lambda j, *_: (j, 0)),
                pl.BlockSpec(memory_space=pl.ANY)]
    args = [ridx, rgate, x2, ys]
    if final:
        in_specs.append(pl.BlockSpec((1, D_MODEL), lambda j, *_: (0, 0)))
        args.append(final_g)
    grid_spec = pltpu.PrefetchScalarGridSpec(
        num_scalar_prefetch=3,
        grid=(T // MOE_TOK_TILE,),
        in_specs=in_specs,
        out_specs=pl.BlockSpec((MOE_TOK_TILE, D_MODEL), lambda j, *_: (j, 0)),
        scratch_shapes=[pltpu.VMEM((2, MOE_LOCAL_ROWS, D_MODEL), F32), pltpu.SemaphoreType.DMA((2,))])
    return pl.pallas_call(
        functools.partial(_moe_combine_kernel, final=final),
        grid_spec=grid_spec,
        out_shape=jax.ShapeDtypeStruct((T, D_MODEL), F32),
        compiler_params=_cparams(("arbitrary",)),
        name="moe_combine",
    )(loff, goff, pc, *args)


def _moe(h, x2, ridx, rgate, wg, wu, wd, final_g):
    T = x2.shape[0]
    loff, goff, pc, gaps, tile_exp, tile_rows, n_tiles = _moe_offsets(ridx, T)
    xs = _moe_dispatch(ridx, h, loff, goff, pc, gaps, n_tiles * MOE_TILE)
    ys = _moe_experts(xs, tile_exp, tile_rows, wg, wu, wd)
    return _moe_combine(x2, ys, ridx, rgate, loff, goff, pc, final_g)


def kernel(x, norm_mix_g, w_in, shift_mu, rwkv_w0, rwkv_w2, rwkv_a0, rwkv_a2, rwkv_g2, rwkv_k_k, rwkv_k_a, rwkv_r_k, rwkv_ln_w, rwkv_ln_b, attn_rel_bias, attn_norm_g, w_out, norm_ffn_g, ffn_w_gate, ffn_w_up, ffn_w_down, moe_router, moe_w_gate, moe_w_up, moe_w_down, norm_final_g):
    B, S, D = x.shape
    depth = w_in.shape[0]
    T = B * S
    tm = min(512, T)
    row = lambda t: t.reshape(1, -1).astype(F32)

    hi = jnp.arange(LANES) // HEAD_DIM
    bd = (hi[:, None] == hi[None, :]).astype(BF16)
    rbt = jnp.pad(jnp.swapaxes(attn_rel_bias, 1, 2).astype(F32),
                  ((0, 0), (0, 0), (0, REL_ROWS - attn_rel_bias.shape[1])))
    bias_tabs = _bias_tables(rbt)

    x2 = x.reshape(T, D).astype(F32)
    for l in range(depth):
        ps, qkv = _inproj(x2, row(norm_mix_g[l]), w_in.astype(F32), l, min(INPROJ_ROWS, T))

        zeros = jnp.zeros((DECAY_LORA, RWKV_WIDTH), F32)
        wa = jnp.concatenate([jnp.concatenate([rwkv_w2[l], zeros], axis=1),
                              jnp.concatenate([zeros, rwkv_a2[l]], axis=1)], axis=0).astype(BF16)
        vec = jnp.stack([rwkv_w0[l], rwkv_a0[l], rwkv_k_k[l], rwkv_k_a[l], rwkv_r_k[l],
                         rwkv_ln_w[l], rwkv_ln_b[l], jnp.zeros_like(rwkv_w0[l])]).astype(F32)
        y_rwkv = _rwkv(ps.reshape(B, S, SHIFT_COLS), row(shift_mu[l]), vec, wa,
                       rwkv_g2[l].astype(BF16), bd)
        y_attn = _attn(qkv.reshape(B, S, 3 * ATTN_WIDTH), bias_tabs, l, row(attn_norm_g[l]))

        li = l // 2
        final_g = row(norm_final_g) if l == depth - 1 else None
        mixed = (y_rwkv.reshape(T, RWKV_WIDTH), y_attn.reshape(T, ATTN_WIDTH), x2,
                 w_out[l].astype(BF16), row(norm_ffn_g[l]))
        if l % 2 == 1:
            router = jnp.pad(moe_router[li].astype(F32), ((0, 0), (0, LANES - N_EXPERTS)))
            x_mid, h, ridx, rgate = _outproj(*mixed, router, tm)
            x2 = _moe(h, x_mid, ridx, rgate, moe_w_gate[li].astype(F32), moe_w_up[li].astype(F32),
                      moe_w_down[li].astype(F32), final_g)
        else:
            x2 = _ffn(*mixed, ffn_w_gate[li].astype(BF16), ffn_w_up[li].astype(BF16),
                      ffn_w_down[li].astype(BF16), final_g, tm)
    return x2.reshape(B, S, D).astype(x.dtype)
```
